```python
import jax, jax.numpy as jnp
from jax import lax
import numpy as np

D_MODEL = 1024
BATCH = 8
SEQ = 2048
DEPTH = 2

HEAD_DIM = 64
N_HEADS = D_MODEL // HEAD_DIM
N_MEM_HEADS = 4
MEM_LEN = 256
DILATED_PATTERNS = ((128, 1), (512, 4), (2048, 16))
N_DIL_GROUPS = len(DILATED_PATTERNS)
DIL_HEADS_PER_GROUP = (N_HEADS - N_MEM_HEADS) // N_DIL_GROUPS
N_FOX_HEADS = N_HEADS - N_MEM_HEADS
Q_BLOCK = 128
N_A_LAYERS = DEPTH // 2
N_B_LAYERS = DEPTH - N_A_LAYERS
DIL_QKV_WIDTH = N_DIL_GROUPS * 3 * DIL_HEADS_PER_GROUP * HEAD_DIM
MEM_WIDTH = N_MEM_HEADS * HEAD_DIM
FOX_WIDTH = N_FOX_HEADS * HEAD_DIM
A_IN = DIL_QKV_WIDTH + MEM_WIDTH
A_OUT = DIL_HEADS_PER_GROUP * HEAD_DIM + MEM_WIDTH
B_IN = FOX_WIDTH + MEM_WIDTH
B_OUT = FOX_WIDTH + MEM_WIDTH
N_EXPERTS = 32
TOP_K = 4
D_FF = D_MODEL
SWIGLU_LIMIT = 7.0
SWIGLU_ALPHA = 1.702
MOE_BLOCK = 128
ROPE_THETA = 10000.0
LN_EPS = 1e-5
ATTN_SCALE = HEAD_DIM ** -0.5
DEEPNORM_ALPHA = (2.0 * DEPTH) ** 0.25
DEEPNORM_BETA = (8.0 * DEPTH) ** -0.25

kernel_name = 'yoco_dilated_fox_memory_moe_deepnorm'


def _layer_norm(x, g, b):
    xf = x.astype(jnp.float32)
    mu = jnp.mean(xf, axis=-1, keepdims=True)
    var = jnp.mean(jnp.square(xf - mu), axis=-1, keepdims=True)
    return ((xf - mu) * lax.rsqrt(var + LN_EPS)).astype(x.dtype) * g + b


def _rope_tables(seq, dtype):
    inv = 1.0 / (ROPE_THETA ** (jnp.arange(0, HEAD_DIM, 2, dtype=jnp.float32) / HEAD_DIM))
    ang = jnp.arange(seq, dtype=jnp.float32)[:, None] * inv[None, :]
    return jnp.cos(ang)[:, None, :].astype(dtype), jnp.sin(ang)[:, None, :].astype(dtype)


def _rope(t, cos, sin):
    t1, t2 = jnp.split(t, 2, axis=-1)
    return jnp.concatenate([t1 * cos - t2 * sin, t2 * cos + t1 * sin], axis=-1)


def _dilated_group_attention(q, k, v, dilation, steps):
    b, s, h, c = q.shape
    length = s // dilation
    nb = -(-length // steps)
    lp = nb * steps

    def to_blocks(t):
        t = t.reshape(b, length, dilation, h, c).transpose(0, 2, 3, 1, 4)
        t = jnp.pad(t, ((0, 0), (0, 0), (0, 0), (0, lp - length), (0, 0)))
        return t.reshape(b, dilation, h, nb, steps, c)

    def with_prev(t):
        prev = jnp.pad(t[:, :, :, :-1], ((0, 0), (0, 0), (0, 0), (1, 0), (0, 0), (0, 0)))
        return jnp.concatenate([prev, t], axis=4)

    qb, kb, vb = to_blocks(q), to_blocks(k), to_blocks(v)
    kk, vv = with_prev(kb), with_prev(vb)
    scores = jnp.einsum('brhnqc,brhnkc->brhnqk', qb, kk).astype(jnp.float32)
    qi = jnp.arange(steps)[:, None]
    kj = jnp.arange(2 * steps)[None, :]
    band = (kj >= qi) & (kj <= qi + steps)
    valid_start = (jnp.arange(nb)[:, None, None] > 0) | (kj >= steps)[None]
    mask = band[None] & valid_start
    scores = jnp.where(mask, scores, -jnp.inf)
    m = jnp.max(scores, axis=-1, keepdims=True)
    p = jnp.exp(scores - m)
    den = jnp.sum(p, axis=-1)
    o = jnp.einsum('brhnqk,brhnkc->brhnqc', p.astype(v.dtype), vv) / den[..., None].astype(v.dtype)
    lse = m[..., 0] + jnp.log(den)
    o = o.reshape(b, dilation, h, lp, c)[:, :, :, :length].transpose(0, 3, 1, 2, 4).reshape(b, s, h, c)
    lse = lse.reshape(b, dilation, h, lp)[..., :length].transpose(0, 3, 1, 2).reshape(b, s, h)
    return o, lse


def _forgetting_attention(q, k, v, log_f_cum):
    b, s, h, c = q.shape
    nb = s // Q_BLOCK
    c_all = log_f_cum.transpose(0, 2, 1)
    q_blocks = q.reshape(b, nb, Q_BLOCK, h, c).transpose(1, 0, 2, 3, 4)
    kpos = jnp.arange(s)

    def one_block(args):
        qb, blk = args
        start = blk * Q_BLOCK
        c_q = lax.dynamic_slice_in_dim(c_all, start, Q_BLOCK, axis=2)
        scores = (jnp.einsum('bqhc,bkhc->bhqk', qb, k).astype(jnp.float32)
                  + c_q[..., None] - c_all[:, :, None, :])
        causal = kpos[None, :] <= (start + jnp.arange(Q_BLOCK))[:, None]
        p = jax.nn.softmax(jnp.where(causal, scores, -jnp.inf), axis=-1)
        return jnp.einsum('bhqk,bkhc->bqhc', p.astype(v.dtype), v)

    out = lax.map(one_block, (q_blocks, jnp.arange(nb)))
    return out.transpose(1, 0, 2, 3, 4).reshape(b, s, h, c)


def _memory_attention(q, mem_k, mem_v):
    scores = jnp.einsum('bqhc,bmhc->bhqm', q, mem_k).astype(jnp.float32)
    p = jax.nn.softmax(scores, axis=-1)
    return jnp.einsum('bhqm,bmhc->bqhc', p.astype(mem_v.dtype), mem_v)


def _mixer_a(x, w_in, w_out, mem_k, mem_v, cos, sin):
    b, s, _ = x.shape
    h = x @ w_in
    qkv = h[..., :DIL_QKV_WIDTH].reshape(b, s, N_DIL_GROUPS, 3, DIL_HEADS_PER_GROUP, HEAD_DIM)
    q_mem = h[..., DIL_QKV_WIDTH:].reshape(b, s, N_MEM_HEADS, HEAD_DIM) * ATTN_SCALE
    outs, lses = [], []
    for g, (window, dilation) in enumerate(DILATED_PATTERNS):
        q = _rope(qkv[:, :, g, 0], cos, sin) * ATTN_SCALE
        k = _rope(qkv[:, :, g, 1], cos, sin)
        o, lse = _dilated_group_attention(q, k, qkv[:, :, g, 2], dilation, window // dilation)
        outs.append(o)
        lses.append(lse)
    wts = jax.nn.softmax(jnp.stack(lses), axis=0).astype(x.dtype)
    dil = jnp.einsum('gbsh,gbshc->bshc', wts, jnp.stack(outs))
    memo = _memory_attention(q_mem, mem_k, mem_v)
    merged = jnp.concatenate([dil.reshape(b, s, -1), memo.reshape(b, s, -1)], axis=-1)
    return merged @ w_out


def _mixer_b(x, w_in, w_out, mem_k, mem_v, k_sh, v_sh, log_f_cum):
    b, s, _ = x.shape
    h = x @ w_in
    q_fox = h[..., :FOX_WIDTH].reshape(b, s, N_FOX_HEADS, HEAD_DIM) * ATTN_SCALE
    q_mem = h[..., FOX_WIDTH:].reshape(b, s, N_MEM_HEADS, HEAD_DIM) * ATTN_SCALE
    fox = _forgetting_attention(q_fox, k_sh, v_sh, log_f_cum)
    memo = _memory_attention(q_mem, mem_k, mem_v)
    merged = jnp.concatenate([fox.reshape(b, s, -1), memo.reshape(b, s, -1)], axis=-1)
    return merged @ w_out


def _shared_kv(x, w_shared, b_forget):
    b, s, _ = x.shape
    h = x @ w_shared
    kv = h[..., :2 * FOX_WIDTH].reshape(b, s, 2, N_FOX_HEADS, HEAD_DIM)
    f_logit = (h[..., 2 * FOX_WIDTH:] + b_forget).astype(jnp.float32)
    log_f_cum = jnp.cumsum(jax.nn.log_sigmoid(f_logit), axis=1)
    return kv[:, :, 0], kv[:, :, 1], log_f_cum


def _moe(x2, w_r, b_r, w_gu, b_gu, w_dn, b_dn):
    n, d = x2.shape
    logits = (x2 @ w_r + b_r).astype(jnp.float32)
    top_v, top_i = lax.top_k(logits, TOP_K)
    gates = jax.nn.softmax(top_v, axis=-1).astype(x2.dtype)
    a = n * TOP_K
    e_flat = top_i.reshape(a)
    tok_flat = jnp.repeat(jnp.arange(n, dtype=jnp.int32), TOP_K)
    g_flat = gates.reshape(a)
    order = jnp.argsort(e_flat)
    e_s, tok_s, g_s = e_flat[order], tok_flat[order], g_flat[order]
    counts = jnp.zeros((N_EXPERTS,), jnp.int32).at[e_flat].add(1)
    padded = (counts + MOE_BLOCK - 1) // MOE_BLOCK * MOE_BLOCK
    pend = jnp.cumsum(padded)
    pstart = pend - padded
    ustart = jnp.cumsum(counts) - counts
    dest = pstart[e_s] + (jnp.arange(a, dtype=jnp.int32) - ustart[e_s])
    n_blocks = -(-a // MOE_BLOCK) + N_EXPERTS
    rows = n_blocks * MOE_BLOCK
    row_tok = jnp.full((rows,), n, jnp.int32).at[dest].set(tok_s)
    row_gate = jnp.zeros((rows,), x2.dtype).at[dest].set(g_s)
    block_exp = jnp.minimum(
        jnp.searchsorted(pend, jnp.arange(n_blocks, dtype=jnp.int32) * MOE_BLOCK, side='right'),
        N_EXPERTS - 1)
    x_pad = jnp.concatenate([x2, jnp.zeros((1, d), x2.dtype)], axis=0)

    def expert_block(args):
        toks, gts, e = args
        hb = x_pad[toks] @ w_gu[e] + b_gu[e]
        gate = jnp.minimum(hb[:, :D_FF], SWIGLU_LIMIT)
        up = jnp.clip(hb[:, D_FF:], -SWIGLU_LIMIT, SWIGLU_LIMIT)
        act = (up + 1.0) * gate * jax.nn.sigmoid(SWIGLU_ALPHA * gate)
        return (act @ w_dn[e] + b_dn[e]) * gts[:, None]

    y_rows = lax.map(expert_block, (row_tok.reshape(n_blocks, MOE_BLOCK),
                                    row_gate.reshape(n_blocks, MOE_BLOCK), block_exp))
    y = jax.ops.segment_sum(y_rows.reshape(rows, d), row_tok, num_segments=n + 1)
    return y[:n]


def setup_inputs(seed: int = 0) -> dict:
    key = jax.random.key(seed)
    ks = jax.random.split(key, 20)

    def nrm(k, shape, fan_in, gain=1.0):
        return jax.random.normal(k, shape, jnp.float32) * (gain * fan_in ** -0.5)

    x = jax.random.normal(ks[0], (BATCH, SEQ, D_MODEL), jnp.float32)
    mem = jax.random.normal(ks[1], (BATCH, MEM_LEN, D_MODEL), jnp.float32)
    w_in_a = nrm(ks[2], (N_A_LAYERS, D_MODEL, A_IN), D_MODEL)
    w_out_a = nrm(ks[3], (N_A_LAYERS, A_OUT, D_MODEL), A_OUT, DEEPNORM_BETA)
    w_in_b = nrm(ks[4], (N_B_LAYERS, D_MODEL, B_IN), D_MODEL)
    w_out_b = nrm(ks[5], (N_B_LAYERS, B_OUT, D_MODEL), B_OUT, DEEPNORM_BETA)
    w_shared_kvf = jnp.concatenate([nrm(ks[6], (D_MODEL, 2 * FOX_WIDTH), D_MODEL),
                                    nrm(ks[7], (D_MODEL, N_FOX_HEADS), D_MODEL, 0.5)], axis=1)
    b_forget = jax.random.uniform(ks[8], (N_FOX_HEADS,), jnp.float32, minval=1.0, maxval=4.0)
    w_mem_kv = nrm(ks[9], (DEPTH, D_MODEL, 2 * MEM_WIDTH), D_MODEL)
    ln_mix_g = 1.0 + 0.02 * jax.random.normal(ks[10], (DEPTH, D_MODEL), jnp.float32)
    ln_mix_b = 0.02 * jax.random.normal(ks[11], (DEPTH, D_MODEL), jnp.float32)
    ln_ffn_g = 1.0 + 0.02 * jax.random.normal(ks[12], (DEPTH, D_MODEL), jnp.float32)
    ln_ffn_b = 0.02 * jax.random.normal(ks[13], (DEPTH, D_MODEL), jnp.float32)
    w_router = nrm(ks[14], (DEPTH, D_MODEL, N_EXPERTS), D_MODEL)
    b_router = 0.01 * jax.random.normal(ks[15], (DEPTH, N_EXPERTS), jnp.float32)
    w_gate_up = nrm(ks[16], (DEPTH, N_EXPERTS, D_MODEL, 2 * D_FF), D_MODEL)
    b_gate_up = 0.01 * jax.random.normal(ks[17], (DEPTH, N_EXPERTS, 2 * D_FF), jnp.float32)
    w_down = nrm(ks[18], (DEPTH, N_EXPERTS, D_FF, D_MODEL), D_FF, DEEPNORM_BETA)
    b_down = 0.01 * jax.random.normal(ks[19], (DEPTH, N_EXPERTS, D_MODEL), jnp.float32)
    return {'x': x, 'mem': mem, 'w_in_a': w_in_a, 'w_out_a': w_out_a, 'w_in_b': w_in_b,
            'w_out_b': w_out_b, 'w_shared_kvf': w_shared_kvf, 'b_forget': b_forget,
            'w_mem_kv': w_mem_kv, 'ln_mix_g': ln_mix_g, 'ln_mix_b': ln_mix_b,
            'ln_ffn_g': ln_ffn_g, 'ln_ffn_b': ln_ffn_b, 'w_router': w_router,
            'b_router': b_router, 'w_gate_up': w_gate_up, 'b_gate_up': b_gate_up,
            'w_down': w_down, 'b_down': b_down}


def reference(x, mem, w_in_a, w_out_a, w_in_b, w_out_b, w_shared_kvf, b_forget, w_mem_kv,
              ln_mix_g, ln_mix_b, ln_ffn_g, ln_ffn_b, w_router, b_router, w_gate_up, b_gate_up,
              w_down, b_down):
    b, s, d = x.shape
    m_len = mem.shape[1]
    cos, sin = _rope_tables(s, x.dtype)
    shared = None
    for layer in range(DEPTH):
        mkv = jnp.einsum('bmd,de->bme', mem, w_mem_kv[layer]).reshape(b, m_len, 2, N_MEM_HEADS, HEAD_DIM)
        mem_k, mem_v = mkv[:, :, 0], mkv[:, :, 1]
        if layer < N_A_LAYERS:
            mix = _mixer_a(x, w_in_a[layer], w_out_a[layer], mem_k, mem_v, cos, sin)
        else:
            j = layer - N_A_LAYERS
            mix = _mixer_b(x, w_in_b[j], w_out_b[j], mem_k, mem_v, shared[0], shared[1], shared[2])
        x = _layer_norm(DEEPNORM_ALPHA * x + mix, ln_mix_g[layer], ln_mix_b[layer])
        ffn = _moe(x.reshape(b * s, d), w_router[layer], b_router[layer], w_gate_up[layer],
                   b_gate_up[layer], w_down[layer], b_down[layer]).reshape(b, s, d)
        x = _layer_norm(DEEPNORM_ALPHA * x + ffn, ln_ffn_g[layer], ln_ffn_b[layer])
        if layer == N_A_LAYERS - 1:
            shared = _shared_kv(x, w_shared_kvf, b_forget)
    return x
```

```python
import functools

import jax
import jax.numpy as jnp
from jax import lax
from jax.experimental import pallas as pl
from jax.experimental.pallas import tpu as pltpu

F32 = jnp.float32
BF16 = jnp.bfloat16

D_MODEL = 1024
DEPTH = 2
HEAD_DIM = 64
N_MEM_HEADS = 4
DILATED_PATTERNS = ((128, 1), (512, 4), (2048, 16))
N_DIL_GROUPS = 3
DIL_HEADS = 4
N_FOX_HEADS = 12
GROUP_W = DIL_HEADS * HEAD_DIM
MEM_W = N_MEM_HEADS * HEAD_DIM
FOX_W = N_FOX_HEADS * HEAD_DIM
DIL_QKV_W = N_DIL_GROUPS * 3 * GROUP_W
N_EXPERTS = 32
TOP_K = 4
D_FF = D_MODEL
SWIGLU_LIMIT = 7.0
SWIGLU_ALPHA = 1.702
ROPE_THETA = 10000.0
LN_EPS = 1e-5
ATTN_SCALE = HEAD_DIM ** -0.5
DEEPNORM_ALPHA = (2.0 * DEPTH) ** 0.25
DIL_STEPS = 128

LANES = 128
FOX_HEAD_W = LANES
FOX_QK_W = N_FOX_HEADS * FOX_HEAD_W
MOE_TM = 256
VMEM_LIMIT = 56 * 1024 * 1024


def _cparams(sem, vmem=None):
    return pltpu.CompilerParams(dimension_semantics=sem, vmem_limit_bytes=vmem)


def _head_mask(width, lo, hi):
    lane = lax.broadcasted_iota(jnp.int32, (1, width), 1)
    return (lane >= lo) & (lane < hi)


def _layer_norm_rows(y, g, b):
    mu = jnp.mean(y, axis=-1, keepdims=True)
    yc = y - mu
    var = jnp.mean(yc * yc, axis=-1, keepdims=True)
    return yc * lax.rsqrt(var + LN_EPS) * g + b


def _matmul_kernel(x_ref, w_ref, o_ref):
    o_ref[...] = jnp.dot(x_ref[...].astype(BF16), w_ref[...],
                         preferred_element_type=F32).astype(o_ref.dtype)


def _matmul(x, w, out_dtype, tm):
    m, k = x.shape
    n = w.shape[1]
    return pl.pallas_call(
        _matmul_kernel,
        grid=(m // tm,),
        in_specs=[pl.BlockSpec((tm, k), lambda i: (i, 0)),
                  pl.BlockSpec((k, n), lambda i: (0, 0))],
        out_specs=pl.BlockSpec((tm, n), lambda i: (i, 0)),
        out_shape=jax.ShapeDtypeStruct((m, n), out_dtype),
        compiler_params=_cparams(("parallel",)),
        name="matmul",
    )(x, w)


def _group_proj_kernel(x_ref, w_ref, cos_ref, sin_ref, *o_refs):
    h = jnp.dot(x_ref[0].astype(BF16), w_ref[...], preferred_element_type=F32)
    cos = cos_ref[...]
    sin = sin_ref[...]
    half = GROUP_W // 2
    for j in range(2):
        t1 = h[:, j * GROUP_W:j * GROUP_W + half]
        t2 = h[:, j * GROUP_W + half:(j + 1) * GROUP_W]
        o_refs[j][0, 0, :, :half] = (t1 * cos - t2 * sin).astype(BF16)
        o_refs[j][0, 0, :, half:] = (t2 * cos + t1 * sin).astype(BF16)
    for j in range(2, len(o_refs)):
        o_refs[j][0, 0] = h[:, j * GROUP_W:(j + 1) * GROUP_W].astype(BF16)


def _group_proj(x, w, cos_t, sin_t, dil):
    b, s, d = x.shape
    length = s // dil
    lt = min(length, 512)
    n_out = w.shape[1] // GROUP_W
    xv = x.reshape(b, length, dil * d)
    cv = cos_t.reshape(length, dil * LANES)
    sv = sin_t.reshape(length, dil * LANES)
    o_spec = pl.BlockSpec((1, 1, lt, GROUP_W), lambda bi, r, l: (bi, r, l, 0))
    return pl.pallas_call(
        _group_proj_kernel,
        grid=(b, dil, length // lt),
        in_specs=[pl.BlockSpec((1, lt, d), lambda bi, r, l: (bi, l, r)),
                  pl.BlockSpec(w.shape, lambda bi, r, l: (0, 0)),
                  pl.BlockSpec((lt, LANES), lambda bi, r, l: (l, r)),
                  pl.BlockSpec((lt, LANES), lambda bi, r, l: (l, r))],
        out_specs=[o_spec] * n_out,
        out_shape=[jax.ShapeDtypeStruct((b, dil, length, GROUP_W), BF16)] * n_out,
        compiler_params=_cparams(("parallel", "parallel", "parallel")),
        name=f"group_proj_d{dil}",
    )(xv, w, cv, sv)


def _dilated_attn_kernel(q_ref, kp_ref, kc_ref, vp_ref, vc_ref, o_ref, lse_ref):
    n = pl.program_id(2)
    q = q_ref[0, 0]
    kk = jnp.concatenate([kp_ref[0, 0], kc_ref[0, 0]], axis=0)
    vv = jnp.concatenate([vp_ref[0, 0], vc_ref[0, 0]], axis=0)
    qi = lax.broadcasted_iota(jnp.int32, (DIL_STEPS, 2 * DIL_STEPS), 0)
    kj = lax.broadcasted_iota(jnp.int32, (DIL_STEPS, 2 * DIL_STEPS), 1)
    valid = (kj >= qi) & (kj <= qi + DIL_STEPS) & ((n > 0) | (kj >= DIL_STEPS))
    out = jnp.zeros((DIL_STEPS, GROUP_W), F32)
    lse = jnp.zeros((DIL_STEPS, GROUP_W), F32)
    half = GROUP_W // 2
    hw = HEAD_DIM // 2
    for h in range(DIL_HEADS):
        qmask = _head_mask(GROUP_W, h * hw, (h + 1) * hw) | _head_mask(GROUP_W, half + h * hw, half + (h + 1) * hw)
        qm = jnp.where(qmask, q, jnp.zeros_like(q))
        sc = lax.dot_general(qm, kk, (((1,), (1,)), ((), ())), preferred_element_type=F32)
        sc = jnp.where(valid, sc, -jnp.inf)
        m = jnp.max(sc, axis=-1, keepdims=True)
        p = jnp.exp(sc - m)
        den = jnp.sum(p, axis=-1, keepdims=True)
        o_all = jnp.dot(p.astype(BF16), vv, preferred_element_type=F32)
        vmask = _head_mask(GROUP_W, h * HEAD_DIM, (h + 1) * HEAD_DIM)
        out = jnp.where(vmask, o_all / den, out)
        lse = jnp.where(vmask, m + jnp.log(den), lse)
    o_ref[0] = out.astype(BF16)
    lse_ref[0] = lse


def _dilated_attn(q, k, v, dil):
    b, _, length, _ = q.shape
    nb = length // DIL_STEPS
    cur = pl.BlockSpec((1, 1, DIL_STEPS, GROUP_W), lambda bi, r, n: (bi, r, n, 0))
    prev = pl.BlockSpec((1, 1, DIL_STEPS, GROUP_W), lambda bi, r, n: (bi, r, jnp.maximum(n - 1, 0), 0))
    out = pl.BlockSpec((1, DIL_STEPS, GROUP_W), lambda bi, r, n: (bi, n, r))
    o, lse = pl.pallas_call(
        _dilated_attn_kernel,
        grid=(b, dil, nb),
        in_specs=[cur, prev, cur, prev, cur],
        out_specs=[out, out],
        out_shape=[jax.ShapeDtypeStruct((b, length, dil * GROUP_W), BF16),
                   jax.ShapeDtypeStruct((b, length, dil * GROUP_W), F32)],
        compiler_params=_cparams(("parallel", "parallel", "parallel")),
        name=f"dilated_attn_d{dil}",
    )(q, k, k, v, v)
    return o.reshape(b * length * dil, GROUP_W), lse.reshape(b * length * dil, GROUP_W)


def _mem_attn_kernel(q_ref, kv_ref, o_ref):
    q = q_ref[...]
    mk = kv_ref[:, :MEM_W]
    mv = kv_ref[:, MEM_W:]
    out = jnp.zeros(q.shape, F32)
    for h in range(N_MEM_HEADS):
        hmask = _head_mask(MEM_W, h * HEAD_DIM, (h + 1) * HEAD_DIM)
        qm = jnp.where(hmask, q, jnp.zeros_like(q))
        sc = lax.dot_general(qm, mk, (((1,), (1,)), ((), ())), preferred_element_type=F32)
        m = jnp.max(sc, axis=-1, keepdims=True)
        p = jnp.exp(sc - m)
        den = jnp.sum(p, axis=-1, keepdims=True)
        o_all = jnp.dot(p.astype(BF16), mv, preferred_element_type=F32)
        out = jnp.where(hmask, o_all / den, out)
    o_ref[...] = out.astype(BF16)


def _mem_attn(q, mkv, batch, tq=512):
    nt = q.shape[0]
    per_b = nt // batch // tq
    m_len = mkv.shape[0] // batch
    return pl.pallas_call(
        _mem_attn_kernel,
        grid=(batch, per_b),
        in_specs=[pl.BlockSpec((tq, MEM_W), lambda bi, i: (bi * per_b + i, 0)),
                  pl.BlockSpec((m_len, 2 * MEM_W), lambda bi, i: (bi, 0))],
        out_specs=pl.BlockSpec((tq, MEM_W), lambda bi, i: (bi * per_b + i, 0)),
        out_shape=jax.ShapeDtypeStruct((nt, MEM_W), BF16),
        compiler_params=_cparams(("parallel", "parallel")),
        name="mem_attn",
    )(q, mkv)


def _out_proj_a_kernel(o0, o1, o2, l0, l1, l2, memo, x_ref, w_ref, g_ref, b_ref, out_ref):
    la, lb, lc = l0[...], l1[...], l2[...]
    mx = jnp.maximum(jnp.maximum(la, lb), lc)
    ea, eb, ec = jnp.exp(la - mx), jnp.exp(lb - mx), jnp.exp(lc - mx)
    z = ea + eb + ec
    dil = (ea * o0[...].astype(F32) + eb * o1[...].astype(F32) + ec * o2[...].astype(F32)) / z
    mix = jnp.dot(dil.astype(BF16), w_ref[:GROUP_W, :], preferred_element_type=F32)
    mix += jnp.dot(memo[...], w_ref[GROUP_W:, :], preferred_element_type=F32)
    y = DEEPNORM_ALPHA * x_ref[...] + mix
    out_ref[...] = _layer_norm_rows(y, g_ref[...], b_ref[...])


def _out_proj_b_kernel(fox, memo, x_ref, w_ref, g_ref, b_ref, out_ref):
    mix = jnp.dot(fox[...], w_ref[:FOX_W, :], preferred_element_type=F32)
    mix += jnp.dot(memo[...], w_ref[FOX_W:, :], preferred_element_type=F32)
    y = DEEPNORM_ALPHA * x_ref[...] + mix
    out_ref[...] = _layer_norm_rows(y, g_ref[...], b_ref[...])


def _out_proj(kernel_fn, acts, x, w, g, b, name, tm=512):
    nt, d = x.shape
    row = lambda i: (i, 0)
    const = lambda i: (0, 0)
    return pl.pallas_call(
        kernel_fn,
        grid=(nt // tm,),
        in_specs=[pl.BlockSpec((tm, a.shape[1]), row) for a in acts]
        + [pl.BlockSpec((tm, d), row), pl.BlockSpec(w.shape, const),
           pl.BlockSpec((1, d), const), pl.BlockSpec((1, d), const)],
        out_specs=pl.BlockSpec((tm, d), row),
        out_shape=jax.ShapeDtypeStruct((nt, d), F32),
        compiler_params=_cparams(("parallel",)),
        name=name,
    )(*acts, x, w, g.reshape(1, d), b.reshape(1, d))


def _router_kernel(x_ref, w_ref, b_ref, idx_ref, gate_ref, rank_ref, cnt_ref, carry_ref):
    @pl.when(pl.program_id(0) == 0)
    def _():
        carry_ref[...] = jnp.zeros_like(carry_ref)

    tm = x_ref.shape[0]
    logits = jnp.dot(x_ref[...], w_ref[...], precision=lax.Precision.HIGHEST,
                     preferred_element_type=F32) + b_ref[...]
    lane = lax.broadcasted_iota(jnp.int32, (tm, N_EXPERTS), 1).astype(F32)
    work = logits
    vals, sels, idxs = [], [], []
    for _ in range(TOP_K):
        mk = jnp.max(work, axis=-1, keepdims=True)
        ik = jnp.min(jnp.where(work == mk, lane, float(N_EXPERTS)), axis=-1, keepdims=True)
        sel = lane == ik
        work = jnp.where(sel, -jnp.inf, work)
        vals.append(mk)
        sels.append(sel)
        idxs.append(ik)
    es = [jnp.exp(v - vals[0]) for v in vals]
    z = es[0] + es[1] + es[2] + es[3]
    hot = jnp.zeros((tm, N_EXPERTS), F32)
    for sel in sels:
        hot = jnp.where(sel, 1.0, hot)
    row = lax.broadcasted_iota(jnp.int32, (tm, tm), 0)
    col = lax.broadcasted_iota(jnp.int32, (tm, tm), 1)
    tri = jnp.where(row > col, 1.0, 0.0).astype(BF16)
    before = jnp.dot(tri, hot.astype(BF16), preferred_element_type=F32) + carry_ref[...]
    k_lane = lax.broadcasted_iota(jnp.int32, (tm, TOP_K), 1)
    idx_o = jnp.zeros((tm, TOP_K), F32)
    gate_o = jnp.zeros((tm, TOP_K), F32)
    rank_o = jnp.zeros((tm, TOP_K), F32)
    for k in range(TOP_K):
        rk = jnp.sum(jnp.where(sels[k], before, 0.0), axis=-1, keepdims=True)
        idx_o = jnp.where(k_lane == k, idxs[k], idx_o)
        gate_o = jnp.where(k_lane == k, es[k] / z, gate_o)
        rank_o = jnp.where(k_lane == k, rk, rank_o)
    idx_ref[...] = idx_o.astype(jnp.int32)
    gate_ref[...] = gate_o
    rank_ref[...] = rank_o.astype(jnp.int32)
    carry_ref[...] += jnp.sum(hot, axis=0, keepdims=True)
    cnt_ref[...] = carry_ref[...].astype(jnp.int32)


def _router(x, w_r, b_r, tm=512):
    nt, d = x.shape
    row = lambda i: (i, 0)
    const = lambda i: (0, 0)
    return pl.pallas_call(
        _router_kernel,
        grid=(nt // tm,),
        in_specs=[pl.BlockSpec((tm, d), row), pl.BlockSpec((d, N_EXPERTS), const),
                  pl.BlockSpec((1, N_EXPERTS), const)],
        out_specs=[pl.BlockSpec((tm, TOP_K), row)] * 3 + [pl.BlockSpec((1, N_EXPERTS), const)],
        out_shape=[jax.ShapeDtypeStruct((nt, TOP_K), jnp.int32),
                   jax.ShapeDtypeStruct((nt, TOP_K), F32),
                   jax.ShapeDtypeStruct((nt, TOP_K), jnp.int32),
                   jax.ShapeDtypeStruct((1, N_EXPERTS), jnp.int32)],
        scratch_shapes=[pltpu.VMEM((1, N_EXPERTS), F32)],
        compiler_params=_cparams(("arbitrary",)),
        name="router",
    )(x, w_r, b_r.reshape(1, N_EXPERTS))


def _row_copy(src, dst, sem):
    return pltpu.make_async_copy(src, dst, sem)


def _dispatch_kernel(dest_ref, x_ref, zeros_ref, xs_ref, sem):
    del zeros_ref
    tm = x_ref.shape[0]
    base = pl.program_id(0) * tm * TOP_K

    def issue(i, carry):
        for k in range(TOP_K):
            d = dest_ref[base + i * TOP_K + k]
            _row_copy(x_ref.at[pl.ds(i, 1), :], xs_ref.at[pl.ds(d, 1), :], sem).start()
        return carry

    lax.fori_loop(0, tm, issue, 0)

    def drain(i, carry):
        _row_copy(x_ref.at[pl.ds(0, 1), :], xs_ref.at[pl.ds(0, 1), :], sem).wait()
        return carry

    lax.fori_loop(0, tm * TOP_K, drain, 0)


def _dispatch(dest_flat, x, rows, tm=256):
    nt, d = x.shape
    zeros = jnp.zeros((rows, d), x.dtype)
    return pl.pallas_call(
        _dispatch_kernel,
        grid_spec=pltpu.PrefetchScalarGridSpec(
            num_scalar_prefetch=1,
            grid=(nt // tm,),
            in_specs=[pl.BlockSpec((tm, d), lambda i, dest: (i, 0)),
                      pl.BlockSpec(memory_space=pl.ANY)],
            out_specs=pl.BlockSpec(memory_space=pl.ANY),
            scratch_shapes=[pltpu.SemaphoreType.DMA(())],
        ),
        out_shape=jax.ShapeDtypeStruct((rows, d), x.dtype),
        input_output_aliases={2: 0},
        compiler_params=_cparams(("arbitrary",)),
        name="moe_dispatch",
    )(dest_flat, x, zeros)


def _expert_kernel(exp_ref, first_ref, active_ref, xs_ref, wgu_ref, bgu_ref, wdn_ref, bdn_ref,
                   y_ref, wgu_bf, wdn_bf):
    del exp_ref
    i = pl.program_id(0)

    @pl.when(first_ref[i] == 1)
    def _():
        wgu_bf[...] = wgu_ref[...].astype(BF16)
        wdn_bf[...] = wdn_ref[...].astype(BF16)

    @pl.when(active_ref[i] == 1)
    def _():
        hb = jnp.dot(xs_ref[...].astype(BF16), wgu_bf[...], preferred_element_type=F32) + bgu_ref[...]
        gate = jnp.minimum(hb[:, :D_FF], SWIGLU_LIMIT)
        up = jnp.clip(hb[:, D_FF:], -SWIGLU_LIMIT, SWIGLU_LIMIT)
        act = (up + 1.0) * gate * jax.nn.sigmoid(SWIGLU_ALPHA * gate)
        y_ref[...] = jnp.dot(act.astype(BF16), wdn_bf[...], preferred_element_type=F32) + bdn_ref[...]

    @pl.when(active_ref[i] == 0)
    def _():
        y_ref[...] = jnp.zeros_like(y_ref)


def _experts(block_exp, block_first, block_active, xs, w_gu, b_gu, w_dn, b_dn, layer):
    rows, d = xs.shape
    n_blocks = rows // MOE_TM
    return pl.pallas_call(
        _expert_kernel,
        grid_spec=pltpu.PrefetchScalarGridSpec(
            num_scalar_prefetch=3,
            grid=(n_blocks,),
            in_specs=[
                pl.BlockSpec((MOE_TM, d), lambda i, e, f, a: (i, 0)),
                pl.BlockSpec((None, None, d, 2 * D_FF), lambda i, e, f, a: (layer, e[i], 0, 0)),
                pl.BlockSpec((None, None, 1, 2 * D_FF), lambda i, e, f, a: (layer, e[i], 0, 0)),
                pl.BlockSpec((None, None, D_FF, d), lambda i, e, f, a: (layer, e[i], 0, 0)),
                pl.BlockSpec((None, None, 1, d), lambda i, e, f, a: (layer, e[i], 0, 0)),
            ],
            out_specs=pl.BlockSpec((MOE_TM, d), lambda i, e, f, a: (i, 0)),
            scratch_shapes=[pltpu.VMEM((d, 2 * D_FF), BF16), pltpu.VMEM((D_FF, d), BF16)],
        ),
        out_shape=jax.ShapeDtypeStruct((rows, d), F32),
        compiler_params=_cparams(("arbitrary",), VMEM_LIMIT),
        name="moe_experts",
    )(block_exp, block_first, block_active, xs, w_gu,
      b_gu.reshape(DEPTH, N_EXPERTS, 1, 2 * D_FF), w_dn, b_dn.reshape(DEPTH, N_EXPERTS, 1, d))


def _combine_kernel(dest_ref, x_ref, gate_ref, g_ref, b_ref, y_ref, out_ref, buf, sem):
    tm = x_ref.shape[0]
    base = pl.program_id(0) * tm * TOP_K

    def issue(i, carry):
        for k in range(TOP_K):
            d = dest_ref[base + i * TOP_K + k]
            _row_copy(y_ref.at[pl.ds(d, 1), :], buf.at[k, pl.ds(i, 1), :], sem).start()
        return carry

    lax.fori_loop(0, tm, issue, 0)

    def drain(i, carry):
        _row_copy(y_ref.at[pl.ds(0, 1), :], buf.at[0, pl.ds(0, 1), :], sem).wait()
        return carry

    lax.fori_loop(0, tm * TOP_K, drain, 0)

    gates = gate_ref[...]
    ffn = gates[:, 0:1] * buf[0]
    for k in range(1, TOP_K):
        ffn += gates[:, k:k + 1] * buf[k]
    y = DEEPNORM_ALPHA * x_ref[...] + ffn
    out_ref[...] = _layer_norm_rows(y, g_ref[...], b_ref[...])


def _combine(dest_flat, x, gates, g, b, y_rows, tm=256):
    nt, d = x.shape
    return pl.pallas_call(
        _combine_kernel,
        grid_spec=pltpu.PrefetchScalarGridSpec(
            num_scalar_prefetch=1,
            grid=(nt // tm,),
            in_specs=[pl.BlockSpec((tm, d), lambda i, dest: (i, 0)),
                      pl.BlockSpec((tm, TOP_K), lambda i, dest: (i, 0)),
                      pl.BlockSpec((1, d), lambda i, dest: (0, 0)),
                      pl.BlockSpec((1, d), lambda i, dest: (0, 0)),
                      pl.BlockSpec(memory_space=pl.ANY)],
            out_specs=pl.BlockSpec((tm, d), lambda i, dest: (i, 0)),
            scratch_shapes=[pltpu.VMEM((TOP_K, tm, d), F32), pltpu.SemaphoreType.DMA(())],
        ),
        out_shape=jax.ShapeDtypeStruct((nt, d), F32),
        compiler_params=_cparams(("arbitrary",)),
        name="moe_combine",
    )(dest_flat, x, gates, g.reshape(1, d), b.reshape(1, d), y_rows)


def _moe_layer(x, layer, w_router, b_router, w_gate_up, b_gate_up, w_down, b_down, ln_g, ln_b):
    nt, _ = x.shape
    idx, gates, rank, counts = _router(x, w_router[layer], b_router[layer])
    counts = counts[0]
    padded = (counts + MOE_TM - 1) // MOE_TM * MOE_TM
    pend = jnp.cumsum(padded)
    pstart = pend - padded
    dest = (pstart[idx] + rank).reshape(nt * TOP_K).astype(jnp.int32)
    n_blocks = nt * TOP_K // MOE_TM + N_EXPERTS
    blk_start = jnp.arange(n_blocks, dtype=jnp.int32) * MOE_TM
    active = blk_start < pend[-1]
    exp_raw = jnp.minimum(jnp.searchsorted(pend, blk_start, side='right'), N_EXPERTS - 1).astype(jnp.int32)
    last_exp = jnp.max(jnp.where(active, exp_raw, 0))
    block_exp = jnp.where(active, exp_raw, last_exp).astype(jnp.int32)
    prev_exp = jnp.concatenate([jnp.full((1,), -1, jnp.int32), block_exp[:-1]])
    block_first = (active & (block_exp != prev_exp)).astype(jnp.int32)
    xs = _dispatch(dest, x, n_blocks * MOE_TM)
    y_rows = _experts(block_exp, block_first, active.astype(jnp.int32), xs,
                      w_gate_up, b_gate_up, w_down, b_down, layer)
    return _combine(dest, x, gates, ln_g, ln_b, y_rows)


def _split3(v):
    hi = v.astype(BF16)
    r1 = v - hi.astype(F32)
    mid = r1.astype(BF16)
    lo = (r1 - mid.astype(F32)).astype(BF16)
    return hi, mid, lo


def _shared_proj_kernel(x_ref, w_ref, wfh_ref, wfl_ref, bf_ref, eq_ref, ek_ref, oq_ref, ok_ref,
                        kq_ref, v_ref, qq_ref, qm_ref, carry_ref):
    @pl.when(pl.program_id(1) == 0)
    def _():
        carry_ref[...] = jnp.zeros_like(carry_ref)

    tm = x_ref.shape[0]
    x = x_ref[...]
    xh = x.astype(BF16)
    xl = (x - xh.astype(F32)).astype(BF16)
    h = jnp.dot(xh, w_ref[...], preferred_element_type=F32)
    f = (jnp.dot(xh, wfh_ref[...], preferred_element_type=F32)
         + jnp.dot(xl, wfh_ref[...], preferred_element_type=F32)
         + jnp.dot(xh, wfl_ref[...], preferred_element_type=F32)) + bf_ref[...]
    ls = jnp.minimum(f, 0.0) - jnp.log1p(jnp.exp(-jnp.abs(f)))
    ls = jnp.where(_head_mask(LANES, 0, N_FOX_HEADS), ls, 0.0)
    row = lax.broadcasted_iota(jnp.int32, (tm, tm), 0)
    col = lax.broadcasted_iota(jnp.int32, (tm, tm), 1)
    tri = jnp.where(row >= col, 1.0, 0.0).astype(BF16)
    a, b, c = _split3(ls)
    cum = (jnp.dot(tri, a, preferred_element_type=F32) + jnp.dot(tri, b, preferred_element_type=F32)
           + jnp.dot(tri, c, preferred_element_type=F32)) + carry_ref[...]
    carry_ref[...] = cum[tm - 1:tm, :]
    ch, cm, cl = _split3(cum)
    cat = (ch.astype(F32) + pltpu.roll(cm.astype(F32), 16, 1) + pltpu.roll(cl.astype(F32), 32, 1)).astype(BF16)
    kq = h[:, :FOX_QK_W] + jnp.dot(cat, ek_ref[...], preferred_element_type=F32) + ok_ref[...]
    qq = (h[:, FOX_QK_W + FOX_W:2 * FOX_QK_W + FOX_W]
          + jnp.dot(cat, eq_ref[...], preferred_element_type=F32) + oq_ref[...])
    kq_ref[...] = kq.astype(BF16)
    v_ref[...] = h[:, FOX_QK_W:FOX_QK_W + FOX_W].astype(BF16)
    qq_ref[...] = qq.astype(BF16)
    qm_ref[...] = h[:, 2 * FOX_QK_W + FOX_W:].astype(BF16)


def _widen_heads(w):
    d = w.shape[0]
    w3 = w.reshape(d, N_FOX_HEADS, HEAD_DIM)
    return jnp.concatenate([w3, jnp.zeros_like(w3)], axis=-1).reshape(d, FOX_QK_W)


def _spread_matrices():
    src = jnp.arange(LANES)[:, None]
    dst = jnp.arange(FOX_QK_W)[None, :]
    head = dst // FOX_HEAD_W
    off = dst % FOX_HEAD_W
    part = src // 16
    is_src = (src % 16 == head) & (src % 16 < N_FOX_HEADS) & (part < 3)
    eq = jnp.where(is_src & (off == HEAD_DIM + part), 1.0, 0.0).astype(BF16)
    ek = jnp.where(is_src & (off == HEAD_DIM + 3 + part), -1.0, 0.0).astype(BF16)
    off1 = jnp.arange(FOX_QK_W) % FOX_HEAD_W
    ones_q = ((off1 >= HEAD_DIM + 3) & (off1 < HEAD_DIM + 6)).astype(F32).reshape(1, FOX_QK_W)
    ones_k = ((off1 >= HEAD_DIM) & (off1 < HEAD_DIM + 3)).astype(F32).reshape(1, FOX_QK_W)
    return eq, ek, ones_q, ones_k


def _shared_proj(x, batch, w_shared_kvf, b_forget, w_in_b, tm=256):
    nt, d = x.shape
    per_b = nt // batch // tm
    w_k = _widen_heads(w_shared_kvf[:, :FOX_W])
    w_v = w_shared_kvf[:, FOX_W:2 * FOX_W]
    w_q = _widen_heads(w_in_b[:, :FOX_W] * ATTN_SCALE)
    w_qm = w_in_b[:, FOX_W:] * ATTN_SCALE
    w_big = jnp.concatenate([w_k, w_v, w_q, w_qm], axis=1).astype(BF16)
    w_f = jnp.pad(w_shared_kvf[:, 2 * FOX_W:], ((0, 0), (0, LANES - N_FOX_HEADS)))
    w_fh = w_f.astype(BF16)
    w_fl = (w_f - w_fh.astype(F32)).astype(BF16)
    b_f = jnp.pad(b_forget, (0, LANES - N_FOX_HEADS)).reshape(1, LANES)
    eq, ek, ones_q, ones_k = _spread_matrices()
    nbig = w_big.shape[1]
    row = lambda bi, i: (bi * per_b + i, 0)
    const = lambda bi, i: (0, 0)
    return pl.pallas_call(
        _shared_proj_kernel,
        grid=(batch, per_b),
        in_specs=[pl.BlockSpec((tm, d), row),
                  pl.BlockSpec((d, nbig), const),
                  pl.BlockSpec((d, LANES), const), pl.BlockSpec((d, LANES), const),
                  pl.BlockSpec((1, LANES), const),
                  pl.BlockSpec((LANES, FOX_QK_W), const), pl.BlockSpec((LANES, FOX_QK_W), const),
                  pl.BlockSpec((1, FOX_QK_W), const), pl.BlockSpec((1, FOX_QK_W), const)],
        out_specs=[pl.BlockSpec((tm, FOX_QK_W), row), pl.BlockSpec((tm, FOX_W), row),
                   pl.BlockSpec((tm, FOX_QK_W), row), pl.BlockSpec((tm, MEM_W), row)],
        out_shape=[jax.ShapeDtypeStruct((nt, FOX_QK_W), BF16), jax.ShapeDtypeStruct((nt, FOX_W), BF16),
                   jax.ShapeDtypeStruct((nt, FOX_QK_W), BF16), jax.ShapeDtypeStruct((nt, MEM_W), BF16)],
        scratch_shapes=[pltpu.VMEM((1, LANES), F32)],
        compiler_params=_cparams(("arbitrary", "arbitrary"), VMEM_LIMIT),
        name="shared_proj",
    )(x, w_big, w_fh, w_fl, b_f, eq, ek, ones_q, ones_k)


def _fox_kernel(q_ref, k_ref, v_ref, o_ref):
    tq = q_ref.shape[0]
    i = pl.program_id(2)
    rowg = i * tq + lax.broadcasted_iota(jnp.int32, (tq, tq), 0)
    coll = lax.broadcasted_iota(jnp.int32, (tq, tq), 1)
    out = jnp.zeros((tq, LANES), F32)
    for hh in range(2):
        qh = q_ref[:, hh * FOX_HEAD_W:(hh + 1) * FOX_HEAD_W]

        def step(j, carry):
            m, l, acc = carry
            off = pl.multiple_of(j * tq, tq)
            kj = k_ref[pl.ds(off, tq), hh * FOX_HEAD_W:(hh + 1) * FOX_HEAD_W]
            sc = lax.dot_general(qh, kj, (((1,), (1,)), ((), ())), preferred_element_type=F32)
            sc = jnp.where(coll + j * tq <= rowg, sc, -jnp.inf)
            m_new = jnp.maximum(m, jnp.max(sc, axis=-1, keepdims=True))
            alpha = jnp.exp(m - m_new)
            p = jnp.exp(sc - m_new)
            l = alpha * l + jnp.sum(p, axis=-1, keepdims=True)
            acc = alpha * acc + jnp.dot(p.astype(BF16), v_ref[pl.ds(off, tq), :], preferred_element_type=F32)
            return m_new, l, acc

        init = (jnp.full((tq, 1), -jnp.inf, F32), jnp.zeros((tq, 1), F32), jnp.zeros((tq, LANES), F32))
        _, l, acc = lax.fori_loop(0, i + 1, step, init)
        out = jnp.where(_head_mask(LANES, hh * HEAD_DIM, (hh + 1) * HEAD_DIM), acc / l, out)
    o_ref[...] = out.astype(BF16)


def _fox_attn(qq, kq, v, batch, tq=256):
    nt = qq.shape[0]
    s = nt // batch
    per_b = s // tq
    pairs = N_FOX_HEADS // 2
    return pl.pallas_call(
        _fox_kernel,
        grid=(batch, pairs, per_b),
        in_specs=[pl.BlockSpec((tq, 2 * FOX_HEAD_W), lambda bi, p, i: (bi * per_b + i, p)),
                  pl.BlockSpec((s, 2 * FOX_HEAD_W), lambda bi, p, i: (bi, p)),
                  pl.BlockSpec((s, LANES), lambda bi, p, i: (bi, p))],
        out_specs=pl.BlockSpec((tq, LANES), lambda bi, p, i: (bi * per_b + i, p)),
        out_shape=jax.ShapeDtypeStruct((nt, FOX_W), BF16),
        compiler_params=_cparams(("parallel", "parallel", "parallel")),
        name="fox_attn",
    )(qq, kq, v)


def _rope_tables(seq):
    inv = 1.0 / (ROPE_THETA ** (jnp.arange(0, HEAD_DIM, 2, dtype=F32) / HEAD_DIM))
    ang = jnp.arange(seq, dtype=F32)[:, None] * inv[None, :]
    return jnp.tile(jnp.cos(ang), (1, DIL_HEADS)), jnp.tile(jnp.sin(ang), (1, DIL_HEADS))


def _rotary_layout(w):
    d = w.shape[0]
    w4 = w.reshape(d, DIL_HEADS, 2, HEAD_DIM // 2)
    return w4.transpose(0, 2, 1, 3).reshape(d, GROUP_W)


def _group_weights(w_in, g, with_mem):
    base = g * 3 * GROUP_W
    cols = [_rotary_layout(w_in[:, base:base + GROUP_W] * ATTN_SCALE),
            _rotary_layout(w_in[:, base + GROUP_W:base + 2 * GROUP_W]),
            w_in[:, base + 2 * GROUP_W:base + 3 * GROUP_W]]
    if with_mem:
        cols.append(w_in[:, DIL_QKV_W:] * ATTN_SCALE)
    return jnp.concatenate(cols, axis=1).astype(BF16)


def kernel(x, mem, w_in_a, w_out_a, w_in_b, w_out_b, w_shared_kvf, b_forget, w_mem_kv, ln_mix_g, ln_mix_b,
           ln_ffn_g, ln_ffn_b, w_router, b_router, w_gate_up, b_gate_up, w_down, b_down):
    b, s, d = x.shape
    nt = b * s
    mem2 = mem.reshape(b * mem.shape[1], d)
    cos_t, sin_t = _rope_tables(s)
    moe = functools.partial(_moe_layer, w_router=w_router, b_router=b_router, w_gate_up=w_gate_up,
                            b_gate_up=b_gate_up, w_down=w_down, b_down=b_down)

    outs, lses, q_mem = [], [], None
    for g, (_, dil) in enumerate(DILATED_PATTERNS):
        res = _group_proj(x, _group_weights(w_in_a[0], g, g == 0), cos_t, sin_t, dil)
        if g == 0:
            q_mem = res[3].reshape(nt, MEM_W)
        o, lse = _dilated_attn(res[0], res[1], res[2], dil)
        outs.append(o)
        lses.append(lse)
    mkv0 = _matmul(mem2, w_mem_kv[0].astype(BF16), BF16, 512)
    memo = _mem_attn(q_mem, mkv0, b)
    x2 = x.reshape(nt, d)
    x2 = _out_proj(_out_proj_a_kernel, outs + lses + [memo], x2, w_out_a[0].astype(BF16),
                   ln_mix_g[0], ln_mix_b[0], "out_proj_a")
    x2 = moe(x2, 0, ln_g=ln_ffn_g[0], ln_b=ln_ffn_b[0])

    kq, v_sh, qq, q_mem = _shared_proj(x2, b, w_shared_kvf, b_forget, w_in_b[0])

    fox = _fox_attn(qq, kq, v_sh, b)
    mkv1 = _matmul(mem2, w_mem_kv[1].astype(BF16), BF16, 512)
    memo = _mem_attn(q_mem, mkv1, b)
    x2 = _out_proj(_out_proj_b_kernel, [fox, memo], x2, w_out_b[0].astype(BF16),
                   ln_mix_g[1], ln_mix_b[1], "out_proj_b")
    x2 = moe(x2, 1, ln_g=ln_ffn_g[1], ln_b=ln_ffn_b[1])
    return x2.reshape(b, s, d)
```

```python
import functools

import jax
import jax.numpy as jnp
from jax import lax
from jax.experimental import pallas as pl
from jax.experimental.pallas import tpu as pltpu

F32 = jnp.float32
BF16 = jnp.bfloat16

D_MODEL = 1024
DEPTH = 2
HEAD_DIM = 64
N_MEM_HEADS = 4
DILATED_PATTERNS = ((128, 1), (512, 4), (2048, 16))
N_DIL_GROUPS = 3
DIL_HEADS = 4
N_FOX_HEADS = 12
GROUP_W = DIL_HEADS * HEAD_DIM
MEM_W = N_MEM_HEADS * HEAD_DIM
FOX_W = N_FOX_HEADS * HEAD_DIM
DIL_QKV_W = N_DIL_GROUPS * 3 * GROUP_W
N_EXPERTS = 32
TOP_K = 4
D_FF = D_MODEL
SWIGLU_LIMIT = 7.0
SWIGLU_ALPHA = 1.702
ROPE_THETA = 10000.0
LN_EPS = 1e-5
ATTN_SCALE = HEAD_DIM ** -0.5
DEEPNORM_ALPHA = (2.0 * DEPTH) ** 0.25
DIL_STEPS = 128

LANES = 128
FOX_HEAD_W = LANES
FOX_QK_W = N_FOX_HEADS * FOX_HEAD_W
MOE_TM = 256
VMEM_LIMIT = 56 * 1024 * 1024


def _cparams(sem, vmem=None):
    return pltpu.CompilerParams(dimension_semantics=sem, vmem_limit_bytes=vmem)


def _head_mask(width, lo, hi):
    lane = lax.broadcasted_iota(jnp.int32, (1, width), 1)
    return (lane >= lo) & (lane < hi)


def _layer_norm_rows(y, g, b):
    mu = jnp.mean(y, axis=-1, keepdims=True)
    yc = y - mu
    var = jnp.mean(yc * yc, axis=-1, keepdims=True)
    return yc * lax.rsqrt(var + LN_EPS) * g + b


def _matmul_kernel(x_ref, w_ref, o_ref):
    o_ref[...] = jnp.dot(x_ref[...].astype(BF16), w_ref[...],
                         preferred_element_type=F32).astype(o_ref.dtype)


def _matmul(x, w, out_dtype, tm):
    m, k = x.shape
    n = w.shape[1]
    return pl.pallas_call(
        _matmul_kernel,
        grid=(m // tm,),
        in_specs=[pl.BlockSpec((tm, k), lambda i: (i, 0)),
                  pl.BlockSpec((k, n), lambda i: (0, 0))],
        out_specs=pl.BlockSpec((tm, n), lambda i: (i, 0)),
        out_shape=jax.ShapeDtypeStruct((m, n), out_dtype),
        compiler_params=_cparams(("parallel",)),
        name="matmul",
    )(x, w)


def _group_proj_kernel(x_ref, w_ref, cos_ref, sin_ref, *o_refs):
    h = jnp.dot(x_ref[0].astype(BF16), w_ref[...], preferred_element_type=F32)
    cos = cos_ref[...]
    sin = sin_ref[...]
    half = GROUP_W // 2
    for j in range(2):
        t1 = h[:, j * GROUP_W:j * GROUP_W + half]
        t2 = h[:, j * GROUP_W + half:(j + 1) * GROUP_W]
        o_refs[j][0, 0, :, :half] = (t1 * cos - t2 * sin).astype(BF16)
        o_refs[j][0, 0, :, half:] = (t2 * cos + t1 * sin).astype(BF16)
    for j in range(2, len(o_refs)):
        o_refs[j][0, 0] = h[:, j * GROUP_W:(j + 1) * GROUP_W].astype(BF16)


def _group_proj(x, w, cos_t, sin_t, dil):
    b, s, d = x.shape
    length = s // dil
    lt = min(length, 512)
    n_out = w.shape[1] // GROUP_W
    xv = x.reshape(b, length, dil * d)
    cv = cos_t.reshape(length, dil * LANES)
    sv = sin_t.reshape(length, dil * LANES)
    o_spec = pl.BlockSpec((1, 1, lt, GROUP_W), lambda bi, r, l: (bi, r, l, 0))
    return pl.pallas_call(
        _group_proj_kernel,
        grid=(b, dil, length // lt),
        in_specs=[pl.BlockSpec((1, lt, d), lambda bi, r, l: (bi, l, r)),
                  pl.BlockSpec(w.shape, lambda bi, r, l: (0, 0)),
                  pl.BlockSpec((lt, LANES), lambda bi, r, l: (l, r)),
                  pl.BlockSpec((lt, LANES), lambda bi, r, l: (l, r))],
        out_specs=[o_spec] * n_out,
        out_shape=[jax.ShapeDtypeStruct((b, dil, length, GROUP_W), BF16)] * n_out,
        compiler_params=_cparams(("parallel", "parallel", "parallel")),
        name=f"group_proj_d{dil}",
    )(xv, w, cv, sv)


def _dilated_attn_kernel(q_ref, kp_ref, kc_ref, vp_ref, vc_ref, o_ref, lse_ref):
    n = pl.program_id(2)
    n_res = q_ref.shape[1]
    n_sub = q_ref.shape[2] // DIL_STEPS
    qi = lax.broadcasted_iota(jnp.int32, (DIL_STEPS, 2 * DIL_STEPS), 0)
    kj = lax.broadcasted_iota(jnp.int32, (DIL_STEPS, 2 * DIL_STEPS), 1)
    band = (kj >= qi) & (kj <= qi + DIL_STEPS)
    half = GROUP_W // 2
    hw = HEAD_DIM // 2
    for r in range(n_res):
        for j in range(n_sub):
            lo, hi = j * DIL_STEPS, (j + 1) * DIL_STEPS
            q = q_ref[0, r, lo:hi, :]
            if j == 0:
                kk = jnp.concatenate([kp_ref[0, r], kc_ref[0, r, lo:hi, :]], axis=0)
                vv = jnp.concatenate([vp_ref[0, r], vc_ref[0, r, lo:hi, :]], axis=0)
                valid = band & ((n > 0) | (kj >= DIL_STEPS))
            else:
                kk = kc_ref[0, r, lo - DIL_STEPS:hi, :]
                vv = vc_ref[0, r, lo - DIL_STEPS:hi, :]
                valid = band
            out = jnp.zeros((DIL_STEPS, GROUP_W), F32)
            lse = jnp.zeros((DIL_STEPS, GROUP_W), F32)
            for h in range(DIL_HEADS):
                qmask = (_head_mask(GROUP_W, h * hw, (h + 1) * hw)
                         | _head_mask(GROUP_W, half + h * hw, half + (h + 1) * hw))
                qm = jnp.where(qmask, q, jnp.zeros_like(q))
                sc = lax.dot_general(qm, kk, (((1,), (1,)), ((), ())), preferred_element_type=F32)
                sc = jnp.where(valid, sc, -jnp.inf)
                m = jnp.max(sc, axis=-1, keepdims=True)
                p = jnp.exp(sc - m)
                den = jnp.sum(p, axis=-1, keepdims=True)
                o_all = jnp.dot(p.astype(BF16), vv, preferred_element_type=F32)
                vmask = _head_mask(GROUP_W, h * HEAD_DIM, (h + 1) * HEAD_DIM)
                out = jnp.where(vmask, o_all / den, out)
                lse = jnp.where(vmask, m + jnp.log(den), lse)
            o_ref[0, lo:hi, r * GROUP_W:(r + 1) * GROUP_W] = out.astype(BF16)
            lse_ref[0, lo:hi, r * GROUP_W:(r + 1) * GROUP_W] = lse


DIL_UNITS = 4


def _dilated_attn(q, k, v, dil):
    b, _, length, _ = q.shape
    n_sub = min(length // DIL_STEPS, DIL_UNITS)
    n_res = DIL_UNITS // n_sub
    rows = n_sub * DIL_STEPS
    cur = pl.BlockSpec((1, n_res, rows, GROUP_W), lambda bi, r, n: (bi, r, n, 0))
    prev = pl.BlockSpec((1, n_res, DIL_STEPS, GROUP_W),
                        lambda bi, r, n: (bi, r, jnp.maximum(n * n_sub - 1, 0), 0))
    out = pl.BlockSpec((1, rows, n_res * GROUP_W), lambda bi, r, n: (bi, n, r))
    o, lse = pl.pallas_call(
        _dilated_attn_kernel,
        grid=(b, dil // n_res, length // rows),
        in_specs=[cur, prev, cur, prev, cur],
        out_specs=[out, out],
        out_shape=[jax.ShapeDtypeStruct((b, length, dil * GROUP_W), BF16),
                   jax.ShapeDtypeStruct((b, length, dil * GROUP_W), F32)],
        compiler_params=_cparams(("parallel", "parallel", "parallel")),
        name=f"dilated_attn_d{dil}",
    )(q, k, k, v, v)
    return o.reshape(b * length * dil, GROUP_W), lse.reshape(b * length * dil, GROUP_W)


def _mem_attn_kernel(q_ref, kv_ref, o_ref):
    q = q_ref[...]
    mk = kv_ref[:, :MEM_W]
    mv = kv_ref[:, MEM_W:]
    out = jnp.zeros(q.shape, F32)
    for h in range(N_MEM_HEADS):
        hmask = _head_mask(MEM_W, h * HEAD_DIM, (h + 1) * HEAD_DIM)
        qm = jnp.where(hmask, q, jnp.zeros_like(q))
        sc = lax.dot_general(qm, mk, (((1,), (1,)), ((), ())), preferred_element_type=F32)
        m = jnp.max(sc, axis=-1, keepdims=True)
        p = jnp.exp(sc - m)
        den = jnp.sum(p, axis=-1, keepdims=True)
        o_all = jnp.dot(p.astype(BF16), mv, preferred_element_type=F32)
        out = jnp.where(hmask, o_all / den, out)
    o_ref[...] = out.astype(BF16)


def _mem_attn(q, mkv, batch, tq=512):
    nt = q.shape[0]
    per_b = nt // batch // tq
    m_len = mkv.shape[0] // batch
    return pl.pallas_call(
        _mem_attn_kernel,
        grid=(batch, per_b),
        in_specs=[pl.BlockSpec((tq, MEM_W), lambda bi, i: (bi * per_b + i, 0)),
                  pl.BlockSpec((m_len, 2 * MEM_W), lambda bi, i: (bi, 0))],
        out_specs=pl.BlockSpec((tq, MEM_W), lambda bi, i: (bi * per_b + i, 0)),
        out_shape=jax.ShapeDtypeStruct((nt, MEM_W), BF16),
        compiler_params=_cparams(("parallel", "parallel")),
        name="mem_attn",
    )(q, mkv)


def _out_proj_a_kernel(o0, o1, o2, l0, l1, l2, memo, x_ref, w_ref, g_ref, b_ref, out_ref):
    la, lb, lc = l0[...], l1[...], l2[...]
    mx = jnp.maximum(jnp.maximum(la, lb), lc)
    ea, eb, ec = jnp.exp(la - mx), jnp.exp(lb - mx), jnp.exp(lc - mx)
    z = ea + eb + ec
    dil = (ea * o0[...].astype(F32) + eb * o1[...].astype(F32) + ec * o2[...].astype(F32)) / z
    mix = jnp.dot(dil.astype(BF16), w_ref[:GROUP_W, :], preferred_element_type=F32)
    mix += jnp.dot(memo[...], w_ref[GROUP_W:, :], preferred_element_type=F32)
    y = DEEPNORM_ALPHA * x_ref[...] + mix
    out_ref[...] = _layer_norm_rows(y, g_ref[...], b_ref[...])


def _out_proj_b_kernel(fox, memo, x_ref, w_ref, g_ref, b_ref, out_ref):
    mix = jnp.dot(fox[...], w_ref[:FOX_W, :], preferred_element_type=F32)
    mix += jnp.dot(memo[...], w_ref[FOX_W:, :], preferred_element_type=F32)
    y = DEEPNORM_ALPHA * x_ref[...] + mix
    out_ref[...] = _layer_norm_rows(y, g_ref[...], b_ref[...])


def _out_proj(kernel_fn, acts, x, w, g, b, name, tm=512):
    nt, d = x.shape
    row = lambda i: (i, 0)
    const = lambda i: (0, 0)
    return pl.pallas_call(
        kernel_fn,
        grid=(nt // tm,),
        in_specs=[pl.BlockSpec((tm, a.shape[1]), row) for a in acts]
        + [pl.BlockSpec((tm, d), row), pl.BlockSpec(w.shape, const),
           pl.BlockSpec((1, d), const), pl.BlockSpec((1, d), const)],
        out_specs=pl.BlockSpec((tm, d), row),
        out_shape=jax.ShapeDtypeStruct((nt, d), F32),
        compiler_params=_cparams(("parallel",)),
        name=name,
    )(*acts, x, w, g.reshape(1, d), b.reshape(1, d))


def _router_kernel(x_ref, wt_ref, b_ref, idx_ref, rank_ref, gate_ref, cnt_ref, carry_ref):
    @pl.when(pl.program_id(0) == 0)
    def _():
        carry_ref[...] = jnp.zeros_like(carry_ref)

    tm = x_ref.shape[0]
    logits = lax.dot_general(wt_ref[...], x_ref[...], (((1,), (1,)), ((), ())),
                             precision=lax.Precision.HIGHEST, preferred_element_type=F32) + b_ref[...]
    sub = lax.broadcasted_iota(jnp.int32, (N_EXPERTS, tm), 0).astype(F32)
    work = logits
    vals, sels, idxs = [], [], []
    for _ in range(TOP_K):
        mk = jnp.max(work, axis=0, keepdims=True)
        ik = jnp.min(jnp.where(work == mk, sub, float(N_EXPERTS)), axis=0, keepdims=True)
        sel = sub == ik
        work = jnp.where(sel, -jnp.inf, work)
        vals.append(mk)
        sels.append(sel)
        idxs.append(ik)
    es = [jnp.exp(v - vals[0]) for v in vals]
    z = es[0] + es[1] + es[2] + es[3]
    hot = jnp.zeros((N_EXPERTS, tm), F32)
    for sel in sels:
        hot = jnp.where(sel, 1.0, hot)
    row = lax.broadcasted_iota(jnp.int32, (tm, tm), 0)
    col = lax.broadcasted_iota(jnp.int32, (tm, tm), 1)
    tri = jnp.where(row < col, 1.0, 0.0).astype(BF16)
    before = jnp.dot(hot.astype(BF16), tri, preferred_element_type=F32) + carry_ref[...]
    k_sub = lax.broadcasted_iota(jnp.int32, (TOP_K, tm), 0)
    g_sub = lax.broadcasted_iota(jnp.int32, (LANES, tm), 0)
    idx_o = jnp.zeros((TOP_K, tm), F32)
    rank_o = jnp.zeros((TOP_K, tm), F32)
    gate_t = jnp.zeros((LANES, tm), F32)
    for k in range(TOP_K):
        rk = jnp.sum(jnp.where(sels[k], before, 0.0), axis=0, keepdims=True)
        idx_o = jnp.where(k_sub == k, idxs[k], idx_o)
        rank_o = jnp.where(k_sub == k, rk, rank_o)
        gate_t = jnp.where(g_sub == k, es[k] / z, gate_t)
    idx_ref[...] = idx_o.astype(jnp.int32)
    rank_ref[...] = rank_o.astype(jnp.int32)
    gate_ref[...] = gate_t.T
    carry_ref[...] += jnp.sum(hot, axis=1, keepdims=True)
    cnt_ref[...] = carry_ref[...].astype(jnp.int32)


def _router(x, w_r, b_r, tm=512):
    nt, d = x.shape
    const = lambda i: (0, 0)
    return pl.pallas_call(
        _router_kernel,
        grid=(nt // tm,),
        in_specs=[pl.BlockSpec((tm, d), lambda i: (i, 0)), pl.BlockSpec((N_EXPERTS, d), const),
                  pl.BlockSpec((N_EXPERTS, 1), const)],
        out_specs=[pl.BlockSpec((TOP_K, tm), lambda i: (0, i)), pl.BlockSpec((TOP_K, tm), lambda i: (0, i)),
                   pl.BlockSpec((tm, LANES), lambda i: (i, 0)), pl.BlockSpec((N_EXPERTS, 1), const)],
        out_shape=[jax.ShapeDtypeStruct((TOP_K, nt), jnp.int32),
                   jax.ShapeDtypeStruct((TOP_K, nt), jnp.int32),
                   jax.ShapeDtypeStruct((nt, LANES), F32),
                   jax.ShapeDtypeStruct((N_EXPERTS, 1), jnp.int32)],
        scratch_shapes=[pltpu.VMEM((N_EXPERTS, 1), F32)],
        compiler_params=_cparams(("arbitrary",)),
        name="router",
    )(x, w_r.T, b_r.reshape(N_EXPERTS, 1))


def _row_copy(src, dst, sem):
    return pltpu.make_async_copy(src, dst, sem)


def _dispatch_kernel(dest_ref, x_ref, zeros_ref, xs_ref, sem):
    del zeros_ref
    tm = x_ref.shape[0]
    nt = tm * pl.num_programs(0)
    base = pl.program_id(0) * tm

    def issue(i, carry):
        for k in range(TOP_K):
            d = dest_ref[k * nt + base + i]
            _row_copy(x_ref.at[pl.ds(i, 1), :], xs_ref.at[pl.ds(d, 1), :], sem).start()
        return carry

    lax.fori_loop(0, tm, issue, 0)

    def drain(i, carry):
        _row_copy(x_ref.at[pl.ds(0, 1), :], xs_ref.at[pl.ds(0, 1), :], sem).wait()
        return carry

    lax.fori_loop(0, tm * TOP_K, drain, 0)


def _dispatch(dest_flat, x, rows, tm=256):
    nt, d = x.shape
    zeros = jnp.zeros((rows, d), x.dtype)
    return pl.pallas_call(
        _dispatch_kernel,
        grid_spec=pltpu.PrefetchScalarGridSpec(
            num_scalar_prefetch=1,
            grid=(nt // tm,),
            in_specs=[pl.BlockSpec((tm, d), lambda i, dest: (i, 0)),
                      pl.BlockSpec(memory_space=pl.ANY)],
            out_specs=pl.BlockSpec(memory_space=pl.ANY),
            scratch_shapes=[pltpu.SemaphoreType.DMA(())],
        ),
        out_shape=jax.ShapeDtypeStruct((rows, d), x.dtype),
        input_output_aliases={2: 0},
        compiler_params=_cparams(("arbitrary",)),
        name="moe_dispatch",
    )(dest_flat, x, zeros)


def _expert_kernel(exp_ref, first_ref, active_ref, xs_ref, wgu_ref, bgu_ref, wdn_ref, bdn_ref,
                   y_ref, wgu_bf, wdn_bf):
    del exp_ref
    i = pl.program_id(0)

    @pl.when(first_ref[i] == 1)
    def _():
        wgu_bf[...] = wgu_ref[...].astype(BF16)
        wdn_bf[...] = wdn_ref[...].astype(BF16)

    @pl.when(active_ref[i] == 1)
    def _():
        hb = jnp.dot(xs_ref[...].astype(BF16), wgu_bf[...], preferred_element_type=F32) + bgu_ref[...]
        gate = jnp.minimum(hb[:, :D_FF], SWIGLU_LIMIT)
        up = jnp.clip(hb[:, D_FF:], -SWIGLU_LIMIT, SWIGLU_LIMIT)
        act = (up + 1.0) * gate * jax.nn.sigmoid(SWIGLU_ALPHA * gate)
        y_ref[...] = jnp.dot(act.astype(BF16), wdn_bf[...], preferred_element_type=F32) + bdn_ref[...]

    @pl.when(active_ref[i] == 0)
    def _():
        y_ref[...] = jnp.zeros_like(y_ref)


def _experts(block_exp, block_first, block_active, xs, w_gu, b_gu, w_dn, b_dn, layer):
    rows, d = xs.shape
    n_blocks = rows // MOE_TM
    return pl.pallas_call(
        _expert_kernel,
        grid_spec=pltpu.PrefetchScalarGridSpec(
            num_scalar_prefetch=3,
            grid=(n_blocks,),
            in_specs=[
                pl.BlockSpec((MOE_TM, d), lambda i, e, f, a: (i, 0)),
                pl.BlockSpec((None, None, d, 2 * D_FF), lambda i, e, f, a: (layer, e[i], 0, 0)),
                pl.BlockSpec((None, None, 1, 2 * D_FF), lambda i, e, f, a: (layer, e[i], 0, 0)),
                pl.BlockSpec((None, None, D_FF, d), lambda i, e, f, a: (layer, e[i], 0, 0)),
                pl.BlockSpec((None, None, 1, d), lambda i, e, f, a: (layer, e[i], 0, 0)),
            ],
            out_specs=pl.BlockSpec((MOE_TM, d), lambda i, e, f, a: (i, 0)),
            scratch_shapes=[pltpu.VMEM((d, 2 * D_FF), BF16), pltpu.VMEM((D_FF, d), BF16)],
        ),
        out_shape=jax.ShapeDtypeStruct((rows, d), F32),
        compiler_params=_cparams(("arbitrary",), VMEM_LIMIT),
        name="moe_experts",
    )(block_exp, block_first, block_active, xs, w_gu,
      b_gu.reshape(DEPTH, N_EXPERTS, 1, 2 * D_FF), w_dn, b_dn.reshape(DEPTH, N_EXPERTS, 1, d))


def _combine_kernel(dest_ref, x_ref, gate_ref, g_ref, b_ref, y_ref, out_ref, buf, sem):
    tm = x_ref.shape[0]
    nt = tm * pl.num_programs(0)
    base = pl.program_id(0) * tm

    def issue(i, carry):
        for k in range(TOP_K):
            d = dest_ref[k * nt + base + i]
            _row_copy(y_ref.at[pl.ds(d, 1), :], buf.at[k, pl.ds(i, 1), :], sem).start()
        return carry

    lax.fori_loop(0, tm, issue, 0)

    def drain(i, carry):
        _row_copy(y_ref.at[pl.ds(0, 1), :], buf.at[0, pl.ds(0, 1), :], sem).wait()
        return carry

    lax.fori_loop(0, tm * TOP_K, drain, 0)

    gates = gate_ref[...]
    ffn = gates[:, 0:1] * buf[0]
    for k in range(1, TOP_K):
        ffn += gates[:, k:k + 1] * buf[k]
    y = DEEPNORM_ALPHA * x_ref[...] + ffn
    out_ref[...] = _layer_norm_rows(y, g_ref[...], b_ref[...])


def _combine(dest_flat, x, gates, g, b, y_rows, tm=256):
    nt, d = x.shape
    return pl.pallas_call(
        _combine_kernel,
        grid_spec=pltpu.PrefetchScalarGridSpec(
            num_scalar_prefetch=1,
            grid=(nt // tm,),
            in_specs=[pl.BlockSpec((tm, d), lambda i, dest: (i, 0)),
                      pl.BlockSpec((tm, LANES), lambda i, dest: (i, 0)),
                      pl.BlockSpec((1, d), lambda i, dest: (0, 0)),
                      pl.BlockSpec((1, d), lambda i, dest: (0, 0)),
                      pl.BlockSpec(memory_space=pl.ANY)],
            out_specs=pl.BlockSpec((tm, d), lambda i, dest: (i, 0)),
            scratch_shapes=[pltpu.VMEM((TOP_K, tm, d), F32), pltpu.SemaphoreType.DMA(())],
        ),
        out_shape=jax.ShapeDtypeStruct((nt, d), F32),
        compiler_params=_cparams(("arbitrary",)),
        name="moe_combine",
    )(dest_flat, x, gates, g.reshape(1, d), b.reshape(1, d), y_rows)


def _moe_layer(x, layer, w_router, b_router, w_gate_up, b_gate_up, w_down, b_down, ln_g, ln_b):
    nt, _ = x.shape
    idx, rank, gates, counts = _router(x, w_router[layer], b_router[layer])
    counts = counts[:, 0]
    padded = (counts + MOE_TM - 1) // MOE_TM * MOE_TM
    pend = jnp.cumsum(padded)
    pstart = pend - padded
    experts = jnp.arange(N_EXPERTS, dtype=jnp.int32)
    start_of = jnp.sum(jnp.where(idx[..., None] == experts, pstart, 0), axis=-1)
    dest = (start_of + rank).reshape(TOP_K * nt).astype(jnp.int32)
    n_blocks = nt * TOP_K // MOE_TM + N_EXPERTS
    blk_start = jnp.arange(n_blocks, dtype=jnp.int32) * MOE_TM
    active = blk_start < pend[-1]
    exp_raw = jnp.minimum(jnp.sum(pend[None, :] <= blk_start[:, None], axis=1), N_EXPERTS - 1).astype(jnp.int32)
    last_exp = jnp.max(jnp.where(active, exp_raw, 0))
    block_exp = jnp.where(active, exp_raw, last_exp).astype(jnp.int32)
    prev_exp = jnp.concatenate([jnp.full((1,), -1, jnp.int32), block_exp[:-1]])
    block_first = (active & (block_exp != prev_exp)).astype(jnp.int32)
    xs = _dispatch(dest, x, n_blocks * MOE_TM)
    y_rows = _experts(block_exp, block_first, active.astype(jnp.int32), xs,
                      w_gate_up, b_gate_up, w_down, b_down, layer)
    return _combine(dest, x, gates, ln_g, ln_b, y_rows)


def _split3(v):
    hi = v.astype(BF16)
    r1 = v - hi.astype(F32)
    mid = r1.astype(BF16)
    lo = (r1 - mid.astype(F32)).astype(BF16)
    return hi, mid, lo


def _shared_proj_kernel(x_ref, w_ref, wfh_ref, wfl_ref, bf_ref, eq_ref, ek_ref, oq_ref, ok_ref,
                        kq_ref, v_ref, qq_ref, qm_ref, carry_ref):
    @pl.when(pl.program_id(1) == 0)
    def _():
        carry_ref[...] = jnp.zeros_like(carry_ref)

    tm = x_ref.shape[0]
    x = x_ref[...]
    xh = x.astype(BF16)
    xl = (x - xh.astype(F32)).astype(BF16)
    h = jnp.dot(xh, w_ref[...], preferred_element_type=F32)
    f = (jnp.dot(xh, wfh_ref[...], preferred_element_type=F32)
         + jnp.dot(xl, wfh_ref[...], preferred_element_type=F32)
         + jnp.dot(xh, wfl_ref[...], preferred_element_type=F32)) + bf_ref[...]
    ls = jnp.minimum(f, 0.0) - jnp.log1p(jnp.exp(-jnp.abs(f)))
    ls = jnp.where(_head_mask(LANES, 0, N_FOX_HEADS), ls, 0.0)
    row = lax.broadcasted_iota(jnp.int32, (tm, tm), 0)
    col = lax.broadcasted_iota(jnp.int32, (tm, tm), 1)
    tri = jnp.where(row >= col, 1.0, 0.0).astype(BF16)
    a, b, c = _split3(ls)
    cum = (jnp.dot(tri, a, preferred_element_type=F32) + jnp.dot(tri, b, preferred_element_type=F32)
           + jnp.dot(tri, c, preferred_element_type=F32)) + carry_ref[...]
    carry_ref[...] = cum[tm - 1:tm, :]
    ch, cm, cl = _split3(cum)
    cat = (ch.astype(F32) + pltpu.roll(cm.astype(F32), 16, 1) + pltpu.roll(cl.astype(F32), 32, 1)).astype(BF16)
    kq = h[:, :FOX_QK_W] + jnp.dot(cat, ek_ref[...], preferred_element_type=F32) + ok_ref[...]
    qq = (h[:, FOX_QK_W + FOX_W:2 * FOX_QK_W + FOX_W]
          + jnp.dot(cat, eq_ref[...], preferred_element_type=F32) + oq_ref[...])
    kq_ref[...] = kq.astype(BF16)
    v_ref[...] = h[:, FOX_QK_W:FOX_QK_W + FOX_W].astype(BF16)
    qq_ref[...] = qq.astype(BF16)
    qm_ref[...] = h[:, 2 * FOX_QK_W + FOX_W:].astype(BF16)


def _widen_heads(w):
    d = w.shape[0]
    w3 = w.reshape(d, N_FOX_HEADS, HEAD_DIM)
    return jnp.concatenate([w3, jnp.zeros_like(w3)], axis=-1).reshape(d, FOX_QK_W)


def _spread_matrices():
    src = jnp.arange(LANES)[:, None]
    dst = jnp.arange(FOX_QK_W)[None, :]
    head = dst // FOX_HEAD_W
    off = dst % FOX_HEAD_W
    part = src // 16
    is_src = (src % 16 == head) & (src % 16 < N_FOX_HEADS) & (part < 3)
    eq = jnp.where(is_src & (off == HEAD_DIM + part), 1.0, 0.0).astype(BF16)
    ek = jnp.where(is_src & (off == HEAD_DIM + 3 + part), -1.0, 0.0).astype(BF16)
    off1 = jnp.arange(FOX_QK_W) % FOX_HEAD_W
    ones_q = ((off1 >= HEAD_DIM + 3) & (off1 < HEAD_DIM + 6)).astype(F32).reshape(1, FOX_QK_W)
    ones_k = ((off1 >= HEAD_DIM) & (off1 < HEAD_DIM + 3)).astype(F32).reshape(1, FOX_QK_W)
    return eq, ek, ones_q, ones_k


def _shared_proj(x, batch, w_shared_kvf, b_forget, w_in_b, tm=256):
    nt, d = x.shape
    per_b = nt // batch // tm
    w_k = _widen_heads(w_shared_kvf[:, :FOX_W])
    w_v = w_shared_kvf[:, FOX_W:2 * FOX_W]
    w_q = _widen_heads(w_in_b[:, :FOX_W] * ATTN_SCALE)
    w_qm = w_in_b[:, FOX_W:] * ATTN_SCALE
    w_big = jnp.concatenate([w_k, w_v, w_q, w_qm], axis=1).astype(BF16)
    w_f = jnp.pad(w_shared_kvf[:, 2 * FOX_W:], ((0, 0), (0, LANES - N_FOX_HEADS)))
    w_fh = w_f.astype(BF16)
    w_fl = (w_f - w_fh.astype(F32)).astype(BF16)
    b_f = jnp.pad(b_forget, (0, LANES - N_FOX_HEADS)).reshape(1, LANES)
    eq, ek, ones_q, ones_k = _spread_matrices()
    nbig = w_big.shape[1]
    row = lambda bi, i: (bi * per_b + i, 0)
    const = lambda bi, i: (0, 0)
    return pl.pallas_call(
        _shared_proj_kernel,
        grid=(batch, per_b),
        in_specs=[pl.BlockSpec((tm, d), row),
                  pl.BlockSpec((d, nbig), const),
                  pl.BlockSpec((d, LANES), const), pl.BlockSpec((d, LANES), const),
                  pl.BlockSpec((1, LANES), const),
                  pl.BlockSpec((LANES, FOX_QK_W), const), pl.BlockSpec((LANES, FOX_QK_W), const),
                  pl.BlockSpec((1, FOX_QK_W), const), pl.BlockSpec((1, FOX_QK_W), const)],
        out_specs=[pl.BlockSpec((tm, FOX_QK_W), row), pl.BlockSpec((tm, FOX_W), row),
                   pl.BlockSpec((tm, FOX_QK_W), row), pl.BlockSpec((tm, MEM_W), row)],
        out_shape=[jax.ShapeDtypeStruct((nt, FOX_QK_W), BF16), jax.ShapeDtypeStruct((nt, FOX_W), BF16),
                   jax.ShapeDtypeStruct((nt, FOX_QK_W), BF16), jax.ShapeDtypeStruct((nt, MEM_W), BF16)],
        scratch_shapes=[pltpu.VMEM((1, LANES), F32)],
        compiler_params=_cparams(("arbitrary", "arbitrary"), VMEM_LIMIT),
        name="shared_proj",
    )(x, w_big, w_fh, w_fl, b_f, eq, ek, ones_q, ones_k)


def _fox_kernel(q_ref, k_ref, v_ref, o_ref):
    tq = q_ref.shape[0]
    i = pl.program_id(2)
    row = lax.broadcasted_iota(jnp.int32, (tq, tq), 0)
    col = lax.broadcasted_iota(jnp.int32, (tq, tq), 1)

    def attend(kv):
        out = jnp.zeros((tq, LANES), F32)
        for hh in range(2):
            qh = q_ref[:, hh * FOX_HEAD_W:(hh + 1) * FOX_HEAD_W]
            kh = k_ref[:kv, hh * FOX_HEAD_W:(hh + 1) * FOX_HEAD_W]
            sc = lax.dot_general(qh, kh, (((1,), (1,)), ((), ())), preferred_element_type=F32)
            diag = jnp.where(col <= row, sc[:, kv - tq:], -jnp.inf)
            sc = diag if kv == tq else jnp.concatenate([sc[:, :kv - tq], diag], axis=1)
            m = jnp.max(sc, axis=-1, keepdims=True)
            p = jnp.exp(sc - m)
            den = jnp.sum(p, axis=-1, keepdims=True)
            acc = jnp.dot(p.astype(BF16), v_ref[:kv, :], preferred_element_type=F32)
            out = jnp.where(_head_mask(LANES, hh * HEAD_DIM, (hh + 1) * HEAD_DIM), acc / den, out)
        o_ref[...] = out.astype(BF16)

    for c in range(k_ref.shape[0] // tq):
        pl.when(i == c)(functools.partial(attend, (c + 1) * tq))


def _fox_attn(qq, kq, v, batch, tq=256):
    nt = qq.shape[0]
    s = nt // batch
    per_b = s // tq
    pairs = N_FOX_HEADS // 2
    return pl.pallas_call(
        _fox_kernel,
        grid=(batch, pairs, per_b),
        in_specs=[pl.BlockSpec((tq, 2 * FOX_HEAD_W), lambda bi, p, i: (bi * per_b + i, p)),
                  pl.BlockSpec((s, 2 * FOX_HEAD_W), lambda bi, p, i: (bi, p)),
                  pl.BlockSpec((s, LANES), lambda bi, p, i: (bi, p))],
        out_specs=pl.BlockSpec((tq, LANES), lambda bi, p, i: (bi * per_b + i, p)),
        out_shape=jax.ShapeDtypeStruct((nt, FOX_W), BF16),
        compiler_params=_cparams(("parallel", "parallel", "parallel")),
        name="fox_attn",
    )(qq, kq, v)


def _rope_tables(seq):
    inv = 1.0 / (ROPE_THETA ** (jnp.arange(0, HEAD_DIM, 2, dtype=F32) / HEAD_DIM))
    ang = jnp.arange(seq, dtype=F32)[:, None] * inv[None, :]
    return jnp.tile(jnp.cos(ang), (1, DIL_HEADS)), jnp.tile(jnp.sin(ang), (1, DIL_HEADS))


def _rotary_layout(w):
    d = w.shape[0]
    w4 = w.reshape(d, DIL_HEADS, 2, HEAD_DIM // 2)
    return w4.transpose(0, 2, 1, 3).reshape(d, GROUP_W)


def _group_weights(w_in, g, with_mem):
    base = g * 3 * GROUP_W
    cols = [_rotary_layout(w_in[:, base:base + GROUP_W] * ATTN_SCALE),
            _rotary_layout(w_in[:, base + GROUP_W:base + 2 * GROUP_W]),
            w_in[:, base + 2 * GROUP_W:base + 3 * GROUP_W]]
    if with_mem:
        cols.append(w_in[:, DIL_QKV_W:] * ATTN_SCALE)
    return jnp.concatenate(cols, axis=1).astype(BF16)


def kernel(x, mem, w_in_a, w_out_a, w_in_b, w_out_b, w_shared_kvf, b_forget, w_mem_kv, ln_mix_g, ln_mix_b,
           ln_ffn_g, ln_ffn_b, w_router, b_router, w_gate_up, b_gate_up, w_down, b_down):
    b, s, d = x.shape
    nt = b * s
    mem2 = mem.reshape(b * mem.shape[1], d)
    cos_t, sin_t = _rope_tables(s)
    moe = functools.partial(_moe_layer, w_router=w_router, b_router=b_router, w_gate_up=w_gate_up,
                            b_gate_up=b_gate_up, w_down=w_down, b_down=b_down)

    outs, lses, q_mem = [], [], None
    for g, (_, dil) in enumerate(DILATED_PATTERNS):
        res = _group_proj(x, _group_weights(w_in_a[0], g, g == 0), cos_t, sin_t, dil)
        if g == 0:
            q_mem = res[3].reshape(nt, MEM_W)
        o, lse = _dilated_attn(res[0], res[1], res[2], dil)
        outs.append(o)
        lses.append(lse)
    mkv0 = _matmul(mem2, w_mem_kv[0].astype(BF16), BF16, 512)
    memo = _mem_attn(q_mem, mkv0, b)
    x2 = x.reshape(nt, d)
    x2 = _out_proj(_out_proj_a_kernel, outs + lses + [memo], x2, w_out_a[0].astype(BF16),
                   ln_mix_g[0], ln_mix_b[0], "out_proj_a")
    x2 = moe(x2, 0, ln_g=ln_ffn_g[0], ln_b=ln_ffn_b[0])

    kq, v_sh, qq, q_mem = _shared_proj(x2, b, w_shared_kvf, b_forget, w_in_b[0])

    fox = _fox_attn(qq, kq, v_sh, b)
    mkv1 = _matmul(mem2, w_mem_kv[1].astype(BF16), BF16, 512)
    memo = _mem_attn(q_mem, mkv1, b)
    x2 = _out_proj(_out_proj_b_kernel, [fox, memo], x2, w_out_b[0].astype(BF16),
                   ln_mix_g[1], ln_mix_b[1], "out_proj_b")
    x2 = moe(x2, 1, ln_g=ln_ffn_g[1], ln_b=ln_ffn_b[1])
    return x2.reshape(b, s, d)
```

```python
import functools

import jax
import jax.numpy as jnp
from jax import lax
from jax.experimental import pallas as pl
from jax.experimental.pallas import tpu as pltpu

F32 = jnp.float32
BF16 = jnp.bfloat16

D_MODEL = 1024
DEPTH = 2
HEAD_DIM = 64
N_MEM_HEADS = 4
DILATED_PATTERNS = ((128, 1), (512, 4), (2048, 16))
N_DIL_GROUPS = 3
DIL_HEADS = 4
N_FOX_HEADS = 12
GROUP_W = DIL_HEADS * HEAD_DIM
MEM_W = N_MEM_HEADS * HEAD_DIM
FOX_W = N_FOX_HEADS * HEAD_DIM
DIL_QKV_W = N_DIL_GROUPS * 3 * GROUP_W
N_EXPERTS = 32
TOP_K = 4
D_FF = D_MODEL
SWIGLU_LIMIT = 7.0
SWIGLU_ALPHA = 1.702
ROPE_THETA = 10000.0
LN_EPS = 1e-5
ATTN_SCALE = HEAD_DIM ** -0.5
DEEPNORM_ALPHA = (2.0 * DEPTH) ** 0.25
DIL_STEPS = 128

LANES = 128
FOX_HEAD_W = LANES
FOX_QK_W = N_FOX_HEADS * FOX_HEAD_W
MOE_TM = 256
VMEM_LIMIT = 56 * 1024 * 1024


def _cparams(sem, vmem=None):
    return pltpu.CompilerParams(dimension_semantics=sem, vmem_limit_bytes=vmem)


def _head_mask(width, lo, hi):
    lane = lax.broadcasted_iota(jnp.int32, (1, width), 1)
    return (lane >= lo) & (lane < hi)


def _layer_norm_rows(y, g, b):
    mu = jnp.mean(y, axis=-1, keepdims=True)
    yc = y - mu
    var = jnp.mean(yc * yc, axis=-1, keepdims=True)
    return yc * lax.rsqrt(var + LN_EPS) * g + b


def _matmul_kernel(x_ref, w_ref, o_ref):
    o_ref[...] = jnp.dot(x_ref[...].astype(BF16), w_ref[...],
                         preferred_element_type=F32).astype(o_ref.dtype)


def _matmul(x, w, out_dtype, tm):
    m, k = x.shape
    n = w.shape[1]
    return pl.pallas_call(
        _matmul_kernel,
        grid=(m // tm,),
        in_specs=[pl.BlockSpec((tm, k), lambda i: (i, 0)),
                  pl.BlockSpec((k, n), lambda i: (0, 0))],
        out_specs=pl.BlockSpec((tm, n), lambda i: (i, 0)),
        out_shape=jax.ShapeDtypeStruct((m, n), out_dtype),
        compiler_params=_cparams(("parallel",)),
        name="matmul",
    )(x, w)


def _group_proj_kernel(x_ref, w_ref, cos_ref, sin_ref, *o_refs):
    h = jnp.dot(x_ref[0].astype(BF16), w_ref[...], preferred_element_type=F32)
    cos = cos_ref[...]
    sin = sin_ref[...]
    half = GROUP_W // 2
    for j in range(2):
        t1 = h[:, j * GROUP_W:j * GROUP_W + half]
        t2 = h[:, j * GROUP_W + half:(j + 1) * GROUP_W]
        o_refs[j][0, 0, :, :half] = (t1 * cos - t2 * sin).astype(BF16)
        o_refs[j][0, 0, :, half:] = (t2 * cos + t1 * sin).astype(BF16)
    for j in range(2, len(o_refs)):
        o_refs[j][0, 0] = h[:, j * GROUP_W:(j + 1) * GROUP_W].astype(BF16)


def _group_proj(x, w, cos_t, sin_t, dil):
    b, s, d = x.shape
    length = s // dil
    lt = min(length, 512)
    n_out = w.shape[1] // GROUP_W
    xv = x.reshape(b, length, dil * d)
    cv = cos_t.reshape(length, dil * LANES)
    sv = sin_t.reshape(length, dil * LANES)
    o_spec = pl.BlockSpec((1, 1, lt, GROUP_W), lambda bi, r, l: (bi, r, l, 0))
    return pl.pallas_call(
        _group_proj_kernel,
        grid=(b, dil, length // lt),
        in_specs=[pl.BlockSpec((1, lt, d), lambda bi, r, l: (bi, l, r)),
                  pl.BlockSpec(w.shape, lambda bi, r, l: (0, 0)),
                  pl.BlockSpec((lt, LANES), lambda bi, r, l: (l, r)),
                  pl.BlockSpec((lt, LANES), lambda bi, r, l: (l, r))],
        out_specs=[o_spec] * n_out,
        out_shape=[jax.ShapeDtypeStruct((b, dil, length, GROUP_W), BF16)] * n_out,
        compiler_params=_cparams(("parallel", "parallel", "parallel")),
        name=f"group_proj_d{dil}",
    )(xv, w, cv, sv)


def _dilated_attn_kernel(q_ref, kp_ref, kc_ref, vp_ref, vc_ref, o_ref, lse_ref):
    n = pl.program_id(2)
    n_res = q_ref.shape[1]
    n_sub = q_ref.shape[2] // DIL_STEPS
    qi = lax.broadcasted_iota(jnp.int32, (DIL_STEPS, 2 * DIL_STEPS), 0)
    kj = lax.broadcasted_iota(jnp.int32, (DIL_STEPS, 2 * DIL_STEPS), 1)
    band = (kj >= qi) & (kj <= qi + DIL_STEPS)
    half = GROUP_W // 2
    hw = HEAD_DIM // 2
    for r in range(n_res):
        for j in range(n_sub):
            lo, hi = j * DIL_STEPS, (j + 1) * DIL_STEPS
            q = q_ref[0, r, lo:hi, :]
            if j == 0:
                kk = jnp.concatenate([kp_ref[0, r], kc_ref[0, r, lo:hi, :]], axis=0)
                vv = jnp.concatenate([vp_ref[0, r], vc_ref[0, r, lo:hi, :]], axis=0)
                valid = band & ((n > 0) | (kj >= DIL_STEPS))
            else:
                kk = kc_ref[0, r, lo - DIL_STEPS:hi, :]
                vv = vc_ref[0, r, lo - DIL_STEPS:hi, :]
                valid = band
            out = jnp.zeros((DIL_STEPS, GROUP_W), F32)
            lse = jnp.zeros((DIL_STEPS, GROUP_W), F32)
            for h in range(DIL_HEADS):
                qmask = (_head_mask(GROUP_W, h * hw, (h + 1) * hw)
                         | _head_mask(GROUP_W, half + h * hw, half + (h + 1) * hw))
                qm = jnp.where(qmask, q, jnp.zeros_like(q))
                sc = lax.dot_general(qm, kk, (((1,), (1,)), ((), ())), preferred_element_type=F32)
                sc = jnp.where(valid, sc, -jnp.inf)
                m = jnp.max(sc, axis=-1, keepdims=True)
                p = jnp.exp(sc - m)
                den = jnp.sum(p, axis=-1, keepdims=True)
                o_all = jnp.dot(p.astype(BF16), vv, preferred_element_type=F32)
                vmask = _head_mask(GROUP_W, h * HEAD_DIM, (h + 1) * HEAD_DIM)
                out = jnp.where(vmask, o_all / den, out)
                lse = jnp.where(vmask, m + jnp.log(den), lse)
            o_ref[0, lo:hi, r * GROUP_W:(r + 1) * GROUP_W] = out.astype(BF16)
            lse_ref[0, lo:hi, r * GROUP_W:(r + 1) * GROUP_W] = lse


DIL_UNITS = 4


def _dilated_attn(q, k, v, dil):
    b, _, length, _ = q.shape
    n_sub = min(length // DIL_STEPS, DIL_UNITS)
    n_res = DIL_UNITS // n_sub
    rows = n_sub * DIL_STEPS
    cur = pl.BlockSpec((1, n_res, rows, GROUP_W), lambda bi, r, n: (bi, r, n, 0))
    prev = pl.BlockSpec((1, n_res, DIL_STEPS, GROUP_W),
                        lambda bi, r, n: (bi, r, jnp.maximum(n * n_sub - 1, 0), 0))
    out = pl.BlockSpec((1, rows, n_res * GROUP_W), lambda bi, r, n: (bi, n, r))
    o, lse = pl.pallas_call(
        _dilated_attn_kernel,
        grid=(b, dil // n_res, length // rows),
        in_specs=[cur, prev, cur, prev, cur],
        out_specs=[out, out],
        out_shape=[jax.ShapeDtypeStruct((b, length, dil * GROUP_W), BF16),
                   jax.ShapeDtypeStruct((b, length, dil * GROUP_W), F32)],
        compiler_params=_cparams(("parallel", "parallel", "parallel")),
        name=f"dilated_attn_d{dil}",
    )(q, k, k, v, v)
    return o.reshape(b * length * dil, GROUP_W), lse.reshape(b * length * dil, GROUP_W)


def _mem_attn_kernel(q_ref, kv_ref, o_ref):
    q = q_ref[...]
    mk = kv_ref[:, :MEM_W]
    mv = kv_ref[:, MEM_W:]
    out = jnp.zeros(q.shape, F32)
    for h in range(N_MEM_HEADS):
        hmask = _head_mask(MEM_W, h * HEAD_DIM, (h + 1) * HEAD_DIM)
        qm = jnp.where(hmask, q, jnp.zeros_like(q))
        sc = lax.dot_general(qm, mk, (((1,), (1,)), ((), ())), preferred_element_type=F32)
        m = jnp.max(sc, axis=-1, keepdims=True)
        p = jnp.exp(sc - m)
        den = jnp.sum(p, axis=-1, keepdims=True)
        o_all = jnp.dot(p.astype(BF16), mv, preferred_element_type=F32)
        out = jnp.where(hmask, o_all / den, out)
    o_ref[...] = out.astype(BF16)


def _mem_attn(q, mkv, batch, tq=512):
    nt = q.shape[0]
    per_b = nt // batch // tq
    m_len = mkv.shape[0] // batch
    return pl.pallas_call(
        _mem_attn_kernel,
        grid=(batch, per_b),
        in_specs=[pl.BlockSpec((tq, MEM_W), lambda bi, i: (bi * per_b + i, 0)),
                  pl.BlockSpec((m_len, 2 * MEM_W), lambda bi, i: (bi, 0))],
        out_specs=pl.BlockSpec((tq, MEM_W), lambda bi, i: (bi * per_b + i, 0)),
        out_shape=jax.ShapeDtypeStruct((nt, MEM_W), BF16),
        compiler_params=_cparams(("parallel", "parallel")),
        name="mem_attn",
    )(q, mkv)


def _out_proj_a_kernel(o0, o1, o2, l0, l1, l2, memo, x_ref, w_ref, g_ref, b_ref, out_ref):
    la, lb, lc = l0[...], l1[...], l2[...]
    mx = jnp.maximum(jnp.maximum(la, lb), lc)
    ea, eb, ec = jnp.exp(la - mx), jnp.exp(lb - mx), jnp.exp(lc - mx)
    z = ea + eb + ec
    dil = (ea * o0[...].astype(F32) + eb * o1[...].astype(F32) + ec * o2[...].astype(F32)) / z
    mix = jnp.dot(dil.astype(BF16), w_ref[:GROUP_W, :], preferred_element_type=F32)
    mix += jnp.dot(memo[...], w_ref[GROUP_W:, :], preferred_element_type=F32)
    y = DEEPNORM_ALPHA * x_ref[...] + mix
    out_ref[...] = _layer_norm_rows(y, g_ref[...], b_ref[...])


def _out_proj_b_kernel(fox, memo, x_ref, w_ref, g_ref, b_ref, out_ref):
    mix = jnp.dot(fox[...], w_ref[:FOX_W, :], preferred_element_type=F32)
    mix += jnp.dot(memo[...], w_ref[FOX_W:, :], preferred_element_type=F32)
    y = DEEPNORM_ALPHA * x_ref[...] + mix
    out_ref[...] = _layer_norm_rows(y, g_ref[...], b_ref[...])


def _out_proj(kernel_fn, acts, x, w, g, b, name, tm=512):
    nt, d = x.shape
    row = lambda i: (i, 0)
    const = lambda i: (0, 0)
    return pl.pallas_call(
        kernel_fn,
        grid=(nt // tm,),
        in_specs=[pl.BlockSpec((tm, a.shape[1]), row) for a in acts]
        + [pl.BlockSpec((tm, d), row), pl.BlockSpec(w.shape, const),
           pl.BlockSpec((1, d), const), pl.BlockSpec((1, d), const)],
        out_specs=pl.BlockSpec((tm, d), row),
        out_shape=jax.ShapeDtypeStruct((nt, d), F32),
        compiler_params=_cparams(("parallel",)),
        name=name,
    )(*acts, x, w, g.reshape(1, d), b.reshape(1, d))


def _router_kernel(x_ref, wt_ref, b_ref, idx_ref, rank_ref, gate_ref, cnt_ref, carry_ref):
    @pl.when(pl.program_id(0) == 0)
    def _():
        carry_ref[...] = jnp.zeros_like(carry_ref)

    tm = x_ref.shape[0]
    logits = lax.dot_general(wt_ref[...], x_ref[...], (((1,), (1,)), ((), ())),
                             precision=lax.Precision.HIGHEST, preferred_element_type=F32) + b_ref[...]
    sub = lax.broadcasted_iota(jnp.int32, (N_EXPERTS, tm), 0).astype(F32)
    work = logits
    vals, sels, idxs = [], [], []
    for _ in range(TOP_K):
        mk = jnp.max(work, axis=0, keepdims=True)
        ik = jnp.min(jnp.where(work == mk, sub, float(N_EXPERTS)), axis=0, keepdims=True)
        sel = sub == ik
        work = jnp.where(sel, -jnp.inf, work)
        vals.append(mk)
        sels.append(sel)
        idxs.append(ik)
    es = [jnp.exp(v - vals[0]) for v in vals]
    z = es[0] + es[1] + es[2] + es[3]
    hot = jnp.zeros((N_EXPERTS, tm), F32)
    for sel in sels:
        hot = jnp.where(sel, 1.0, hot)
    row = lax.broadcasted_iota(jnp.int32, (tm, tm), 0)
    col = lax.broadcasted_iota(jnp.int32, (tm, tm), 1)
    tri = jnp.where(row < col, 1.0, 0.0).astype(BF16)
    before = jnp.dot(hot.astype(BF16), tri, preferred_element_type=F32) + carry_ref[...]
    k_sub = lax.broadcasted_iota(jnp.int32, (TOP_K, tm), 0)
    g_sub = lax.broadcasted_iota(jnp.int32, (LANES, tm), 0)
    idx_o = jnp.zeros((TOP_K, tm), F32)
    rank_o = jnp.zeros((TOP_K, tm), F32)
    gate_t = jnp.zeros((LANES, tm), F32)
    for k in range(TOP_K):
        rk = jnp.sum(jnp.where(sels[k], before, 0.0), axis=0, keepdims=True)
        idx_o = jnp.where(k_sub == k, idxs[k], idx_o)
        rank_o = jnp.where(k_sub == k, rk, rank_o)
        gate_t = jnp.where(g_sub == k, es[k] / z, gate_t)
    idx_ref[...] = idx_o.astype(jnp.int32)
    rank_ref[...] = rank_o.astype(jnp.int32)
    gate_ref[...] = gate_t.T
    carry_ref[...] += jnp.sum(hot, axis=1, keepdims=True)
    cnt_ref[...] = carry_ref[...].astype(jnp.int32)


def _router(x, w_r, b_r, tm=512):
    nt, d = x.shape
    const = lambda i: (0, 0)
    return pl.pallas_call(
        _router_kernel,
        grid=(nt // tm,),
        in_specs=[pl.BlockSpec((tm, d), lambda i: (i, 0)), pl.BlockSpec((N_EXPERTS, d), const),
                  pl.BlockSpec((N_EXPERTS, 1), const)],
        out_specs=[pl.BlockSpec((TOP_K, tm), lambda i: (0, i)), pl.BlockSpec((TOP_K, tm), lambda i: (0, i)),
                   pl.BlockSpec((tm, LANES), lambda i: (i, 0)), pl.BlockSpec((N_EXPERTS, 1), const)],
        out_shape=[jax.ShapeDtypeStruct((TOP_K, nt), jnp.int32),
                   jax.ShapeDtypeStruct((TOP_K, nt), jnp.int32),
                   jax.ShapeDtypeStruct((nt, LANES), F32),
                   jax.ShapeDtypeStruct((N_EXPERTS, 1), jnp.int32)],
        scratch_shapes=[pltpu.VMEM((N_EXPERTS, 1), F32)],
        compiler_params=_cparams(("arbitrary",)),
        name="router",
    )(x, w_r.T, b_r.reshape(N_EXPERTS, 1))


def _row_copy(src, dst, sem):
    return pltpu.make_async_copy(src, dst, sem)


def _pack_bf16_pairs(x):
    w = x.shape[1] // 2
    hi = lax.bitcast_convert_type(x[:, :w].astype(BF16).astype(F32), jnp.uint32)
    lo = lax.bitcast_convert_type(x[:, w:].astype(BF16).astype(F32), jnp.uint32)
    return hi | (lo >> 16)


def _unpack_bf16_pairs(p):
    hi = lax.bitcast_convert_type(p & jnp.uint32(0xFFFF0000), F32).astype(BF16)
    lo = lax.bitcast_convert_type(p << 16, F32).astype(BF16)
    return hi, lo


def _dispatch_kernel(dest_ref, tail_ref, x_ref, zeros_ref, xs_ref, packed, sem, zsem):
    tm = x_ref.shape[0]
    nt = tm * pl.num_programs(0)
    base = pl.program_id(0) * tm

    packed[...] = _pack_bf16_pairs(x_ref[...])

    def tail_copy(e):
        t = pl.multiple_of(jnp.maximum(tail_ref[e], 0), MOE_TM)
        return _row_copy(zeros_ref, xs_ref.at[pl.ds(t, MOE_TM), :], zsem)

    def unused_copy(j):
        return _row_copy(zeros_ref, xs_ref.at[pl.ds(pl.multiple_of(j * MOE_TM, MOE_TM), MOE_TM), :], zsem)

    @pl.when(pl.program_id(0) == 0)
    def _():
        n_blocks = xs_ref.shape[0] // MOE_TM
        first_unused = tail_ref[N_EXPERTS]
        for e in range(N_EXPERTS):
            pl.when(tail_ref[e] >= 0)(lambda e=e: tail_copy(e).start())
        lax.fori_loop(first_unused, n_blocks, lambda j, c: (unused_copy(j).start(), c)[1], 0)
        for e in range(N_EXPERTS):
            pl.when(tail_ref[e] >= 0)(lambda e=e: tail_copy(e).wait())
        lax.fori_loop(first_unused, n_blocks, lambda j, c: (unused_copy(j).wait(), c)[1], 0)

    def issue(i, carry):
        for k in range(TOP_K):
            d = dest_ref[k * nt + base + i]
            _row_copy(packed.at[pl.ds(i, 1), :], xs_ref.at[pl.ds(d, 1), :], sem).start()
        return carry

    lax.fori_loop(0, tm, issue, 0)
    for _ in range(TOP_K):
        _row_copy(packed, xs_ref.at[pl.ds(0, tm), :], sem).wait()


def _dispatch(dest_flat, tail, x, rows, tm=256):
    nt, d = x.shape
    zeros = jnp.zeros((MOE_TM, d // 2), jnp.uint32)
    return pl.pallas_call(
        _dispatch_kernel,
        grid_spec=pltpu.PrefetchScalarGridSpec(
            num_scalar_prefetch=2,
            grid=(nt // tm,),
            in_specs=[pl.BlockSpec((tm, d), lambda i, dest, tail: (i, 0)),
                      pl.BlockSpec(memory_space=pl.ANY)],
            out_specs=pl.BlockSpec(memory_space=pl.ANY),
            scratch_shapes=[pltpu.VMEM((tm, d // 2), jnp.uint32),
                            pltpu.SemaphoreType.DMA(()), pltpu.SemaphoreType.DMA(())],
        ),
        out_shape=jax.ShapeDtypeStruct((rows, d // 2), jnp.uint32),
        compiler_params=_cparams(("arbitrary",)),
        name="moe_dispatch",
    )(dest_flat, tail, x, zeros)


def _expert_kernel(exp_ref, first_ref, active_ref, next_ref, slot_ref, xsrc_ref,
                   xs_ref, bgu_ref, bdn_ref, wgu_hbm, wdn_hbm,
                   y_ref, wgu_f32, wdn_f32, wgu_bf, wdn_bf, sem, *, layer):
    del xsrc_ref
    i = pl.program_id(0)

    def weight_copies(e, s):
        return (pltpu.make_async_copy(wgu_hbm.at[layer, e], wgu_f32.at[s], sem.at[0, s]),
                pltpu.make_async_copy(wdn_hbm.at[layer, e], wdn_f32.at[s], sem.at[1, s]))

    @pl.when(i == 0)
    def _():
        for c in weight_copies(exp_ref[0], slot_ref[0]):
            c.start()

    @pl.when(first_ref[i] == 1)
    def _():
        s = slot_ref[i]
        for c in weight_copies(exp_ref[i], s):
            c.wait()

        @pl.when(next_ref[i] >= 0)
        def _():
            for c in weight_copies(next_ref[i], 1 - s):
                c.start()

        wgu_bf[...] = wgu_f32[s].astype(BF16)
        wdn_bf[...] = wdn_f32[s].astype(BF16)

    @pl.when(active_ref[i] == 1)
    def _():
        half = xs_ref.shape[1]
        x_hi, x_lo = _unpack_bf16_pairs(xs_ref[...])
        hb = (jnp.dot(x_hi, wgu_bf[:half, :], preferred_element_type=F32)
              + jnp.dot(x_lo, wgu_bf[half:, :], preferred_element_type=F32)) + bgu_ref[...]
        gate = jnp.minimum(hb[:, :D_FF], SWIGLU_LIMIT)
        up = jnp.clip(hb[:, D_FF:], -SWIGLU_LIMIT, SWIGLU_LIMIT)
        act = (up + 1.0) * gate * jax.nn.sigmoid(SWIGLU_ALPHA * gate)
        y_ref[...] = jnp.dot(act.astype(BF16), wdn_bf[...], preferred_element_type=F32) + bdn_ref[...]

    @pl.when(active_ref[i] == 0)
    def _():
        y_ref[...] = jnp.zeros_like(y_ref)


def _experts(blocks, xs, w_gu, b_gu, w_dn, b_dn, layer):
    rows = xs.shape[0]
    d = w_dn.shape[-1]
    n_blocks = rows // MOE_TM
    return pl.pallas_call(
        functools.partial(_expert_kernel, layer=layer),
        grid_spec=pltpu.PrefetchScalarGridSpec(
            num_scalar_prefetch=6,
            grid=(n_blocks,),
            in_specs=[
                pl.BlockSpec((MOE_TM, d // 2), lambda i, e, f, a, n, s, x: (x[i], 0)),
                pl.BlockSpec((None, None, 1, 2 * D_FF), lambda i, e, f, a, n, s, x: (layer, e[i], 0, 0)),
                pl.BlockSpec((None, None, 1, d), lambda i, e, f, a, n, s, x: (layer, e[i], 0, 0)),
                pl.BlockSpec(memory_space=pl.ANY),
                pl.BlockSpec(memory_space=pl.ANY),
            ],
            out_specs=pl.BlockSpec((MOE_TM, d), lambda i, e, f, a, n, s, x: (i, 0)),
            scratch_shapes=[pltpu.VMEM((2, d, 2 * D_FF), F32), pltpu.VMEM((2, D_FF, d), F32),
                            pltpu.VMEM((d, 2 * D_FF), BF16), pltpu.VMEM((D_FF, d), BF16),
                            pltpu.SemaphoreType.DMA((2, 2))],
        ),
        out_shape=jax.ShapeDtypeStruct((rows, d), F32),
        compiler_params=_cparams(("arbitrary",), VMEM_LIMIT),
        name="moe_experts",
    )(*blocks, xs, b_gu.reshape(DEPTH, N_EXPERTS, 1, 2 * D_FF), b_dn.reshape(DEPTH, N_EXPERTS, 1, d),
      w_gu, w_dn)


def _combine_kernel(dest_ref, x_ref, gate_ref, g_ref, b_ref, y_ref, out_ref, buf, sem):
    tm = x_ref.shape[0]
    nt = tm * pl.num_programs(0)
    base = pl.program_id(0) * tm

    def issue(i, carry):
        for k in range(TOP_K):
            d = dest_ref[k * nt + base + i]
            _row_copy(y_ref.at[pl.ds(d, 1), :], buf.at[k, pl.ds(i, 1), :], sem).start()
        return carry

    lax.fori_loop(0, tm, issue, 0)
    for k in range(TOP_K):
        _row_copy(y_ref.at[pl.ds(0, tm), :], buf.at[k], sem).wait()

    gates = gate_ref[...]
    ffn = gates[:, 0:1] * buf[0]
    for k in range(1, TOP_K):
        ffn += gates[:, k:k + 1] * buf[k]
    y = DEEPNORM_ALPHA * x_ref[...] + ffn
    out_ref[...] = _layer_norm_rows(y, g_ref[...], b_ref[...])


def _combine(dest_flat, x, gates, g, b, y_rows, tm=256):
    nt, d = x.shape
    return pl.pallas_call(
        _combine_kernel,
        grid_spec=pltpu.PrefetchScalarGridSpec(
            num_scalar_prefetch=1,
            grid=(nt // tm,),
            in_specs=[pl.BlockSpec((tm, d), lambda i, dest: (i, 0)),
                      pl.BlockSpec((tm, LANES), lambda i, dest: (i, 0)),
                      pl.BlockSpec((1, d), lambda i, dest: (0, 0)),
                      pl.BlockSpec((1, d), lambda i, dest: (0, 0)),
                      pl.BlockSpec(memory_space=pl.ANY)],
            out_specs=pl.BlockSpec((tm, d), lambda i, dest: (i, 0)),
            scratch_shapes=[pltpu.VMEM((TOP_K, tm, d), F32), pltpu.SemaphoreType.DMA(())],
        ),
        out_shape=jax.ShapeDtypeStruct((nt, d), F32),
        compiler_params=_cparams(("arbitrary",)),
        name="moe_combine",
    )(dest_flat, x, gates, g.reshape(1, d), b.reshape(1, d), y_rows)


def _moe_layer(x, layer, w_router, b_router, w_gate_up, b_gate_up, w_down, b_down, ln_g, ln_b):
    nt, _ = x.shape
    idx, rank, gates, counts = _router(x, w_router[layer], b_router[layer])
    counts = counts[:, 0]
    padded = (counts + MOE_TM - 1) // MOE_TM * MOE_TM
    pend = jnp.cumsum(padded)
    pstart = pend - padded
    experts = jnp.arange(N_EXPERTS, dtype=jnp.int32)
    start_of = jnp.sum(jnp.where(idx[..., None] == experts, pstart, 0), axis=-1)
    dest = (start_of + rank).reshape(TOP_K * nt).astype(jnp.int32)
    n_blocks = nt * TOP_K // MOE_TM + N_EXPERTS
    blk_start = jnp.arange(n_blocks, dtype=jnp.int32) * MOE_TM
    active = blk_start < pend[-1]
    exp_raw = jnp.minimum(jnp.sum(pend[None, :] <= blk_start[:, None], axis=1), N_EXPERTS - 1).astype(jnp.int32)
    last_exp = jnp.max(jnp.where(active, exp_raw, 0))
    block_exp = jnp.where(active, exp_raw, last_exp).astype(jnp.int32)
    prev_exp = jnp.concatenate([jnp.full((1,), -1, jnp.int32), block_exp[:-1]])
    block_first = (active & (block_exp != prev_exp)).astype(jnp.int32)
    has = padded > 0
    slot_e = (jnp.cumsum(has.astype(jnp.int32)) - 1) % 2
    later = lax.cummin(jnp.where(has, experts, N_EXPERTS)[::-1])[::-1]
    next_e = jnp.concatenate([later[1:], jnp.full((1,), N_EXPERTS, jnp.int32)])
    next_e = jnp.where(next_e < N_EXPERTS, next_e, -1)
    n_active = pend[-1] // MOE_TM
    tail = jnp.concatenate([jnp.where(has, pend - MOE_TM, -1), n_active[None]]).astype(jnp.int32)
    blocks = (block_exp, block_first, active.astype(jnp.int32), next_e[block_exp].astype(jnp.int32),
              slot_e[block_exp].astype(jnp.int32),
              jnp.minimum(jnp.arange(n_blocks, dtype=jnp.int32), n_active - 1).astype(jnp.int32))
    xs = _dispatch(dest, tail, x, n_blocks * MOE_TM)
    y_rows = _experts(blocks, xs, w_gate_up, b_gate_up, w_down, b_down, layer)
    return _combine(dest, x, gates, ln_g, ln_b, y_rows)


def _split3(v):
    hi = v.astype(BF16)
    r1 = v - hi.astype(F32)
    mid = r1.astype(BF16)
    lo = (r1 - mid.astype(F32)).astype(BF16)
    return hi, mid, lo


def _shared_proj_kernel(x_ref, w_ref, wfh_ref, wfl_ref, bf_ref, eq_ref, ek_ref, oq_ref, ok_ref,
                        kq_ref, v_ref, qq_ref, qm_ref, carry_ref):
    @pl.when(pl.program_id(1) == 0)
    def _():
        carry_ref[...] = jnp.zeros_like(carry_ref)

    tm = x_ref.shape[0]
    x = x_ref[...]
    xh = x.astype(BF16)
    xl = (x - xh.astype(F32)).astype(BF16)
    h = jnp.dot(xh, w_ref[...], preferred_element_type=F32)
    f = (jnp.dot(xh, wfh_ref[...], preferred_element_type=F32)
         + jnp.dot(xl, wfh_ref[...], preferred_element_type=F32)
         + jnp.dot(xh, wfl_ref[...], preferred_element_type=F32)) + bf_ref[...]
    ls = jnp.minimum(f, 0.0) - jnp.log1p(jnp.exp(-jnp.abs(f)))
    ls = jnp.where(_head_mask(LANES, 0, N_FOX_HEADS), ls, 0.0)
    row = lax.broadcasted_iota(jnp.int32, (tm, tm), 0)
    col = lax.broadcasted_iota(jnp.int32, (tm, tm), 1)
    tri = jnp.where(row >= col, 1.0, 0.0).astype(BF16)
    a, b, c = _split3(ls)
    cum = (jnp.dot(tri, a, preferred_element_type=F32) + jnp.dot(tri, b, preferred_element_type=F32)
           + jnp.dot(tri, c, preferred_element_type=F32)) + carry_ref[...]
    carry_ref[...] = cum[tm - 1:tm, :]
    ch, cm, cl = _split3(cum)
    cat = (ch.astype(F32) + pltpu.roll(cm.astype(F32), 16, 1) + pltpu.roll(cl.astype(F32), 32, 1)).astype(BF16)
    kq = h[:, :FOX_QK_W] + jnp.dot(cat, ek_ref[...], preferred_element_type=F32) + ok_ref[...]
    qq = (h[:, FOX_QK_W + FOX_W:2 * FOX_QK_W + FOX_W]
          + jnp.dot(cat, eq_ref[...], preferred_element_type=F32) + oq_ref[...])
    kq_ref[...] = kq.astype(BF16)
    v_ref[...] = h[:, FOX_QK_W:FOX_QK_W + FOX_W].astype(BF16)
    qq_ref[...] = qq.astype(BF16)
    qm_ref[...] = h[:, 2 * FOX_QK_W + FOX_W:].astype(BF16)


def _widen_heads(w):
    d = w.shape[0]
    w3 = w.reshape(d, N_FOX_HEADS, HEAD_DIM)
    return jnp.concatenate([w3, jnp.zeros_like(w3)], axis=-1).reshape(d, FOX_QK_W)


def _spread_matrices():
    src = jnp.arange(LANES)[:, None]
    dst = jnp.arange(FOX_QK_W)[None, :]
    head = dst // FOX_HEAD_W
    off = dst % FOX_HEAD_W
    part = src // 16
    is_src = (src % 16 == head) & (src % 16 < N_FOX_HEADS) & (part < 3)
    eq = jnp.where(is_src & (off == HEAD_DIM + part), 1.0, 0.0).astype(BF16)
    ek = jnp.where(is_src & (off == HEAD_DIM + 3 + part), -1.0, 0.0).astype(BF16)
    off1 = jnp.arange(FOX_QK_W) % FOX_HEAD_W
    ones_q = ((off1 >= HEAD_DIM + 3) & (off1 < HEAD_DIM + 6)).astype(F32).reshape(1, FOX_QK_W)
    ones_k = ((off1 >= HEAD_DIM) & (off1 < HEAD_DIM + 3)).astype(F32).reshape(1, FOX_QK_W)
    return eq, ek, ones_q, ones_k


def _shared_proj(x, batch, w_shared_kvf, b_forget, w_in_b, tm=256):
    nt, d = x.shape
    per_b = nt // batch // tm
    w_k = _widen_heads(w_shared_kvf[:, :FOX_W])
    w_v = w_shared_kvf[:, FOX_W:2 * FOX_W]
    w_q = _widen_heads(w_in_b[:, :FOX_W] * ATTN_SCALE)
    w_qm = w_in_b[:, FOX_W:] * ATTN_SCALE
    w_big = jnp.concatenate([w_k, w_v, w_q, w_qm], axis=1).astype(BF16)
    w_f = jnp.pad(w_shared_kvf[:, 2 * FOX_W:], ((0, 0), (0, LANES - N_FOX_HEADS)))
    w_fh = w_f.astype(BF16)
    w_fl = (w_f - w_fh.astype(F32)).astype(BF16)
    b_f = jnp.pad(b_forget, (0, LANES - N_FOX_HEADS)).reshape(1, LANES)
    eq, ek, ones_q, ones_k = _spread_matrices()
    nbig = w_big.shape[1]
    row = lambda bi, i: (bi * per_b + i, 0)
    const = lambda bi, i: (0, 0)
    return pl.pallas_call(
        _shared_proj_kernel,
        grid=(batch, per_b),
        in_specs=[pl.BlockSpec((tm, d), row),
                  pl.BlockSpec((d, nbig), const),
                  pl.BlockSpec((d, LANES), const), pl.BlockSpec((d, LANES), const),
                  pl.BlockSpec((1, LANES), const),
                  pl.BlockSpec((LANES, FOX_QK_W), const), pl.BlockSpec((LANES, FOX_QK_W), const),
                  pl.BlockSpec((1, FOX_QK_W), const), pl.BlockSpec((1, FOX_QK_W), const)],
        out_specs=[pl.BlockSpec((tm, FOX_QK_W), row), pl.BlockSpec((tm, FOX_W), row),
                   pl.BlockSpec((tm, FOX_QK_W), row), pl.BlockSpec((tm, MEM_W), row)],
        out_shape=[jax.ShapeDtypeStruct((nt, FOX_QK_W), BF16), jax.ShapeDtypeStruct((nt, FOX_W), BF16),
                   jax.ShapeDtypeStruct((nt, FOX_QK_W), BF16), jax.ShapeDtypeStruct((nt, MEM_W), BF16)],
        scratch_shapes=[pltpu.VMEM((1, LANES), F32)],
        compiler_params=_cparams(("arbitrary", "arbitrary"), VMEM_LIMIT),
        name="shared_proj",
    )(x, w_big, w_fh, w_fl, b_f, eq, ek, ones_q, ones_k)


def _fox_kernel(q_ref, k_ref, v_ref, o_ref):
    tq = q_ref.shape[0]
    i = pl.program_id(2)
    row = lax.broadcasted_iota(jnp.int32, (tq, tq), 0)
    col = lax.broadcasted_iota(jnp.int32, (tq, tq), 1)

    def attend(kv):
        out = jnp.zeros((tq, LANES), F32)
        for hh in range(2):
            qh = q_ref[:, hh * FOX_HEAD_W:(hh + 1) * FOX_HEAD_W]
            kh = k_ref[:kv, hh * FOX_HEAD_W:(hh + 1) * FOX_HEAD_W]
            sc = lax.dot_general(qh, kh, (((1,), (1,)), ((), ())), preferred_element_type=F32)
            diag = jnp.where(col <= row, sc[:, kv - tq:], -jnp.inf)
            sc = diag if kv == tq else jnp.concatenate([sc[:, :kv - tq], diag], axis=1)
            m = jnp.max(sc, axis=-1, keepdims=True)
            p = jnp.exp(sc - m)
            den = jnp.sum(p, axis=-1, keepdims=True)
            acc = jnp.dot(p.astype(BF16), v_ref[:kv, :], preferred_element_type=F32)
            out = jnp.where(_head_mask(LANES, hh * HEAD_DIM, (hh + 1) * HEAD_DIM), acc / den, out)
        o_ref[...] = out.astype(BF16)

    for c in range(k_ref.shape[0] // tq):
        pl.when(i == c)(functools.partial(attend, (c + 1) * tq))


def _fox_attn(qq, kq, v, batch, tq=256):
    nt = qq.shape[0]
    s = nt // batch
    per_b = s // tq
    pairs = N_FOX_HEADS // 2
    return pl.pallas_call(
        _fox_kernel,
        grid=(batch, pairs, per_b),
        in_specs=[pl.BlockSpec((tq, 2 * FOX_HEAD_W), lambda bi, p, i: (bi * per_b + i, p)),
                  pl.BlockSpec((s, 2 * FOX_HEAD_W), lambda bi, p, i: (bi, p)),
                  pl.BlockSpec((s, LANES), lambda bi, p, i: (bi, p))],
        out_specs=pl.BlockSpec((tq, LANES), lambda bi, p, i: (bi * per_b + i, p)),
        out_shape=jax.ShapeDtypeStruct((nt, FOX_W), BF16),
        compiler_params=_cparams(("parallel", "parallel", "parallel")),
        name="fox_attn",
    )(qq, kq, v)


def _rope_tables(seq):
    inv = 1.0 / (ROPE_THETA ** (jnp.arange(0, HEAD_DIM, 2, dtype=F32) / HEAD_DIM))
    ang = jnp.arange(seq, dtype=F32)[:, None] * inv[None, :]
    return jnp.tile(jnp.cos(ang), (1, DIL_HEADS)), jnp.tile(jnp.sin(ang), (1, DIL_HEADS))


def _rotary_layout(w):
    d = w.shape[0]
    w4 = w.reshape(d, DIL_HEADS, 2, HEAD_DIM // 2)
    return w4.transpose(0, 2, 1, 3).reshape(d, GROUP_W)


def _group_weights(w_in, g, with_mem):
    base = g * 3 * GROUP_W
    cols = [_rotary_layout(w_in[:, base:base + GROUP_W] * ATTN_SCALE),
            _rotary_layout(w_in[:, base + GROUP_W:base + 2 * GROUP_W]),
            w_in[:, base + 2 * GROUP_W:base + 3 * GROUP_W]]
    if with_mem:
        cols.append(w_in[:, DIL_QKV_W:] * ATTN_SCALE)
    return jnp.concatenate(cols, axis=1).astype(BF16)


def kernel(x, mem, w_in_a, w_out_a, w_in_b, w_out_b, w_shared_kvf, b_forget, w_mem_kv, ln_mix_g, ln_mix_b,
           ln_ffn_g, ln_ffn_b, w_router, b_router, w_gate_up, b_gate_up, w_down, b_down):
    b, s, d = x.shape
    nt = b * s
    mem2 = mem.reshape(b * mem.shape[1], d)
    cos_t, sin_t = _rope_tables(s)
    moe = functools.partial(_moe_layer, w_router=w_router, b_router=b_router, w_gate_up=w_gate_up,
                            b_gate_up=b_gate_up, w_down=w_down, b_down=b_down)

    outs, lses, q_mem = [], [], None
    for g, (_, dil) in enumerate(DILATED_PATTERNS):
        res = _group_proj(x, _group_weights(w_in_a[0], g, g == 0), cos_t, sin_t, dil)
        if g == 0:
            q_mem = res[3].reshape(nt, MEM_W)
        o, lse = _dilated_attn(res[0], res[1], res[2], dil)
        outs.append(o)
        lses.append(lse)
    mkv0 = _matmul(mem2, w_mem_kv[0].astype(BF16), BF16, 512)
    memo = _mem_attn(q_mem, mkv0, b)
    x2 = x.reshape(nt, d)
    x2 = _out_proj(_out_proj_a_kernel, outs + lses + [memo], x2, w_out_a[0].astype(BF16),
                   ln_mix_g[0], ln_mix_b[0], "out_proj_a")
    x2 = moe(x2, 0, ln_g=ln_ffn_g[0], ln_b=ln_ffn_b[0])

    kq, v_sh, qq, q_mem = _shared_proj(x2, b, w_shared_kvf, b_forget, w_in_b[0])

    fox = _fox_attn(qq, kq, v_sh, b)
    mkv1 = _matmul(mem2, w_mem_kv[1].astype(BF16), BF16, 512)
    memo = _mem_attn(q_mem, mkv1, b)
    x2 = _out_proj(_out_proj_b_kernel, [fox, memo], x2, w_out_b[0].astype(BF16),
                   ln_mix_g[1], ln_mix_b[1], "out_proj_b")
    x2 = moe(x2, 1, ln_g=ln_ffn_g[1], ln_b=ln_ffn_b[1])
    return x2.reshape(b, s, d)
```

```python
import functools

import jax
import jax.numpy as jnp
from jax import lax
from jax.experimental import pallas as pl
from jax.experimental.pallas import tpu as pltpu

F32 = jnp.float32
BF16 = jnp.bfloat16

D_MODEL = 1024
DEPTH = 2
HEAD_DIM = 64
N_MEM_HEADS = 4
DILATED_PATTERNS = ((128, 1), (512, 4), (2048, 16))
N_DIL_GROUPS = 3
DIL_HEADS = 4
N_FOX_HEADS = 12
GROUP_W = DIL_HEADS * HEAD_DIM
MEM_W = N_MEM_HEADS * HEAD_DIM
FOX_W = N_FOX_HEADS * HEAD_DIM
DIL_QKV_W = N_DIL_GROUPS * 3 * GROUP_W
N_EXPERTS = 32
TOP_K = 4
D_FF = D_MODEL
SWIGLU_LIMIT = 7.0
SWIGLU_ALPHA = 1.702
ROPE_THETA = 10000.0
LN_EPS = 1e-5
ATTN_SCALE = HEAD_DIM ** -0.5
DEEPNORM_ALPHA = (2.0 * DEPTH) ** 0.25
DIL_STEPS = 128

LANES = 128
FOX_HEAD_W = LANES
FOX_QK_W = N_FOX_HEADS * FOX_HEAD_W
MOE_TM = 256
VMEM_LIMIT = 56 * 1024 * 1024


def _cparams(sem, vmem=None):
    return pltpu.CompilerParams(dimension_semantics=sem, vmem_limit_bytes=vmem)


def _head_mask(width, lo, hi):
    lane = lax.broadcasted_iota(jnp.int32, (1, width), 1)
    return (lane >= lo) & (lane < hi)


def _layer_norm_rows(y, g, b):
    mu = jnp.mean(y, axis=-1, keepdims=True)
    yc = y - mu
    var = jnp.mean(yc * yc, axis=-1, keepdims=True)
    return yc * lax.rsqrt(var + LN_EPS) * g + b


def _matmul_kernel(x_ref, w_ref, o_ref):
    o_ref[...] = jnp.dot(x_ref[...].astype(BF16), w_ref[...],
                         preferred_element_type=F32).astype(o_ref.dtype)


def _matmul(x, w, out_dtype, tm):
    m, k = x.shape
    n = w.shape[1]
    return pl.pallas_call(
        _matmul_kernel,
        grid=(m // tm,),
        in_specs=[pl.BlockSpec((tm, k), lambda i: (i, 0)),
                  pl.BlockSpec((k, n), lambda i: (0, 0))],
        out_specs=pl.BlockSpec((tm, n), lambda i: (i, 0)),
        out_shape=jax.ShapeDtypeStruct((m, n), out_dtype),
        compiler_params=_cparams(("parallel",)),
        name="matmul",
    )(x, w)


def _group_proj_kernel(x_ref, w_ref, cos_ref, sin_ref, *o_refs):
    h = jnp.dot(x_ref[0].astype(BF16), w_ref[...], preferred_element_type=F32)
    cos = cos_ref[...]
    sin = sin_ref[...]
    half = GROUP_W // 2
    for j in range(2):
        t1 = h[:, j * GROUP_W:j * GROUP_W + half]
        t2 = h[:, j * GROUP_W + half:(j + 1) * GROUP_W]
        o_refs[j][0, 0, :, :half] = (t1 * cos - t2 * sin).astype(BF16)
        o_refs[j][0, 0, :, half:] = (t2 * cos + t1 * sin).astype(BF16)
    for j in range(2, len(o_refs)):
        o_refs[j][0, 0] = h[:, j * GROUP_W:(j + 1) * GROUP_W].astype(BF16)


def _group_proj(x, w, cos_t, sin_t, dil):
    b, s, d = x.shape
    length = s // dil
    lt = min(length, 512)
    n_out = w.shape[1] // GROUP_W
    xv = x.reshape(b, length, dil * d)
    cv = cos_t.reshape(length, dil * LANES)
    sv = sin_t.reshape(length, dil * LANES)
    o_spec = pl.BlockSpec((1, 1, lt, GROUP_W), lambda bi, r, l: (bi, r, l, 0))
    return pl.pallas_call(
        _group_proj_kernel,
        grid=(b, dil, length // lt),
        in_specs=[pl.BlockSpec((1, lt, d), lambda bi, r, l: (bi, l, r)),
                  pl.BlockSpec(w.shape, lambda bi, r, l: (0, 0)),
                  pl.BlockSpec((lt, LANES), lambda bi, r, l: (l, r)),
                  pl.BlockSpec((lt, LANES), lambda bi, r, l: (l, r))],
        out_specs=[o_spec] * n_out,
        out_shape=[jax.ShapeDtypeStruct((b, dil, length, GROUP_W), BF16)] * n_out,
        compiler_params=_cparams(("parallel", "parallel", "parallel")),
        name=f"group_proj_d{dil}",
    )(xv, w, cv, sv)


def _dilated_attn_kernel(q_ref, kp_ref, kc_ref, vp_ref, vc_ref, o_ref, lse_ref):
    n = pl.program_id(2)
    n_res = q_ref.shape[1]
    n_sub = q_ref.shape[2] // DIL_STEPS
    qi = lax.broadcasted_iota(jnp.int32, (DIL_STEPS, 2 * DIL_STEPS), 0)
    kj = lax.broadcasted_iota(jnp.int32, (DIL_STEPS, 2 * DIL_STEPS), 1)
    band = (kj >= qi) & (kj <= qi + DIL_STEPS)
    half = GROUP_W // 2
    hw = HEAD_DIM // 2
    for r in range(n_res):
        for j in range(n_sub):
            lo, hi = j * DIL_STEPS, (j + 1) * DIL_STEPS
            q = q_ref[0, r, lo:hi, :]
            if j == 0:
                kk = jnp.concatenate([kp_ref[0, r], kc_ref[0, r, lo:hi, :]], axis=0)
                vv = jnp.concatenate([vp_ref[0, r], vc_ref[0, r, lo:hi, :]], axis=0)
                valid = band & ((n > 0) | (kj >= DIL_STEPS))
            else:
                kk = kc_ref[0, r, lo - DIL_STEPS:hi, :]
                vv = vc_ref[0, r, lo - DIL_STEPS:hi, :]
                valid = band
            out = jnp.zeros((DIL_STEPS, GROUP_W), F32)
            lse = jnp.zeros((DIL_STEPS, GROUP_W), F32)
            for h in range(DIL_HEADS):
                qmask = (_head_mask(GROUP_W, h * hw, (h + 1) * hw)
                         | _head_mask(GROUP_W, half + h * hw, half + (h + 1) * hw))
                qm = jnp.where(qmask, q, jnp.zeros_like(q))
                sc = lax.dot_general(qm, kk, (((1,), (1,)), ((), ())), preferred_element_type=F32)
                sc = jnp.where(valid, sc, -jnp.inf)
                m = jnp.max(sc, axis=-1, keepdims=True)
                p = jnp.exp(sc - m)
                den = jnp.sum(p, axis=-1, keepdims=True)
                o_all = jnp.dot(p.astype(BF16), vv, preferred_element_type=F32)
                vmask = _head_mask(GROUP_W, h * HEAD_DIM, (h + 1) * HEAD_DIM)
                out = jnp.where(vmask, o_all / den, out)
                lse = jnp.where(vmask, m + jnp.log(den), lse)
            o_ref[0, lo:hi, r * GROUP_W:(r + 1) * GROUP_W] = out.astype(BF16)
            lse_ref[0, lo:hi, r * GROUP_W:(r + 1) * GROUP_W] = lse


DIL_UNITS = 4


def _dilated_attn(q, k, v, dil):
    b, _, length, _ = q.shape
    n_sub = min(length // DIL_STEPS, DIL_UNITS)
    n_res = DIL_UNITS // n_sub
    rows = n_sub * DIL_STEPS
    cur = pl.BlockSpec((1, n_res, rows, GROUP_W), lambda bi, r, n: (bi, r, n, 0))
    prev = pl.BlockSpec((1, n_res, DIL_STEPS, GROUP_W),
                        lambda bi, r, n: (bi, r, jnp.maximum(n * n_sub - 1, 0), 0))
    out = pl.BlockSpec((1, rows, n_res * GROUP_W), lambda bi, r, n: (bi, n, r))
    o, lse = pl.pallas_call(
        _dilated_attn_kernel,
        grid=(b, dil // n_res, length // rows),
        in_specs=[cur, prev, cur, prev, cur],
        out_specs=[out, out],
        out_shape=[jax.ShapeDtypeStruct((b, length, dil * GROUP_W), BF16),
                   jax.ShapeDtypeStruct((b, length, dil * GROUP_W), F32)],
        compiler_params=_cparams(("parallel", "parallel", "parallel")),
        name=f"dilated_attn_d{dil}",
    )(q, k, k, v, v)
    return o.reshape(b * length * dil, GROUP_W), lse.reshape(b * length * dil, GROUP_W)


def _mem_attn_kernel(q_ref, kv_ref, o_ref):
    q = q_ref[...]
    mk = kv_ref[:, :MEM_W]
    mv = kv_ref[:, MEM_W:]
    out = jnp.zeros(q.shape, F32)
    for h in range(N_MEM_HEADS):
        hmask = _head_mask(MEM_W, h * HEAD_DIM, (h + 1) * HEAD_DIM)
        qm = jnp.where(hmask, q, jnp.zeros_like(q))
        sc = lax.dot_general(qm, mk, (((1,), (1,)), ((), ())), preferred_element_type=F32)
        m = jnp.max(sc, axis=-1, keepdims=True)
        p = jnp.exp(sc - m)
        den = jnp.sum(p, axis=-1, keepdims=True)
        o_all = jnp.dot(p.astype(BF16), mv, preferred_element_type=F32)
        out = jnp.where(hmask, o_all / den, out)
    o_ref[...] = out.astype(BF16)


def _mem_attn(q, mkv, batch, tq=512):
    nt = q.shape[0]
    per_b = nt // batch // tq
    m_len = mkv.shape[0] // batch
    return pl.pallas_call(
        _mem_attn_kernel,
        grid=(batch, per_b),
        in_specs=[pl.BlockSpec((tq, MEM_W), lambda bi, i: (bi * per_b + i, 0)),
                  pl.BlockSpec((m_len, 2 * MEM_W), lambda bi, i: (bi, 0))],
        out_specs=pl.BlockSpec((tq, MEM_W), lambda bi, i: (bi * per_b + i, 0)),
        out_shape=jax.ShapeDtypeStruct((nt, MEM_W), BF16),
        compiler_params=_cparams(("parallel", "parallel")),
        name="mem_attn",
    )(q, mkv)


def _out_proj_a_kernel(o0, o1, o2, l0, l1, l2, memo, x_ref, w_ref, g_ref, b_ref, out_ref):
    la, lb, lc = l0[...], l1[...], l2[...]
    mx = jnp.maximum(jnp.maximum(la, lb), lc)
    ea, eb, ec = jnp.exp(la - mx), jnp.exp(lb - mx), jnp.exp(lc - mx)
    z = ea + eb + ec
    dil = (ea * o0[...].astype(F32) + eb * o1[...].astype(F32) + ec * o2[...].astype(F32)) / z
    mix = jnp.dot(dil.astype(BF16), w_ref[:GROUP_W, :], preferred_element_type=F32)
    mix += jnp.dot(memo[...], w_ref[GROUP_W:, :], preferred_element_type=F32)
    y = DEEPNORM_ALPHA * x_ref[...] + mix
    out_ref[...] = _layer_norm_rows(y, g_ref[...], b_ref[...])


def _out_proj_b_kernel(fox, memo, x_ref, w_ref, g_ref, b_ref, out_ref):
    mix = jnp.dot(fox[...], w_ref[:FOX_W, :], preferred_element_type=F32)
    mix += jnp.dot(memo[...], w_ref[FOX_W:, :], preferred_element_type=F32)
    y = DEEPNORM_ALPHA * x_ref[...] + mix
    out_ref[...] = _layer_norm_rows(y, g_ref[...], b_ref[...])


def _out_proj(kernel_fn, acts, x, w, g, b, name, tm=512):
    nt, d = x.shape
    row = lambda i: (i, 0)
    const = lambda i: (0, 0)
    return pl.pallas_call(
        kernel_fn,
        grid=(nt // tm,),
        in_specs=[pl.BlockSpec((tm, a.shape[1]), row) for a in acts]
        + [pl.BlockSpec((tm, d), row), pl.BlockSpec(w.shape, const),
           pl.BlockSpec((1, d), const), pl.BlockSpec((1, d), const)],
        out_specs=pl.BlockSpec((tm, d), row),
        out_shape=jax.ShapeDtypeStruct((nt, d), F32),
        compiler_params=_cparams(("parallel",)),
        name=name,
    )(*acts, x, w, g.reshape(1, d), b.reshape(1, d))


def _router_kernel(x_ref, wt_ref, b_ref, idx_ref, rank_ref, gate_ref, cnt_ref, carry_ref):
    @pl.when(pl.program_id(0) == 0)
    def _():
        carry_ref[...] = jnp.zeros_like(carry_ref)

    tm = x_ref.shape[0]
    logits = lax.dot_general(wt_ref[...], x_ref[...], (((1,), (1,)), ((), ())),
                             precision=lax.Precision.HIGHEST, preferred_element_type=F32) + b_ref[...]
    sub = lax.broadcasted_iota(jnp.int32, (N_EXPERTS, tm), 0).astype(F32)
    work = logits
    vals, sels, idxs = [], [], []
    for _ in range(TOP_K):
        mk = jnp.max(work, axis=0, keepdims=True)
        ik = jnp.min(jnp.where(work == mk, sub, float(N_EXPERTS)), axis=0, keepdims=True)
        sel = sub == ik
        work = jnp.where(sel, -jnp.inf, work)
        vals.append(mk)
        sels.append(sel)
        idxs.append(ik)
    es = [jnp.exp(v - vals[0]) for v in vals]
    z = es[0] + es[1] + es[2] + es[3]
    hot = jnp.zeros((N_EXPERTS, tm), F32)
    for sel in sels:
        hot = jnp.where(sel, 1.0, hot)
    row = lax.broadcasted_iota(jnp.int32, (tm, tm), 0)
    col = lax.broadcasted_iota(jnp.int32, (tm, tm), 1)
    tri = jnp.where(row < col, 1.0, 0.0).astype(BF16)
    before = jnp.dot(hot.astype(BF16), tri, preferred_element_type=F32) + carry_ref[...]
    k_sub = lax.broadcasted_iota(jnp.int32, (TOP_K, tm), 0)
    g_sub = lax.broadcasted_iota(jnp.int32, (LANES, tm), 0)
    idx_o = jnp.zeros((TOP_K, tm), F32)
    rank_o = jnp.zeros((TOP_K, tm), F32)
    gate_t = jnp.zeros((LANES, tm), F32)
    for k in range(TOP_K):
        rk = jnp.sum(jnp.where(sels[k], before, 0.0), axis=0, keepdims=True)
        idx_o = jnp.where(k_sub == k, idxs[k], idx_o)
        rank_o = jnp.where(k_sub == k, rk, rank_o)
        gate_t = jnp.where(g_sub == k, es[k] / z, gate_t)
    idx_ref[...] = idx_o.astype(jnp.int32)
    rank_ref[...] = rank_o.astype(jnp.int32)
    gate_ref[...] = gate_t.T
    carry_ref[...] += jnp.sum(hot, axis=1, keepdims=True)
    cnt_ref[...] = carry_ref[...].astype(jnp.int32)


def _router(x, w_r, b_r, tm=512):
    nt, d = x.shape
    const = lambda i: (0, 0)
    return pl.pallas_call(
        _router_kernel,
        grid=(nt // tm,),
        in_specs=[pl.BlockSpec((tm, d), lambda i: (i, 0)), pl.BlockSpec((N_EXPERTS, d), const),
                  pl.BlockSpec((N_EXPERTS, 1), const)],
        out_specs=[pl.BlockSpec((TOP_K, tm), lambda i: (0, i)), pl.BlockSpec((TOP_K, tm), lambda i: (0, i)),
                   pl.BlockSpec((tm, LANES), lambda i: (i, 0)), pl.BlockSpec((N_EXPERTS, 1), const)],
        out_shape=[jax.ShapeDtypeStruct((TOP_K, nt), jnp.int32),
                   jax.ShapeDtypeStruct((TOP_K, nt), jnp.int32),
                   jax.ShapeDtypeStruct((nt, LANES), F32),
                   jax.ShapeDtypeStruct((N_EXPERTS, 1), jnp.int32)],
        scratch_shapes=[pltpu.VMEM((N_EXPERTS, 1), F32)],
        compiler_params=_cparams(("arbitrary",)),
        name="router",
    )(x, w_r.T, b_r.reshape(N_EXPERTS, 1))


def _row_copy(src, dst, sem):
    return pltpu.make_async_copy(src, dst, sem)


ROW_ISSUE_UNROLL = 2


def _dispatch_kernel(dest_ref, tail_ref, x_ref, zeros_ref, xs_ref, sem, zsem):
    tm = x_ref.shape[0]
    nt = tm * pl.num_programs(0)
    base = pl.program_id(0) * tm

    def tail_copy(e):
        t = pl.multiple_of(jnp.maximum(tail_ref[e], 0), MOE_TM)
        return _row_copy(zeros_ref, xs_ref.at[pl.ds(t, MOE_TM), :], zsem)

    def unused_copy(j):
        return _row_copy(zeros_ref, xs_ref.at[pl.ds(pl.multiple_of(j * MOE_TM, MOE_TM), MOE_TM), :], zsem)

    @pl.when(pl.program_id(0) == 0)
    def _():
        n_blocks = xs_ref.shape[0] // MOE_TM
        first_unused = tail_ref[N_EXPERTS]
        for e in range(N_EXPERTS):
            pl.when(tail_ref[e] >= 0)(lambda e=e: tail_copy(e).start())
        lax.fori_loop(first_unused, n_blocks, lambda j, c: (unused_copy(j).start(), c)[1], 0)
        for e in range(N_EXPERTS):
            pl.when(tail_ref[e] >= 0)(lambda e=e: tail_copy(e).wait())
        lax.fori_loop(first_unused, n_blocks, lambda j, c: (unused_copy(j).wait(), c)[1], 0)

    def issue(i, carry):
        for k in range(TOP_K):
            d = dest_ref[k * nt + base + i]
            _row_copy(x_ref.at[pl.ds(i, 1), :], xs_ref.at[pl.ds(d, 1), :], sem).start()
        return carry

    lax.fori_loop(0, tm, issue, 0, unroll=ROW_ISSUE_UNROLL)
    for _ in range(TOP_K):
        _row_copy(x_ref, xs_ref.at[pl.ds(0, tm), :], sem).wait()


def _dispatch(dest_flat, tail, x, rows, tm=256):
    nt, d = x.shape
    zeros = jnp.zeros((MOE_TM, d), x.dtype)
    return pl.pallas_call(
        _dispatch_kernel,
        grid_spec=pltpu.PrefetchScalarGridSpec(
            num_scalar_prefetch=2,
            grid=(nt // tm,),
            in_specs=[pl.BlockSpec((tm, d), lambda i, dest, tail: (i, 0)),
                      pl.BlockSpec(memory_space=pl.ANY)],
            out_specs=pl.BlockSpec(memory_space=pl.ANY),
            scratch_shapes=[pltpu.SemaphoreType.DMA(()), pltpu.SemaphoreType.DMA(())],
        ),
        out_shape=jax.ShapeDtypeStruct((rows, d), x.dtype),
        compiler_params=_cparams(("arbitrary",)),
        name="moe_dispatch",
    )(dest_flat, tail, x, zeros)


def _expert_kernel(exp_ref, first_ref, active_ref, next_ref, slot_ref, xsrc_ref,
                   xs_ref, bgu_ref, bdn_ref, wgu_hbm, wdn_hbm,
                   y_ref, wgu_f32, wdn_f32, wgu_bf, wdn_bf, sem, *, layer):
    del xsrc_ref
    i = pl.program_id(0)

    def weight_copies(e, s):
        return (pltpu.make_async_copy(wgu_hbm.at[layer, e], wgu_f32.at[s], sem.at[0, s]),
                pltpu.make_async_copy(wdn_hbm.at[layer, e], wdn_f32.at[s], sem.at[1, s]))

    @pl.when(i == 0)
    def _():
        for c in weight_copies(exp_ref[0], slot_ref[0]):
            c.start()

    @pl.when(first_ref[i] == 1)
    def _():
        s = slot_ref[i]
        for c in weight_copies(exp_ref[i], s):
            c.wait()

        @pl.when(next_ref[i] >= 0)
        def _():
            for c in weight_copies(next_ref[i], 1 - s):
                c.start()

        wgu_bf[...] = wgu_f32[s].astype(BF16)
        wdn_bf[...] = wdn_f32[s].astype(BF16)

    @pl.when(active_ref[i] == 1)
    def _():
        hb = jnp.dot(xs_ref[...].astype(BF16), wgu_bf[...], preferred_element_type=F32) + bgu_ref[...]
        gate = jnp.minimum(hb[:, :D_FF], SWIGLU_LIMIT)
        up = jnp.clip(hb[:, D_FF:], -SWIGLU_LIMIT, SWIGLU_LIMIT)
        act = (up + 1.0) * gate * jax.nn.sigmoid(SWIGLU_ALPHA * gate)
        y_ref[...] = jnp.dot(act.astype(BF16), wdn_bf[...], preferred_element_type=F32) + bdn_ref[...]

    @pl.when(active_ref[i] == 0)
    def _():
        y_ref[...] = jnp.zeros_like(y_ref)


def _experts(blocks, xs, w_gu, b_gu, w_dn, b_dn, layer):
    rows = xs.shape[0]
    d = w_dn.shape[-1]
    n_blocks = rows // MOE_TM
    return pl.pallas_call(
        functools.partial(_expert_kernel, layer=layer),
        grid_spec=pltpu.PrefetchScalarGridSpec(
            num_scalar_prefetch=6,
            grid=(n_blocks,),
            in_specs=[
                pl.BlockSpec((MOE_TM, d), lambda i, e, f, a, n, s, x: (x[i], 0)),
                pl.BlockSpec((None, None, 1, 2 * D_FF), lambda i, e, f, a, n, s, x: (layer, e[i], 0, 0)),
                pl.BlockSpec((None, None, 1, d), lambda i, e, f, a, n, s, x: (layer, e[i], 0, 0)),
                pl.BlockSpec(memory_space=pl.ANY),
                pl.BlockSpec(memory_space=pl.ANY),
            ],
            out_specs=pl.BlockSpec((MOE_TM, d), lambda i, e, f, a, n, s, x: (i, 0)),
            scratch_shapes=[pltpu.VMEM((2, d, 2 * D_FF), F32), pltpu.VMEM((2, D_FF, d), F32),
                            pltpu.VMEM((d, 2 * D_FF), BF16), pltpu.VMEM((D_FF, d), BF16),
                            pltpu.SemaphoreType.DMA((2, 2))],
        ),
        out_shape=jax.ShapeDtypeStruct((rows, d), F32),
        compiler_params=_cparams(("arbitrary",), VMEM_LIMIT),
        name="moe_experts",
    )(*blocks, xs, b_gu.reshape(DEPTH, N_EXPERTS, 1, 2 * D_FF), b_dn.reshape(DEPTH, N_EXPERTS, 1, d),
      w_gu, w_dn)


def _combine_kernel(dest_ref, x_ref, gate_ref, g_ref, b_ref, y_ref, out_ref, buf, sem):
    tm = x_ref.shape[0]
    step = pl.program_id(0)
    n_steps = pl.num_programs(0)
    nt = tm * n_steps
    slot = step % 2

    def gather_rows(s, to_slot):
        def issue(i, carry):
            for k in range(TOP_K):
                d = dest_ref[k * nt + s * tm + i]
                _row_copy(y_ref.at[pl.ds(d, 1), :], buf.at[to_slot, k, pl.ds(i, 1), :], sem.at[to_slot]).start()
            return carry

        lax.fori_loop(0, tm, issue, 0, unroll=ROW_ISSUE_UNROLL)

    pl.when(step == 0)(lambda: gather_rows(0, 0))
    pl.when(step + 1 < n_steps)(lambda: gather_rows(step + 1, 1 - slot))
    for k in range(TOP_K):
        _row_copy(y_ref.at[pl.ds(0, tm), :], buf.at[slot, k], sem.at[slot]).wait()

    gates = gate_ref[...]
    ffn = gates[:, 0:1] * buf[slot, 0]
    for k in range(1, TOP_K):
        ffn += gates[:, k:k + 1] * buf[slot, k]
    y = DEEPNORM_ALPHA * x_ref[...] + ffn
    out_ref[...] = _layer_norm_rows(y, g_ref[...], b_ref[...])


def _combine(dest_flat, x, gates, g, b, y_rows, tm=256):
    nt, d = x.shape
    return pl.pallas_call(
        _combine_kernel,
        grid_spec=pltpu.PrefetchScalarGridSpec(
            num_scalar_prefetch=1,
            grid=(nt // tm,),
            in_specs=[pl.BlockSpec((tm, d), lambda i, dest: (i, 0)),
                      pl.BlockSpec((tm, LANES), lambda i, dest: (i, 0)),
                      pl.BlockSpec((1, d), lambda i, dest: (0, 0)),
                      pl.BlockSpec((1, d), lambda i, dest: (0, 0)),
                      pl.BlockSpec(memory_space=pl.ANY)],
            out_specs=pl.BlockSpec((tm, d), lambda i, dest: (i, 0)),
            scratch_shapes=[pltpu.VMEM((2, TOP_K, tm, d), F32), pltpu.SemaphoreType.DMA((2,))],
        ),
        out_shape=jax.ShapeDtypeStruct((nt, d), F32),
        compiler_params=_cparams(("arbitrary",), VMEM_LIMIT),
        name="moe_combine",
    )(dest_flat, x, gates, g.reshape(1, d), b.reshape(1, d), y_rows)


def _moe_layer(x, layer, w_router, b_router, w_gate_up, b_gate_up, w_down, b_down, ln_g, ln_b):
    nt, _ = x.shape
    idx, rank, gates, counts = _router(x, w_router[layer], b_router[layer])
    counts = counts[:, 0]
    padded = (counts + MOE_TM - 1) // MOE_TM * MOE_TM
    pend = jnp.cumsum(padded)
    pstart = pend - padded
    experts = jnp.arange(N_EXPERTS, dtype=jnp.int32)
    start_of = jnp.sum(jnp.where(idx[..., None] == experts, pstart, 0), axis=-1)
    dest = (start_of + rank).reshape(TOP_K * nt).astype(jnp.int32)
    n_blocks = nt * TOP_K // MOE_TM + N_EXPERTS
    blk_start = jnp.arange(n_blocks, dtype=jnp.int32) * MOE_TM
    active = blk_start < pend[-1]
    exp_raw = jnp.minimum(jnp.sum(pend[None, :] <= blk_start[:, None], axis=1), N_EXPERTS - 1).astype(jnp.int32)
    last_exp = jnp.max(jnp.where(active, exp_raw, 0))
    block_exp = jnp.where(active, exp_raw, last_exp).astype(jnp.int32)
    prev_exp = jnp.concatenate([jnp.full((1,), -1, jnp.int32), block_exp[:-1]])
    block_first = (active & (block_exp != prev_exp)).astype(jnp.int32)
    has = padded > 0
    slot_e = (jnp.cumsum(has.astype(jnp.int32)) - 1) % 2
    later = lax.cummin(jnp.where(has, experts, N_EXPERTS)[::-1])[::-1]
    next_e = jnp.concatenate([later[1:], jnp.full((1,), N_EXPERTS, jnp.int32)])
    next_e = jnp.where(next_e < N_EXPERTS, next_e, -1)
    n_active = pend[-1] // MOE_TM
    tail = jnp.concatenate([jnp.where(has, pend - MOE_TM, -1), n_active[None]]).astype(jnp.int32)
    blocks = (block_exp, block_first, active.astype(jnp.int32), next_e[block_exp].astype(jnp.int32),
              slot_e[block_exp].astype(jnp.int32),
              jnp.minimum(jnp.arange(n_blocks, dtype=jnp.int32), n_active - 1).astype(jnp.int32))
    xs = _dispatch(dest, tail, x, n_blocks * MOE_TM)
    y_rows = _experts(blocks, xs, w_gate_up, b_gate_up, w_down, b_down, layer)
    return _combine(dest, x, gates, ln_g, ln_b, y_rows)


def _split3(v):
    hi = v.astype(BF16)
    r1 = v - hi.astype(F32)
    mid = r1.astype(BF16)
    lo = (r1 - mid.astype(F32)).astype(BF16)
    return hi, mid, lo


def _shared_proj_kernel(x_ref, w_ref, wfh_ref, wfl_ref, bf_ref, eq_ref, ek_ref, oq_ref, ok_ref,
                        kq_ref, v_ref, qq_ref, qm_ref, carry_ref):
    @pl.when(pl.program_id(1) == 0)
    def _():
        carry_ref[...] = jnp.zeros_like(carry_ref)

    tm = x_ref.shape[0]
    x = x_ref[...]
    xh = x.astype(BF16)
    xl = (x - xh.astype(F32)).astype(BF16)
    h = jnp.dot(xh, w_ref[...], preferred_element_type=F32)
    f = (jnp.dot(xh, wfh_ref[...], preferred_element_type=F32)
         + jnp.dot(xl, wfh_ref[...], preferred_element_type=F32)
         + jnp.dot(xh, wfl_ref[...], preferred_element_type=F32)) + bf_ref[...]
    ls = jnp.minimum(f, 0.0) - jnp.log1p(jnp.exp(-jnp.abs(f)))
    ls = jnp.where(_head_mask(LANES, 0, N_FOX_HEADS), ls, 0.0)
    row = lax.broadcasted_iota(jnp.int32, (tm, tm), 0)
    col = lax.broadcasted_iota(jnp.int32, (tm, tm), 1)
    tri = jnp.where(row >= col, 1.0, 0.0).astype(BF16)
    a, b, c = _split3(ls)
    cum = (jnp.dot(tri, a, preferred_element_type=F32) + jnp.dot(tri, b, preferred_element_type=F32)
           + jnp.dot(tri, c, preferred_element_type=F32)) + carry_ref[...]
    carry_ref[...] = cum[tm - 1:tm, :]
    ch, cm, cl = _split3(cum)
    cat = (ch.astype(F32) + pltpu.roll(cm.astype(F32), 16, 1) + pltpu.roll(cl.astype(F32), 32, 1)).astype(BF16)
    kq = h[:, :FOX_QK_W] + jnp.dot(cat, ek_ref[...], preferred_element_type=F32) + ok_ref[...]
    qq = (h[:, FOX_QK_W + FOX_W:2 * FOX_QK_W + FOX_W]
          + jnp.dot(cat, eq_ref[...], preferred_element_type=F32) + oq_ref[...])
    kq_ref[...] = kq.astype(BF16)
    v_ref[...] = h[:, FOX_QK_W:FOX_QK_W + FOX_W].astype(BF16)
    qq_ref[...] = qq.astype(BF16)
    qm_ref[...] = h[:, 2 * FOX_QK_W + FOX_W:].astype(BF16)


def _widen_heads(w):
    d = w.shape[0]
    w3 = w.reshape(d, N_FOX_HEADS, HEAD_DIM)
    return jnp.concatenate([w3, jnp.zeros_like(w3)], axis=-1).reshape(d, FOX_QK_W)


def _spread_matrices():
    src = jnp.arange(LANES)[:, None]
    dst = jnp.arange(FOX_QK_W)[None, :]
    head = dst // FOX_HEAD_W
    off = dst % FOX_HEAD_W
    part = src // 16
    is_src = (src % 16 == head) & (src % 16 < N_FOX_HEADS) & (part < 3)
    eq = jnp.where(is_src & (off == HEAD_DIM + part), 1.0, 0.0).astype(BF16)
    ek = jnp.where(is_src & (off == HEAD_DIM + 3 + part), -1.0, 0.0).astype(BF16)
    off1 = jnp.arange(FOX_QK_W) % FOX_HEAD_W
    ones_q = ((off1 >= HEAD_DIM + 3) & (off1 < HEAD_DIM + 6)).astype(F32).reshape(1, FOX_QK_W)
    ones_k = ((off1 >= HEAD_DIM) & (off1 < HEAD_DIM + 3)).astype(F32).reshape(1, FOX_QK_W)
    return eq, ek, ones_q, ones_k


def _shared_proj(x, batch, w_shared_kvf, b_forget, w_in_b, tm=256):
    nt, d = x.shape
    per_b = nt // batch // tm
    w_k = _widen_heads(w_shared_kvf[:, :FOX_W])
    w_v = w_shared_kvf[:, FOX_W:2 * FOX_W]
    w_q = _widen_heads(w_in_b[:, :FOX_W] * ATTN_SCALE)
    w_qm = w_in_b[:, FOX_W:] * ATTN_SCALE
    w_big = jnp.concatenate([w_k, w_v, w_q, w_qm], axis=1).astype(BF16)
    w_f = jnp.pad(w_shared_kvf[:, 2 * FOX_W:], ((0, 0), (0, LANES - N_FOX_HEADS)))
    w_fh = w_f.astype(BF16)
    w_fl = (w_f - w_fh.astype(F32)).astype(BF16)
    b_f = jnp.pad(b_forget, (0, LANES - N_FOX_HEADS)).reshape(1, LANES)
    eq, ek, ones_q, ones_k = _spread_matrices()
    nbig = w_big.shape[1]
    row = lambda bi, i: (bi * per_b + i, 0)
    const = lambda bi, i: (0, 0)
    return pl.pallas_call(
        _shared_proj_kernel,
        grid=(batch, per_b),
        in_specs=[pl.BlockSpec((tm, d), row),
                  pl.BlockSpec((d, nbig), const),
                  pl.BlockSpec((d, LANES), const), pl.BlockSpec((d, LANES), const),
                  pl.BlockSpec((1, LANES), const),
                  pl.BlockSpec((LANES, FOX_QK_W), const), pl.BlockSpec((LANES, FOX_QK_W), const),
                  pl.BlockSpec((1, FOX_QK_W), const), pl.BlockSpec((1, FOX_QK_W), const)],
        out_specs=[pl.BlockSpec((tm, FOX_QK_W), row), pl.BlockSpec((tm, FOX_W), row),
                   pl.BlockSpec((tm, FOX_QK_W), row), pl.BlockSpec((tm, MEM_W), row)],
        out_shape=[jax.ShapeDtypeStruct((nt, FOX_QK_W), BF16), jax.ShapeDtypeStruct((nt, FOX_W), BF16),
                   jax.ShapeDtypeStruct((nt, FOX_QK_W), BF16), jax.ShapeDtypeStruct((nt, MEM_W), BF16)],
        scratch_shapes=[pltpu.VMEM((1, LANES), F32)],
        compiler_params=_cparams(("arbitrary", "arbitrary"), VMEM_LIMIT),
        name="shared_proj",
    )(x, w_big, w_fh, w_fl, b_f, eq, ek, ones_q, ones_k)


def _fox_kernel(q_ref, k_ref, v_ref, o_ref):
    tq = q_ref.shape[0]
    i = pl.program_id(2)
    row = lax.broadcasted_iota(jnp.int32, (tq, tq), 0)
    col = lax.broadcasted_iota(jnp.int32, (tq, tq), 1)

    def attend(kv):
        out = jnp.zeros((tq, LANES), F32)
        for hh in range(2):
            qh = q_ref[:, hh * FOX_HEAD_W:(hh + 1) * FOX_HEAD_W]
            kh = k_ref[:kv, hh * FOX_HEAD_W:(hh + 1) * FOX_HEAD_W]
            sc = lax.dot_general(qh, kh, (((1,), (1,)), ((), ())), preferred_element_type=F32)
            diag = jnp.where(col <= row, sc[:, kv - tq:], -jnp.inf)
            sc = diag if kv == tq else jnp.concatenate([sc[:, :kv - tq], diag], axis=1)
            m = jnp.max(sc, axis=-1, keepdims=True)
            p = jnp.exp(sc - m)
            den = jnp.sum(p, axis=-1, keepdims=True)
            acc = jnp.dot(p.astype(BF16), v_ref[:kv, :], preferred_element_type=F32)
            out = jnp.where(_head_mask(LANES, hh * HEAD_DIM, (hh + 1) * HEAD_DIM), acc / den, out)
        o_ref[...] = out.astype(BF16)

    for c in range(k_ref.shape[0] // tq):
        pl.when(i == c)(functools.partial(attend, (c + 1) * tq))


def _fox_attn(qq, kq, v, batch, tq=256):
    nt = qq.shape[0]
    s = nt // batch
    per_b = s // tq
    pairs = N_FOX_HEADS // 2
    return pl.pallas_call(
        _fox_kernel,
        grid=(batch, pairs, per_b),
        in_specs=[pl.BlockSpec((tq, 2 * FOX_HEAD_W), lambda bi, p, i: (bi * per_b + i, p)),
                  pl.BlockSpec((s, 2 * FOX_HEAD_W), lambda bi, p, i: (bi, p)),
                  pl.BlockSpec((s, LANES), lambda bi, p, i: (bi, p))],
        out_specs=pl.BlockSpec((tq, LANES), lambda bi, p, i: (bi * per_b + i, p)),
        out_shape=jax.ShapeDtypeStruct((nt, FOX_W), BF16),
        compiler_params=_cparams(("parallel", "parallel", "parallel")),
        name="fox_attn",
    )(qq, kq, v)


def _rope_tables(seq):
    inv = 1.0 / (ROPE_THETA ** (jnp.arange(0, HEAD_DIM, 2, dtype=F32) / HEAD_DIM))
    ang = jnp.arange(seq, dtype=F32)[:, None] * inv[None, :]
    return jnp.tile(jnp.cos(ang), (1, DIL_HEADS)), jnp.tile(jnp.sin(ang), (1, DIL_HEADS))


def _rotary_layout(w):
    d = w.shape[0]
    w4 = w.reshape(d, DIL_HEADS, 2, HEAD_DIM // 2)
    return w4.transpose(0, 2, 1, 3).reshape(d, GROUP_W)


def _group_weights(w_in, g, with_mem):
    base = g * 3 * GROUP_W
    cols = [_rotary_layout(w_in[:, base:base + GROUP_W] * ATTN_SCALE),
            _rotary_layout(w_in[:, base + GROUP_W:base + 2 * GROUP_W]),
            w_in[:, base + 2 * GROUP_W:base + 3 * GROUP_W]]
    if with_mem:
        cols.append(w_in[:, DIL_QKV_W:] * ATTN_SCALE)
    return jnp.concatenate(cols, axis=1).astype(BF16)


def kernel(x, mem, w_in_a, w_out_a, w_in_b, w_out_b, w_shared_kvf, b_forget, w_mem_kv, ln_mix_g, ln_mix_b,
           ln_ffn_g, ln_ffn_b, w_router, b_router, w_gate_up, b_gate_up, w_down, b_down):
    b, s, d = x.shape
    nt = b * s
    mem2 = mem.reshape(b * mem.shape[1], d)
    cos_t, sin_t = _rope_tables(s)
    moe = functools.partial(_moe_layer, w_router=w_router, b_router=b_router, w_gate_up=w_gate_up,
                            b_gate_up=b_gate_up, w_down=w_down, b_down=b_down)

    outs, lses, q_mem = [], [], None
    for g, (_, dil) in enumerate(DILATED_PATTERNS):
        res = _group_proj(x, _group_weights(w_in_a[0], g, g == 0), cos_t, sin_t, dil)
        if g == 0:
            q_mem = res[3].reshape(nt, MEM_W)
        o, lse = _dilated_attn(res[0], res[1], res[2], dil)
        outs.append(o)
        lses.append(lse)
    mkv0 = _matmul(mem2, w_mem_kv[0].astype(BF16), BF16, 512)
    memo = _mem_attn(q_mem, mkv0, b)
    x2 = x.reshape(nt, d)
    x2 = _out_proj(_out_proj_a_kernel, outs + lses + [memo], x2, w_out_a[0].astype(BF16),
                   ln_mix_g[0], ln_mix_b[0], "out_proj_a")
    x2 = moe(x2, 0, ln_g=ln_ffn_g[0], ln_b=ln_ffn_b[0])

    kq, v_sh, qq, q_mem = _shared_proj(x2, b, w_shared_kvf, b_forget, w_in_b[0])

    fox = _fox_attn(qq, kq, v_sh, b)
    mkv1 = _matmul(mem2, w_mem_kv[1].astype(BF16), BF16, 512)
    memo = _mem_attn(q_mem, mkv1, b)
    x2 = _out_proj(_out_proj_b_kernel, [fox, memo], x2, w_out_b[0].astype(BF16),
                   ln_mix_g[1], ln_mix_b[1], "out_proj_b")
    x2 = moe(x2, 1, ln_g=ln_ffn_g[1], ln_b=ln_ffn_b[1])
    return x2.reshape(b, s, d)
```

```python
import functools

import jax
import jax.numpy as jnp
from jax import lax
from jax.experimental import pallas as pl
from jax.experimental.pallas import tpu as pltpu

F32 = jnp.float32
BF16 = jnp.bfloat16

D_MODEL = 1024
DEPTH = 2
HEAD_DIM = 64
N_MEM_HEADS = 4
DILATED_PATTERNS = ((128, 1), (512, 4), (2048, 16))
N_DIL_GROUPS = 3
DIL_HEADS = 4
N_FOX_HEADS = 12
GROUP_W = DIL_HEADS * HEAD_DIM
MEM_W = N_MEM_HEADS * HEAD_DIM
FOX_W = N_FOX_HEADS * HEAD_DIM
DIL_QKV_W = N_DIL_GROUPS * 3 * GROUP_W
N_EXPERTS = 32
TOP_K = 4
D_FF = D_MODEL
SWIGLU_LIMIT = 7.0
SWIGLU_ALPHA = 1.702
ROPE_THETA = 10000.0
LN_EPS = 1e-5
ATTN_SCALE = HEAD_DIM ** -0.5
DEEPNORM_ALPHA = (2.0 * DEPTH) ** 0.25
DIL_STEPS = 128

LANES = 128
FOX_HEAD_W = LANES
FOX_QK_W = N_FOX_HEADS * FOX_HEAD_W
MOE_TM = 256
VMEM_LIMIT = 56 * 1024 * 1024


def _cparams(sem, vmem=None):
    return pltpu.CompilerParams(dimension_semantics=sem, vmem_limit_bytes=vmem)


def _head_mask(width, lo, hi):
    lane = lax.broadcasted_iota(jnp.int32, (1, width), 1)
    return (lane >= lo) & (lane < hi)


def _layer_norm_rows(y, g, b):
    mu = jnp.mean(y, axis=-1, keepdims=True)
    yc = y - mu
    var = jnp.mean(yc * yc, axis=-1, keepdims=True)
    return yc * lax.rsqrt(var + LN_EPS) * g + b


def _matmul_kernel(x_ref, w_ref, o_ref):
    o_ref[...] = jnp.dot(x_ref[...].astype(BF16), w_ref[...],
                         preferred_element_type=F32).astype(o_ref.dtype)


def _matmul(x, w, out_dtype, tm):
    m, k = x.shape
    n = w.shape[1]
    return pl.pallas_call(
        _matmul_kernel,
        grid=(m // tm,),
        in_specs=[pl.BlockSpec((tm, k), lambda i: (i, 0)),
                  pl.BlockSpec((k, n), lambda i: (0, 0))],
        out_specs=pl.BlockSpec((tm, n), lambda i: (i, 0)),
        out_shape=jax.ShapeDtypeStruct((m, n), out_dtype),
        compiler_params=_cparams(("parallel",)),
        name="matmul",
    )(x, w)


def _group_proj_kernel(x_ref, w_ref, cos_ref, sin_ref, *o_refs):
    h = jnp.dot(x_ref[0].astype(BF16), w_ref[...], preferred_element_type=F32)
    cos = cos_ref[...]
    sin = sin_ref[...]
    half = GROUP_W // 2
    for j in range(2):
        t1 = h[:, j * GROUP_W:j * GROUP_W + half]
        t2 = h[:, j * GROUP_W + half:(j + 1) * GROUP_W]
        o_refs[j][0, 0, :, :half] = (t1 * cos - t2 * sin).astype(BF16)
        o_refs[j][0, 0, :, half:] = (t2 * cos + t1 * sin).astype(BF16)
    for j in range(2, len(o_refs)):
        o_refs[j][0, 0] = h[:, j * GROUP_W:(j + 1) * GROUP_W].astype(BF16)


def _group_proj(x, w, cos_t, sin_t, dil):
    b, s, d = x.shape
    length = s // dil
    lt = min(length, 512)
    n_out = w.shape[1] // GROUP_W
    xv = x.reshape(b, length, dil * d)
    cv = cos_t.reshape(length, dil * LANES)
    sv = sin_t.reshape(length, dil * LANES)
    o_spec = pl.BlockSpec((1, 1, lt, GROUP_W), lambda bi, r, l: (bi, r, l, 0))
    return pl.pallas_call(
        _group_proj_kernel,
        grid=(b, dil, length // lt),
        in_specs=[pl.BlockSpec((1, lt, d), lambda bi, r, l: (bi, l, r)),
                  pl.BlockSpec(w.shape, lambda bi, r, l: (0, 0)),
                  pl.BlockSpec((lt, LANES), lambda bi, r, l: (l, r)),
                  pl.BlockSpec((lt, LANES), lambda bi, r, l: (l, r))],
        out_specs=[o_spec] * n_out,
        out_shape=[jax.ShapeDtypeStruct((b, dil, length, GROUP_W), BF16)] * n_out,
        compiler_params=_cparams(("parallel", "parallel", "parallel")),
        name=f"group_proj_d{dil}",
    )(xv, w, cv, sv)


def _dilated_attn_kernel(q_ref, kp_ref, kc_ref, vp_ref, vc_ref, o_ref, lse_ref):
    n = pl.program_id(2)
    n_res = q_ref.shape[1]
    n_sub = q_ref.shape[2] // DIL_STEPS
    qi = lax.broadcasted_iota(jnp.int32, (DIL_STEPS, 2 * DIL_STEPS), 0)
    kj = lax.broadcasted_iota(jnp.int32, (DIL_STEPS, 2 * DIL_STEPS), 1)
    band = (kj >= qi) & (kj <= qi + DIL_STEPS)
    half = GROUP_W // 2
    hw = HEAD_DIM // 2
    for r in range(n_res):
        for j in range(n_sub):
            lo, hi = j * DIL_STEPS, (j + 1) * DIL_STEPS
            q = q_ref[0, r, lo:hi, :]
            if j == 0:
                kk = jnp.concatenate([kp_ref[0, r], kc_ref[0, r, lo:hi, :]], axis=0)
                vv = jnp.concatenate([vp_ref[0, r], vc_ref[0, r, lo:hi, :]], axis=0)
                valid = band & ((n > 0) | (kj >= DIL_STEPS))
            else:
                kk = kc_ref[0, r, lo - DIL_STEPS:hi, :]
                vv = vc_ref[0, r, lo - DIL_STEPS:hi, :]
                valid = band
            out = jnp.zeros((DIL_STEPS, GROUP_W), F32)
            lse = jnp.zeros((DIL_STEPS, GROUP_W), F32)
            for h in range(DIL_HEADS):
                qmask = (_head_mask(GROUP_W, h * hw, (h + 1) * hw)
                         | _head_mask(GROUP_W, half + h * hw, half + (h + 1) * hw))
                qm = jnp.where(qmask, q, jnp.zeros_like(q))
                sc = lax.dot_general(qm, kk, (((1,), (1,)), ((), ())), preferred_element_type=F32)
                sc = jnp.where(valid, sc, -jnp.inf)
                m = jnp.max(sc, axis=-1, keepdims=True)
                p = jnp.exp(sc - m)
                den = jnp.sum(p, axis=-1, keepdims=True)
                o_all = jnp.dot(p.astype(BF16), vv, preferred_element_type=F32)
                vmask = _head_mask(GROUP_W, h * HEAD_DIM, (h + 1) * HEAD_DIM)
                out = jnp.where(vmask, o_all / den, out)
                lse = jnp.where(vmask, m + jnp.log(den), lse)
            o_ref[0, lo:hi, r * GROUP_W:(r + 1) * GROUP_W] = out.astype(BF16)
            lse_ref[0, lo:hi, r * GROUP_W:(r + 1) * GROUP_W] = lse


DIL_UNITS = 4


def _dilated_attn(q, k, v, dil):
    b, _, length, _ = q.shape
    n_sub = min(length // DIL_STEPS, DIL_UNITS)
    n_res = DIL_UNITS // n_sub
    rows = n_sub * DIL_STEPS
    cur = pl.BlockSpec((1, n_res, rows, GROUP_W), lambda bi, r, n: (bi, r, n, 0))
    prev = pl.BlockSpec((1, n_res, DIL_STEPS, GROUP_W),
                        lambda bi, r, n: (bi, r, jnp.maximum(n * n_sub - 1, 0), 0))
    out = pl.BlockSpec((1, rows, n_res * GROUP_W), lambda bi, r, n: (bi, n, r))
    o, lse = pl.pallas_call(
        _dilated_attn_kernel,
        grid=(b, dil // n_res, length // rows),
        in_specs=[cur, prev, cur, prev, cur],
        out_specs=[out, out],
        out_shape=[jax.ShapeDtypeStruct((b, length, dil * GROUP_W), BF16),
                   jax.ShapeDtypeStruct((b, length, dil * GROUP_W), F32)],
        compiler_params=_cparams(("parallel", "parallel", "parallel")),
        name=f"dilated_attn_d{dil}",
    )(q, k, k, v, v)
    return o.reshape(b * length * dil, GROUP_W), lse.reshape(b * length * dil, GROUP_W)


def _mem_attn_kernel(q_ref, kv_ref, o_ref):
    q = q_ref[...]
    mk = kv_ref[:, :MEM_W]
    mv = kv_ref[:, MEM_W:]
    out = jnp.zeros(q.shape, F32)
    for h in range(N_MEM_HEADS):
        hmask = _head_mask(MEM_W, h * HEAD_DIM, (h + 1) * HEAD_DIM)
        qm = jnp.where(hmask, q, jnp.zeros_like(q))
        sc = lax.dot_general(qm, mk, (((1,), (1,)), ((), ())), preferred_element_type=F32)
        m = jnp.max(sc, axis=-1, keepdims=True)
        p = jnp.exp(sc - m)
        den = jnp.sum(p, axis=-1, keepdims=True)
        o_all = jnp.dot(p.astype(BF16), mv, preferred_element_type=F32)
        out = jnp.where(hmask, o_all / den, out)
    o_ref[...] = out.astype(BF16)


def _mem_attn(q, mkv, batch, tq=512):
    nt = q.shape[0]
    per_b = nt // batch // tq
    m_len = mkv.shape[0] // batch
    return pl.pallas_call(
        _mem_attn_kernel,
        grid=(batch, per_b),
        in_specs=[pl.BlockSpec((tq, MEM_W), lambda bi, i: (bi * per_b + i, 0)),
                  pl.BlockSpec((m_len, 2 * MEM_W), lambda bi, i: (bi, 0))],
        out_specs=pl.BlockSpec((tq, MEM_W), lambda bi, i: (bi * per_b + i, 0)),
        out_shape=jax.ShapeDtypeStruct((nt, MEM_W), BF16),
        compiler_params=_cparams(("parallel", "parallel")),
        name="mem_attn",
    )(q, mkv)


def _out_proj_a_kernel(o0, o1, o2, l0, l1, l2, memo, x_ref, w_ref, g_ref, b_ref, out_ref):
    la, lb, lc = l0[...], l1[...], l2[...]
    mx = jnp.maximum(jnp.maximum(la, lb), lc)
    ea, eb, ec = jnp.exp(la - mx), jnp.exp(lb - mx), jnp.exp(lc - mx)
    z = ea + eb + ec
    dil = (ea * o0[...].astype(F32) + eb * o1[...].astype(F32) + ec * o2[...].astype(F32)) / z
    mix = jnp.dot(dil.astype(BF16), w_ref[:GROUP_W, :], preferred_element_type=F32)
    mix += jnp.dot(memo[...], w_ref[GROUP_W:, :], preferred_element_type=F32)
    y = DEEPNORM_ALPHA * x_ref[...] + mix
    out_ref[...] = _layer_norm_rows(y, g_ref[...], b_ref[...])


def _out_proj_b_kernel(fox, memo, x_ref, w_ref, g_ref, b_ref, out_ref):
    mix = jnp.dot(fox[...], w_ref[:FOX_W, :], preferred_element_type=F32)
    mix += jnp.dot(memo[...], w_ref[FOX_W:, :], preferred_element_type=F32)
    y = DEEPNORM_ALPHA * x_ref[...] + mix
    out_ref[...] = _layer_norm_rows(y, g_ref[...], b_ref[...])


def _out_proj(kernel_fn, acts, x, w, g, b, name, tm=512):
    nt, d = x.shape
    row = lambda i: (i, 0)
    const = lambda i: (0, 0)
    return pl.pallas_call(
        kernel_fn,
        grid=(nt // tm,),
        in_specs=[pl.BlockSpec((tm, a.shape[1]), row) for a in acts]
        + [pl.BlockSpec((tm, d), row), pl.BlockSpec(w.shape, const),
           pl.BlockSpec((1, d), const), pl.BlockSpec((1, d), const)],
        out_specs=pl.BlockSpec((tm, d), row),
        out_shape=jax.ShapeDtypeStruct((nt, d), F32),
        compiler_params=_cparams(("parallel",)),
        name=name,
    )(*acts, x, w, g.reshape(1, d), b.reshape(1, d))


def _router_kernel(x_ref, wt_ref, b_ref, idx_ref, rank_ref, gate_ref, cnt_ref, carry_ref):
    @pl.when(pl.program_id(0) == 0)
    def _():
        carry_ref[...] = jnp.zeros_like(carry_ref)

    tm = x_ref.shape[0]
    logits = lax.dot_general(wt_ref[...], x_ref[...], (((1,), (1,)), ((), ())),
                             precision=lax.Precision.HIGHEST, preferred_element_type=F32) + b_ref[...]
    sub = lax.broadcasted_iota(jnp.int32, (N_EXPERTS, tm), 0).astype(F32)
    work = logits
    vals, sels, idxs = [], [], []
    for _ in range(TOP_K):
        mk = jnp.max(work, axis=0, keepdims=True)
        ik = jnp.min(jnp.where(work == mk, sub, float(N_EXPERTS)), axis=0, keepdims=True)
        sel = sub == ik
        work = jnp.where(sel, -jnp.inf, work)
        vals.append(mk)
        sels.append(sel)
        idxs.append(ik)
    es = [jnp.exp(v - vals[0]) for v in vals]
    z = es[0] + es[1] + es[2] + es[3]
    hot = jnp.zeros((N_EXPERTS, tm), F32)
    for sel in sels:
        hot = jnp.where(sel, 1.0, hot)
    row = lax.broadcasted_iota(jnp.int32, (tm, tm), 0)
    col = lax.broadcasted_iota(jnp.int32, (tm, tm), 1)
    tri = jnp.where(row < col, 1.0, 0.0).astype(BF16)
    before = jnp.dot(hot.astype(BF16), tri, preferred_element_type=F32) + carry_ref[...]
    k_sub = lax.broadcasted_iota(jnp.int32, (TOP_K, tm), 0)
    g_sub = lax.broadcasted_iota(jnp.int32, (LANES, tm), 0)
    idx_o = jnp.zeros((TOP_K, tm), F32)
    rank_o = jnp.zeros((TOP_K, tm), F32)
    gate_t = jnp.zeros((LANES, tm), F32)
    for k in range(TOP_K):
        rk = jnp.sum(jnp.where(sels[k], before, 0.0), axis=0, keepdims=True)
        idx_o = jnp.where(k_sub == k, idxs[k], idx_o)
        rank_o = jnp.where(k_sub == k, rk, rank_o)
        gate_t = jnp.where(g_sub == k, es[k] / z, gate_t)
    idx_ref[...] = idx_o.astype(jnp.int32)
    rank_ref[...] = rank_o.astype(jnp.int32)
    gate_ref[...] = gate_t.T
    carry_ref[...] += jnp.sum(hot, axis=1, keepdims=True)
    cnt_ref[...] = carry_ref[...].astype(jnp.int32)


def _router(x, w_r, b_r, tm=512):
    nt, d = x.shape
    const = lambda i: (0, 0)
    return pl.pallas_call(
        _router_kernel,
        grid=(nt // tm,),
        in_specs=[pl.BlockSpec((tm, d), lambda i: (i, 0)), pl.BlockSpec((N_EXPERTS, d), const),
                  pl.BlockSpec((N_EXPERTS, 1), const)],
        out_specs=[pl.BlockSpec((TOP_K, tm), lambda i: (0, i)), pl.BlockSpec((TOP_K, tm), lambda i: (0, i)),
                   pl.BlockSpec((tm, LANES), lambda i: (i, 0)), pl.BlockSpec((N_EXPERTS, 1), const)],
        out_shape=[jax.ShapeDtypeStruct((TOP_K, nt), jnp.int32),
                   jax.ShapeDtypeStruct((TOP_K, nt), jnp.int32),
                   jax.ShapeDtypeStruct((nt, LANES), F32),
                   jax.ShapeDtypeStruct((N_EXPERTS, 1), jnp.int32)],
        scratch_shapes=[pltpu.VMEM((N_EXPERTS, 1), F32)],
        compiler_params=_cparams(("arbitrary",)),
        name="router",
    )(x, w_r.T, b_r.reshape(N_EXPERTS, 1))


def _row_copy(src, dst, sem):
    return pltpu.make_async_copy(src, dst, sem)


ROW_ISSUE_UNROLL = 2


def _dispatch_kernel(dest_ref, tail_ref, x_ref, xs_ref, zeros_ref, sem, zsem):
    tm = x_ref.shape[0]
    nt = tm * pl.num_programs(0)
    base = pl.program_id(0) * tm

    def tail_copy(e):
        t = pl.multiple_of(jnp.maximum(tail_ref[e], 0), MOE_TM)
        return _row_copy(zeros_ref, xs_ref.at[pl.ds(t, MOE_TM), :], zsem)

    def unused_copy(j):
        return _row_copy(zeros_ref, xs_ref.at[pl.ds(pl.multiple_of(j * MOE_TM, MOE_TM), MOE_TM), :], zsem)

    @pl.when(pl.program_id(0) == 0)
    def _():
        zeros_ref[...] = jnp.zeros_like(zeros_ref)
        n_blocks = xs_ref.shape[0] // MOE_TM
        first_unused = tail_ref[N_EXPERTS]
        for e in range(N_EXPERTS):
            pl.when(tail_ref[e] >= 0)(lambda e=e: tail_copy(e).start())
        lax.fori_loop(first_unused, n_blocks, lambda j, c: (unused_copy(j).start(), c)[1], 0)
        for e in range(N_EXPERTS):
            pl.when(tail_ref[e] >= 0)(lambda e=e: tail_copy(e).wait())
        lax.fori_loop(first_unused, n_blocks, lambda j, c: (unused_copy(j).wait(), c)[1], 0)

    def issue(i, carry):
        for k in range(TOP_K):
            d = dest_ref[k * nt + base + i]
            _row_copy(x_ref.at[pl.ds(i, 1), :], xs_ref.at[pl.ds(d, 1), :], sem).start()
        return carry

    lax.fori_loop(0, tm, issue, 0, unroll=ROW_ISSUE_UNROLL)
    for _ in range(TOP_K):
        _row_copy(x_ref, xs_ref.at[pl.ds(0, tm), :], sem).wait()


def _dispatch(dest_flat, tail, x, rows, tm=256):
    nt, d = x.shape
    return pl.pallas_call(
        _dispatch_kernel,
        grid_spec=pltpu.PrefetchScalarGridSpec(
            num_scalar_prefetch=2,
            grid=(nt // tm,),
            in_specs=[pl.BlockSpec((tm, d), lambda i, dest, tail: (i, 0))],
            out_specs=pl.BlockSpec(memory_space=pl.ANY),
            scratch_shapes=[pltpu.VMEM((MOE_TM, d), x.dtype),
                            pltpu.SemaphoreType.DMA(()), pltpu.SemaphoreType.DMA(())],
        ),
        out_shape=jax.ShapeDtypeStruct((rows, d), x.dtype),
        compiler_params=_cparams(("arbitrary",)),
        name="moe_dispatch",
    )(dest_flat, tail, x)


def _expert_kernel(exp_ref, first_ref, active_ref, next_ref, slot_ref, xsrc_ref,
                   xs_ref, bgu_ref, bdn_ref, wgu_hbm, wdn_hbm,
                   y_ref, wgu_f32, wdn_f32, wgu_bf, wdn_bf, sem, *, layer):
    del xsrc_ref
    i = pl.program_id(0)

    def weight_copies(e, s):
        return (pltpu.make_async_copy(wgu_hbm.at[layer, e], wgu_f32.at[s], sem.at[0, s]),
                pltpu.make_async_copy(wdn_hbm.at[layer, e], wdn_f32.at[s], sem.at[1, s]))

    @pl.when(i == 0)
    def _():
        for c in weight_copies(exp_ref[0], slot_ref[0]):
            c.start()

    @pl.when(first_ref[i] == 1)
    def _():
        s = slot_ref[i]
        for c in weight_copies(exp_ref[i], s):
            c.wait()

        @pl.when(next_ref[i] >= 0)
        def _():
            for c in weight_copies(next_ref[i], 1 - s):
                c.start()

        wgu_bf[...] = wgu_f32[s].astype(BF16)
        wdn_bf[...] = wdn_f32[s].astype(BF16)

    @pl.when(active_ref[i] == 1)
    def _():
        hb = jnp.dot(xs_ref[...].astype(BF16), wgu_bf[...], preferred_element_type=F32) + bgu_ref[...]
        gate = jnp.minimum(hb[:, :D_FF], SWIGLU_LIMIT)
        up = jnp.clip(hb[:, D_FF:], -SWIGLU_LIMIT, SWIGLU_LIMIT)
        act = (up + 1.0) * gate * jax.nn.sigmoid(SWIGLU_ALPHA * gate)
        y_ref[...] = jnp.dot(act.astype(BF16), wdn_bf[...], preferred_element_type=F32) + bdn_ref[...]

    @pl.when(active_ref[i] == 0)
    def _():
        y_ref[...] = jnp.zeros_like(y_ref)


def _experts(blocks, xs, w_gu, b_gu, w_dn, b_dn, layer):
    rows = xs.shape[0]
    d = w_dn.shape[-1]
    n_blocks = rows // MOE_TM
    return pl.pallas_call(
        functools.partial(_expert_kernel, layer=layer),
        grid_spec=pltpu.PrefetchScalarGridSpec(
            num_scalar_prefetch=6,
            grid=(n_blocks,),
            in_specs=[
                pl.BlockSpec((MOE_TM, d), lambda i, e, f, a, n, s, x: (x[i], 0)),
                pl.BlockSpec((None, None, 1, 2 * D_FF), lambda i, e, f, a, n, s, x: (layer, e[i], 0, 0)),
                pl.BlockSpec((None, None, 1, d), lambda i, e, f, a, n, s, x: (layer, e[i], 0, 0)),
                pl.BlockSpec(memory_space=pl.ANY),
                pl.BlockSpec(memory_space=pl.ANY),
            ],
            out_specs=pl.BlockSpec((MOE_TM, d), lambda i, e, f, a, n, s, x: (i, 0)),
            scratch_shapes=[pltpu.VMEM((2, d, 2 * D_FF), F32), pltpu.VMEM((2, D_FF, d), F32),
                            pltpu.VMEM((d, 2 * D_FF), BF16), pltpu.VMEM((D_FF, d), BF16),
                            pltpu.SemaphoreType.DMA((2, 2))],
        ),
        out_shape=jax.ShapeDtypeStruct((rows, d), F32),
        compiler_params=_cparams(("arbitrary",), VMEM_LIMIT),
        name="moe_experts",
    )(*blocks, xs, b_gu.reshape(DEPTH, N_EXPERTS, 1, 2 * D_FF), b_dn.reshape(DEPTH, N_EXPERTS, 1, d),
      w_gu, w_dn)


def _combine_kernel(dest_ref, x_ref, gate_ref, g_ref, b_ref, y_ref, out_ref, buf, sem):
    tm = x_ref.shape[0]
    step = pl.program_id(0)
    n_steps = pl.num_programs(0)
    nt = tm * n_steps
    slot = step % 2

    def gather_rows(s, to_slot):
        def issue(i, carry):
            for k in range(TOP_K):
                d = dest_ref[k * nt + s * tm + i]
                _row_copy(y_ref.at[pl.ds(d, 1), :], buf.at[to_slot, k, pl.ds(i, 1), :], sem.at[to_slot]).start()
            return carry

        lax.fori_loop(0, tm, issue, 0, unroll=ROW_ISSUE_UNROLL)

    pl.when(step == 0)(lambda: gather_rows(0, 0))
    pl.when(step + 1 < n_steps)(lambda: gather_rows(step + 1, 1 - slot))
    for k in range(TOP_K):
        _row_copy(y_ref.at[pl.ds(0, tm), :], buf.at[slot, k], sem.at[slot]).wait()

    gates = gate_ref[...]
    ffn = gates[:, 0:1] * buf[slot, 0]
    for k in range(1, TOP_K):
        ffn += gates[:, k:k + 1] * buf[slot, k]
    y = DEEPNORM_ALPHA * x_ref[...] + ffn
    out_ref[...] = _layer_norm_rows(y, g_ref[...], b_ref[...])


def _combine(dest_flat, x, gates, g, b, y_rows, tm=256):
    nt, d = x.shape
    return pl.pallas_call(
        _combine_kernel,
        grid_spec=pltpu.PrefetchScalarGridSpec(
            num_scalar_prefetch=1,
            grid=(nt // tm,),
            in_specs=[pl.BlockSpec((tm, d), lambda i, dest: (i, 0)),
                      pl.BlockSpec((tm, LANES), lambda i, dest: (i, 0)),
                      pl.BlockSpec((1, d), lambda i, dest: (0, 0)),
                      pl.BlockSpec((1, d), lambda i, dest: (0, 0)),
                      pl.BlockSpec(memory_space=pl.ANY)],
            out_specs=pl.BlockSpec((tm, d), lambda i, dest: (i, 0)),
            scratch_shapes=[pltpu.VMEM((2, TOP_K, tm, d), F32), pltpu.SemaphoreType.DMA((2,))],
        ),
        out_shape=jax.ShapeDtypeStruct((nt, d), F32),
        compiler_params=_cparams(("arbitrary",), VMEM_LIMIT),
        name="moe_combine",
    )(dest_flat, x, gates, g.reshape(1, d), b.reshape(1, d), y_rows)


def _moe_layer(x, layer, w_router, b_router, w_gate_up, b_gate_up, w_down, b_down, ln_g, ln_b):
    nt, _ = x.shape
    idx, rank, gates, counts = _router(x, w_router[layer], b_router[layer])
    counts = counts[:, 0]
    padded = (counts + MOE_TM - 1) // MOE_TM * MOE_TM
    pend = jnp.cumsum(padded)
    pstart = pend - padded
    experts = jnp.arange(N_EXPERTS, dtype=jnp.int32)
    start_of = jnp.sum(jnp.where(idx[..., None] == experts, pstart, 0), axis=-1)
    dest = (start_of + rank).reshape(TOP_K * nt).astype(jnp.int32)
    n_blocks = nt * TOP_K // MOE_TM + N_EXPERTS
    blk_start = jnp.arange(n_blocks, dtype=jnp.int32) * MOE_TM
    active = blk_start < pend[-1]
    exp_raw = jnp.minimum(jnp.sum(pend[None, :] <= blk_start[:, None], axis=1), N_EXPERTS - 1).astype(jnp.int32)
    last_exp = jnp.max(jnp.where(active, exp_raw, 0))
    block_exp = jnp.where(active, exp_raw, last_exp).astype(jnp.int32)
    prev_exp = jnp.concatenate([jnp.full((1,), -1, jnp.int32), block_exp[:-1]])
    block_first = (active & (block_exp != prev_exp)).astype(jnp.int32)
    has = padded > 0
    slot_e = (jnp.cumsum(has.astype(jnp.int32)) - 1) % 2
    later = lax.cummin(jnp.where(has, experts, N_EXPERTS)[::-1])[::-1]
    next_e = jnp.concatenate([later[1:], jnp.full((1,), N_EXPERTS, jnp.int32)])
    next_e = jnp.where(next_e < N_EXPERTS, next_e, -1)
    n_active = pend[-1] // MOE_TM
    tail = jnp.concatenate([jnp.where(has, pend - MOE_TM, -1), n_active[None]]).astype(jnp.int32)
    blocks = (block_exp, block_first, active.astype(jnp.int32), next_e[block_exp].astype(jnp.int32),
              slot_e[block_exp].astype(jnp.int32),
              jnp.minimum(jnp.arange(n_blocks, dtype=jnp.int32), n_active - 1).astype(jnp.int32))
    xs = _dispatch(dest, tail, x, n_blocks * MOE_TM)
    y_rows = _experts(blocks, xs, w_gate_up, b_gate_up, w_down, b_down, layer)
    return _combine(dest, x, gates, ln_g, ln_b, y_rows)


def _split3(v):
    hi = v.astype(BF16)
    r1 = v - hi.astype(F32)
    mid = r1.astype(BF16)
    lo = (r1 - mid.astype(F32)).astype(BF16)
    return hi, mid, lo


def _shared_proj_kernel(x_ref, w_ref, wfh_ref, wfl_ref, bf_ref, eq_ref, ek_ref, oq_ref, ok_ref,
                        kq_ref, v_ref, qq_ref, qm_ref, carry_ref):
    @pl.when(pl.program_id(1) == 0)
    def _():
        carry_ref[...] = jnp.zeros_like(carry_ref)

    tm = x_ref.shape[0]
    x = x_ref[...]
    xh = x.astype(BF16)
    xl = (x - xh.astype(F32)).astype(BF16)
    h = jnp.dot(xh, w_ref[...], preferred_element_type=F32)
    f = (jnp.dot(xh, wfh_ref[...], preferred_element_type=F32)
         + jnp.dot(xl, wfh_ref[...], preferred_element_type=F32)
         + jnp.dot(xh, wfl_ref[...], preferred_element_type=F32)) + bf_ref[...]
    ls = jnp.minimum(f, 0.0) - jnp.log1p(jnp.exp(-jnp.abs(f)))
    ls = jnp.where(_head_mask(LANES, 0, N_FOX_HEADS), ls, 0.0)
    row = lax.broadcasted_iota(jnp.int32, (tm, tm), 0)
    col = lax.broadcasted_iota(jnp.int32, (tm, tm), 1)
    tri = jnp.where(row >= col, 1.0, 0.0).astype(BF16)
    a, b, c = _split3(ls)
    cum = (jnp.dot(tri, a, preferred_element_type=F32) + jnp.dot(tri, b, preferred_element_type=F32)
           + jnp.dot(tri, c, preferred_element_type=F32)) + carry_ref[...]
    carry_ref[...] = cum[tm - 1:tm, :]
    ch, cm, cl = _split3(cum)
    cat = (ch.astype(F32) + pltpu.roll(cm.astype(F32), 16, 1) + pltpu.roll(cl.astype(F32), 32, 1)).astype(BF16)
    kq = h[:, :FOX_QK_W] + jnp.dot(cat, ek_ref[...], preferred_element_type=F32) + ok_ref[...]
    qq = (h[:, FOX_QK_W + FOX_W:2 * FOX_QK_W + FOX_W]
          + jnp.dot(cat, eq_ref[...], preferred_element_type=F32) + oq_ref[...])
    kq_ref[...] = kq.astype(BF16)
    v_ref[...] = h[:, FOX_QK_W:FOX_QK_W + FOX_W].astype(BF16)
    qq_ref[...] = qq.astype(BF16)
    qm_ref[...] = h[:, 2 * FOX_QK_W + FOX_W:].astype(BF16)


def _widen_heads(w):
    d = w.shape[0]
    w3 = w.reshape(d, N_FOX_HEADS, HEAD_DIM)
    return jnp.concatenate([w3, jnp.zeros_like(w3)], axis=-1).reshape(d, FOX_QK_W)


def _spread_matrices():
    src = jnp.arange(LANES)[:, None]
    dst = jnp.arange(FOX_QK_W)[None, :]
    head = dst // FOX_HEAD_W
    off = dst % FOX_HEAD_W
    part = src // 16
    is_src = (src % 16 == head) & (src % 16 < N_FOX_HEADS) & (part < 3)
    eq = jnp.where(is_src & (off == HEAD_DIM + part), 1.0, 0.0).astype(BF16)
    ek = jnp.where(is_src & (off == HEAD_DIM + 3 + part), -1.0, 0.0).astype(BF16)
    off1 = jnp.arange(FOX_QK_W) % FOX_HEAD_W
    ones_q = ((off1 >= HEAD_DIM + 3) & (off1 < HEAD_DIM + 6)).astype(F32).reshape(1, FOX_QK_W)
    ones_k = ((off1 >= HEAD_DIM) & (off1 < HEAD_DIM + 3)).astype(F32).reshape(1, FOX_QK_W)
    return eq, ek, ones_q, ones_k


def _shared_proj(x, batch, w_shared_kvf, b_forget, w_in_b, tm=256):
    nt, d = x.shape
    per_b = nt // batch // tm
    w_k = _widen_heads(w_shared_kvf[:, :FOX_W])
    w_v = w_shared_kvf[:, FOX_W:2 * FOX_W]
    w_q = _widen_heads(w_in_b[:, :FOX_W] * ATTN_SCALE)
    w_qm = w_in_b[:, FOX_W:] * ATTN_SCALE
    w_big = jnp.concatenate([w_k, w_v, w_q, w_qm], axis=1).astype(BF16)
    w_f = jnp.pad(w_shared_kvf[:, 2 * FOX_W:], ((0, 0), (0, LANES - N_FOX_HEADS)))
    w_fh = w_f.astype(BF16)
    w_fl = (w_f - w_fh.astype(F32)).astype(BF16)
    b_f = jnp.pad(b_forget, (0, LANES - N_FOX_HEADS)).reshape(1, LANES)
    eq, ek, ones_q, ones_k = _spread_matrices()
    nbig = w_big.shape[1]
    row = lambda bi, i: (bi * per_b + i, 0)
    const = lambda bi, i: (0, 0)
    return pl.pallas_call(
        _shared_proj_kernel,
        grid=(batch, per_b),
        in_specs=[pl.BlockSpec((tm, d), row),
                  pl.BlockSpec((d, nbig), const),
                  pl.BlockSpec((d, LANES), const), pl.BlockSpec((d, LANES), const),
                  pl.BlockSpec((1, LANES), const),
                  pl.BlockSpec((LANES, FOX_QK_W), const), pl.BlockSpec((LANES, FOX_QK_W), const),
                  pl.BlockSpec((1, FOX_QK_W), const), pl.BlockSpec((1, FOX_QK_W), const)],
        out_specs=[pl.BlockSpec((tm, FOX_QK_W), row), pl.BlockSpec((tm, FOX_W), row),
                   pl.BlockSpec((tm, FOX_QK_W), row), pl.BlockSpec((tm, MEM_W), row)],
        out_shape=[jax.ShapeDtypeStruct((nt, FOX_QK_W), BF16), jax.ShapeDtypeStruct((nt, FOX_W), BF16),
                   jax.ShapeDtypeStruct((nt, FOX_QK_W), BF16), jax.ShapeDtypeStruct((nt, MEM_W), BF16)],
        scratch_shapes=[pltpu.VMEM((1, LANES), F32)],
        compiler_params=_cparams(("arbitrary", "arbitrary"), VMEM_LIMIT),
        name="shared_proj",
    )(x, w_big, w_fh, w_fl, b_f, eq, ek, ones_q, ones_k)


def _fox_kernel(q_ref, k_ref, v_ref, o_ref):
    tq = q_ref.shape[0]
    i = pl.program_id(2)
    row = lax.broadcasted_iota(jnp.int32, (tq, tq), 0)
    col = lax.broadcasted_iota(jnp.int32, (tq, tq), 1)

    def attend(kv):
        out = jnp.zeros((tq, LANES), F32)
        for hh in range(2):
            qh = q_ref[:, hh * FOX_HEAD_W:(hh + 1) * FOX_HEAD_W]
            kh = k_ref[:kv, hh * FOX_HEAD_W:(hh + 1) * FOX_HEAD_W]
            sc = lax.dot_general(qh, kh, (((1,), (1,)), ((), ())), preferred_element_type=F32)
            diag = jnp.where(col <= row, sc[:, kv - tq:], -jnp.inf)
            sc = diag if kv == tq else jnp.concatenate([sc[:, :kv - tq], diag], axis=1)
            m = jnp.max(sc, axis=-1, keepdims=True)
            p = jnp.exp(sc - m)
            den = jnp.sum(p, axis=-1, keepdims=True)
            acc = jnp.dot(p.astype(BF16), v_ref[:kv, :], preferred_element_type=F32)
            out = jnp.where(_head_mask(LANES, hh * HEAD_DIM, (hh + 1) * HEAD_DIM), acc / den, out)
        o_ref[...] = out.astype(BF16)

    for c in range(k_ref.shape[0] // tq):
        pl.when(i == c)(functools.partial(attend, (c + 1) * tq))


def _fox_attn(qq, kq, v, batch, tq=256):
    nt = qq.shape[0]
    s = nt // batch
    per_b = s // tq
    pairs = N_FOX_HEADS // 2
    return pl.pallas_call(
        _fox_kernel,
        grid=(batch, pairs, per_b),
        in_specs=[pl.BlockSpec((tq, 2 * FOX_HEAD_W), lambda bi, p, i: (bi * per_b + i, p)),
                  pl.BlockSpec((s, 2 * FOX_HEAD_W), lambda bi, p, i: (bi, p)),
                  pl.BlockSpec((s, LANES), lambda bi, p, i: (bi, p))],
        out_specs=pl.BlockSpec((tq, LANES), lambda bi, p, i: (bi * per_b + i, p)),
        out_shape=jax.ShapeDtypeStruct((nt, FOX_W), BF16),
        compiler_params=_cparams(("parallel", "parallel", "parallel")),
        name="fox_attn",
    )(qq, kq, v)


def _rope_tables(seq):
    inv = 1.0 / (ROPE_THETA ** (jnp.arange(0, HEAD_DIM, 2, dtype=F32) / HEAD_DIM))
    ang = jnp.arange(seq, dtype=F32)[:, None] * inv[None, :]
    return jnp.tile(jnp.cos(ang), (1, DIL_HEADS)), jnp.tile(jnp.sin(ang), (1, DIL_HEADS))


def _rotary_layout(w):
    d = w.shape[0]
    w4 = w.reshape(d, DIL_HEADS, 2, HEAD_DIM // 2)
    return w4.transpose(0, 2, 1, 3).reshape(d, GROUP_W)


def _group_weights(w_in, g, with_mem):
    base = g * 3 * GROUP_W
    cols = [_rotary_layout(w_in[:, base:base + GROUP_W] * ATTN_SCALE),
            _rotary_layout(w_in[:, base + GROUP_W:base + 2 * GROUP_W]),
            w_in[:, base + 2 * GROUP_W:base + 3 * GROUP_W]]
    if with_mem:
        cols.append(w_in[:, DIL_QKV_W:] * ATTN_SCALE)
    return jnp.concatenate(cols, axis=1).astype(BF16)


def kernel(x, mem, w_in_a, w_out_a, w_in_b, w_out_b, w_shared_kvf, b_forget, w_mem_kv, ln_mix_g, ln_mix_b,
           ln_ffn_g, ln_ffn_b, w_router, b_router, w_gate_up, b_gate_up, w_down, b_down):
    b, s, d = x.shape
    nt = b * s
    mem2 = mem.reshape(b * mem.shape[1], d)
    cos_t, sin_t = _rope_tables(s)
    moe = functools.partial(_moe_layer, w_router=w_router, b_router=b_router, w_gate_up=w_gate_up,
                            b_gate_up=b_gate_up, w_down=w_down, b_down=b_down)

    outs, lses, q_mem = [], [], None
    for g, (_, dil) in enumerate(DILATED_PATTERNS):
        res = _group_proj(x, _group_weights(w_in_a[0], g, g == 0), cos_t, sin_t, dil)
        if g == 0:
            q_mem = res[3].reshape(nt, MEM_W)
        o, lse = _dilated_attn(res[0], res[1], res[2], dil)
        outs.append(o)
        lses.append(lse)
    mkv0 = _matmul(mem2, w_mem_kv[0].astype(BF16), BF16, 512)
    memo = _mem_attn(q_mem, mkv0, b)
    x2 = x.reshape(nt, d)
    x2 = _out_proj(_out_proj_a_kernel, outs + lses + [memo], x2, w_out_a[0].astype(BF16),
                   ln_mix_g[0], ln_mix_b[0], "out_proj_a")
    x2 = moe(x2, 0, ln_g=ln_ffn_g[0], ln_b=ln_ffn_b[0])

    kq, v_sh, qq, q_mem = _shared_proj(x2, b, w_shared_kvf, b_forget, w_in_b[0])

    fox = _fox_attn(qq, kq, v_sh, b)
    mkv1 = _matmul(mem2, w_mem_kv[1].astype(BF16), BF16, 512)
    memo = _mem_attn(q_mem, mkv1, b)
    x2 = _out_proj(_out_proj_b_kernel, [fox, memo], x2, w_out_b[0].astype(BF16),
                   ln_mix_g[1], ln_mix_b[1], "out_proj_b")
    x2 = moe(x2, 1, ln_g=ln_ffn_g[1], ln_b=ln_ffn_b[1])
    return x2.reshape(b, s, d)
```

```python
import functools

import jax
import jax.numpy as jnp
from jax import lax
from jax.experimental import pallas as pl
from jax.experimental.pallas import tpu as pltpu

F32 = jnp.float32
BF16 = jnp.bfloat16

D_MODEL = 1024
DEPTH = 2
HEAD_DIM = 64
N_MEM_HEADS = 4
DILATED_PATTERNS = ((128, 1), (512, 4), (2048, 16))
N_DIL_GROUPS = 3
DIL_HEADS = 4
N_FOX_HEADS = 12
GROUP_W = DIL_HEADS * HEAD_DIM
MEM_W = N_MEM_HEADS * HEAD_DIM
FOX_W = N_FOX_HEADS * HEAD_DIM
DIL_QKV_W = N_DIL_GROUPS * 3 * GROUP_W
N_EXPERTS = 32
TOP_K = 4
D_FF = D_MODEL
SWIGLU_LIMIT = 7.0
SWIGLU_ALPHA = 1.702
ROPE_THETA = 10000.0
LN_EPS = 1e-5
ATTN_SCALE = HEAD_DIM ** -0.5
DEEPNORM_ALPHA = (2.0 * DEPTH) ** 0.25
DIL_STEPS = 128

LANES = 128
FOX_HEAD_W = LANES
FOX_QK_W = N_FOX_HEADS * FOX_HEAD_W
MOE_TM = 256
VMEM_LIMIT = 56 * 1024 * 1024


def _cparams(sem, vmem=None):
    return pltpu.CompilerParams(dimension_semantics=sem, vmem_limit_bytes=vmem)


def _head_mask(width, lo, hi):
    lane = lax.broadcasted_iota(jnp.int32, (1, width), 1)
    return (lane >= lo) & (lane < hi)


def _layer_norm_rows(y, g, b):
    mu = jnp.mean(y, axis=-1, keepdims=True)
    yc = y - mu
    var = jnp.mean(yc * yc, axis=-1, keepdims=True)
    return yc * lax.rsqrt(var + LN_EPS) * g + b


def _matmul_kernel(x_ref, w_ref, o_ref):
    o_ref[...] = jnp.dot(x_ref[...].astype(BF16), w_ref[...],
                         preferred_element_type=F32).astype(o_ref.dtype)


def _matmul(x, w, out_dtype, tm):
    m, k = x.shape
    n = w.shape[1]
    return pl.pallas_call(
        _matmul_kernel,
        grid=(m // tm,),
        in_specs=[pl.BlockSpec((tm, k), lambda i: (i, 0)),
                  pl.BlockSpec((k, n), lambda i: (0, 0))],
        out_specs=pl.BlockSpec((tm, n), lambda i: (i, 0)),
        out_shape=jax.ShapeDtypeStruct((m, n), out_dtype),
        compiler_params=_cparams(("parallel",)),
        name="matmul",
    )(x, w)


def _group_proj_kernel(x_ref, w_ref, cos_ref, sin_ref, *o_refs):
    h = jnp.dot(x_ref[0].astype(BF16), w_ref[...], preferred_element_type=F32)
    cos = cos_ref[...]
    sin = sin_ref[...]
    half = GROUP_W // 2
    for j in range(2):
        t1 = h[:, j * GROUP_W:j * GROUP_W + half]
        t2 = h[:, j * GROUP_W + half:(j + 1) * GROUP_W]
        o_refs[j][0, 0, :, :half] = (t1 * cos - t2 * sin).astype(BF16)
        o_refs[j][0, 0, :, half:] = (t2 * cos + t1 * sin).astype(BF16)
    for j in range(2, len(o_refs)):
        o_refs[j][0, 0] = h[:, j * GROUP_W:(j + 1) * GROUP_W].astype(BF16)


def _group_proj(x, w, cos_t, sin_t, dil):
    b, s, d = x.shape
    length = s // dil
    lt = min(length, 512)
    n_out = w.shape[1] // GROUP_W
    xv = x.reshape(b, length, dil * d)
    cv = cos_t.reshape(length, dil * LANES)
    sv = sin_t.reshape(length, dil * LANES)
    o_spec = pl.BlockSpec((1, 1, lt, GROUP_W), lambda bi, r, l: (bi, r, l, 0))
    return pl.pallas_call(
        _group_proj_kernel,
        grid=(b, dil, length // lt),
        in_specs=[pl.BlockSpec((1, lt, d), lambda bi, r, l: (bi, l, r)),
                  pl.BlockSpec(w.shape, lambda bi, r, l: (0, 0)),
                  pl.BlockSpec((lt, LANES), lambda bi, r, l: (l, r)),
                  pl.BlockSpec((lt, LANES), lambda bi, r, l: (l, r))],
        out_specs=[o_spec] * n_out,
        out_shape=[jax.ShapeDtypeStruct((b, dil, length, GROUP_W), BF16)] * n_out,
        compiler_params=_cparams(("parallel", "parallel", "parallel")),
        name=f"group_proj_d{dil}",
    )(xv, w, cv, sv)


def _dilated_attn_kernel(q_ref, kp_ref, kc_ref, vp_ref, vc_ref, o_ref, lse_ref):
    n = pl.program_id(2)
    n_res = q_ref.shape[1]
    n_sub = q_ref.shape[2] // DIL_STEPS
    qi = lax.broadcasted_iota(jnp.int32, (DIL_STEPS, 2 * DIL_STEPS), 0)
    kj = lax.broadcasted_iota(jnp.int32, (DIL_STEPS, 2 * DIL_STEPS), 1)
    band = (kj >= qi) & (kj <= qi + DIL_STEPS)
    half = GROUP_W // 2
    hw = HEAD_DIM // 2
    for r in range(n_res):
        for j in range(n_sub):
            lo, hi = j * DIL_STEPS, (j + 1) * DIL_STEPS
            q = q_ref[0, r, lo:hi, :]
            if j == 0:
                kk = jnp.concatenate([kp_ref[0, r], kc_ref[0, r, lo:hi, :]], axis=0)
                vv = jnp.concatenate([vp_ref[0, r], vc_ref[0, r, lo:hi, :]], axis=0)
                valid = band & ((n > 0) | (kj >= DIL_STEPS))
            else:
                kk = kc_ref[0, r, lo - DIL_STEPS:hi, :]
                vv = vc_ref[0, r, lo - DIL_STEPS:hi, :]
                valid = band
            out = jnp.zeros((DIL_STEPS, GROUP_W), F32)
            lse = jnp.zeros((DIL_STEPS, GROUP_W), F32)
            for h in range(DIL_HEADS):
                qmask = (_head_mask(GROUP_W, h * hw, (h + 1) * hw)
                         | _head_mask(GROUP_W, half + h * hw, half + (h + 1) * hw))
                qm = jnp.where(qmask, q, jnp.zeros_like(q))
                sc = lax.dot_general(qm, kk, (((1,), (1,)), ((), ())), preferred_element_type=F32)
                sc = jnp.where(valid, sc, -jnp.inf)
                m = jnp.max(sc, axis=-1, keepdims=True)
                p = jnp.exp(sc - m)
                den = jnp.sum(p, axis=-1, keepdims=True)
                o_all = jnp.dot(p.astype(BF16), vv, preferred_element_type=F32)
                vmask = _head_mask(GROUP_W, h * HEAD_DIM, (h + 1) * HEAD_DIM)
                out = jnp.where(vmask, o_all / den, out)
                lse = jnp.where(vmask, m + jnp.log(den), lse)
            o_ref[0, lo:hi, r * GROUP_W:(r + 1) * GROUP_W] = out.astype(BF16)
            lse_ref[0, lo:hi, r * GROUP_W:(r + 1) * GROUP_W] = lse


DIL_UNITS = 4


def _dilated_attn(q, k, v, dil):
    b, _, length, _ = q.shape
    n_sub = min(length // DIL_STEPS, DIL_UNITS)
    n_res = DIL_UNITS // n_sub
    rows = n_sub * DIL_STEPS
    cur = pl.BlockSpec((1, n_res, rows, GROUP_W), lambda bi, r, n: (bi, r, n, 0))
    prev = pl.BlockSpec((1, n_res, DIL_STEPS, GROUP_W),
                        lambda bi, r, n: (bi, r, jnp.maximum(n * n_sub - 1, 0), 0))
    out = pl.BlockSpec((1, rows, n_res * GROUP_W), lambda bi, r, n: (bi, n, r))
    o, lse = pl.pallas_call(
        _dilated_attn_kernel,
        grid=(b, dil // n_res, length // rows),
        in_specs=[cur, prev, cur, prev, cur],
        out_specs=[out, out],
        out_shape=[jax.ShapeDtypeStruct((b, length, dil * GROUP_W), BF16),
                   jax.ShapeDtypeStruct((b, length, dil * GROUP_W), F32)],
        compiler_params=_cparams(("parallel", "parallel", "parallel")),
        name=f"dilated_attn_d{dil}",
    )(q, k, k, v, v)
    return o.reshape(b * length * dil, GROUP_W), lse.reshape(b * length * dil, GROUP_W)


def _mem_attn_kernel(q_ref, kv_ref, o_ref):
    q = q_ref[...]
    mk = kv_ref[:, :MEM_W]
    mv = kv_ref[:, MEM_W:]
    out = jnp.zeros(q.shape, F32)
    for h in range(N_MEM_HEADS):
        hmask = _head_mask(MEM_W, h * HEAD_DIM, (h + 1) * HEAD_DIM)
        qm = jnp.where(hmask, q, jnp.zeros_like(q))
        sc = lax.dot_general(qm, mk, (((1,), (1,)), ((), ())), preferred_element_type=F32)
        m = jnp.max(sc, axis=-1, keepdims=True)
        p = jnp.exp(sc - m)
        den = jnp.sum(p, axis=-1, keepdims=True)
        o_all = jnp.dot(p.astype(BF16), mv, preferred_element_type=F32)
        out = jnp.where(hmask, o_all / den, out)
    o_ref[...] = out.astype(BF16)


def _mem_attn(q, mkv, batch, tq=512):
    nt = q.shape[0]
    per_b = nt // batch // tq
    m_len = mkv.shape[0] // batch
    return pl.pallas_call(
        _mem_attn_kernel,
        grid=(batch, per_b),
        in_specs=[pl.BlockSpec((tq, MEM_W), lambda bi, i: (bi * per_b + i, 0)),
                  pl.BlockSpec((m_len, 2 * MEM_W), lambda bi, i: (bi, 0))],
        out_specs=pl.BlockSpec((tq, MEM_W), lambda bi, i: (bi * per_b + i, 0)),
        out_shape=jax.ShapeDtypeStruct((nt, MEM_W), BF16),
        compiler_params=_cparams(("parallel", "parallel")),
        name="mem_attn",
    )(q, mkv)


def _out_proj_a_kernel(o0, o1, o2, l0, l1, l2, memo, x_ref, w_ref, g_ref, b_ref, out_ref):
    la, lb, lc = l0[...], l1[...], l2[...]
    mx = jnp.maximum(jnp.maximum(la, lb), lc)
    ea, eb, ec = jnp.exp(la - mx), jnp.exp(lb - mx), jnp.exp(lc - mx)
    z = ea + eb + ec
    dil = (ea * o0[...].astype(F32) + eb * o1[...].astype(F32) + ec * o2[...].astype(F32)) / z
    mix = jnp.dot(dil.astype(BF16), w_ref[:GROUP_W, :], preferred_element_type=F32)
    mix += jnp.dot(memo[...], w_ref[GROUP_W:, :], preferred_element_type=F32)
    y = DEEPNORM_ALPHA * x_ref[...] + mix
    out_ref[...] = _layer_norm_rows(y, g_ref[...], b_ref[...])


def _out_proj_b_kernel(fox, memo, x_ref, w_ref, g_ref, b_ref, out_ref):
    mix = jnp.dot(fox[...], w_ref[:FOX_W, :], preferred_element_type=F32)
    mix += jnp.dot(memo[...], w_ref[FOX_W:, :], preferred_element_type=F32)
    y = DEEPNORM_ALPHA * x_ref[...] + mix
    out_ref[...] = _layer_norm_rows(y, g_ref[...], b_ref[...])


def _out_proj(kernel_fn, acts, x, w, g, b, name, tm=512):
    nt, d = x.shape
    row = lambda i: (i, 0)
    const = lambda i: (0, 0)
    return pl.pallas_call(
        kernel_fn,
        grid=(nt // tm,),
        in_specs=[pl.BlockSpec((tm, a.shape[1]), row) for a in acts]
        + [pl.BlockSpec((tm, d), row), pl.BlockSpec(w.shape, const),
           pl.BlockSpec((1, d), const), pl.BlockSpec((1, d), const)],
        out_specs=pl.BlockSpec((tm, d), row),
        out_shape=jax.ShapeDtypeStruct((nt, d), F32),
        compiler_params=_cparams(("parallel",)),
        name=name,
    )(*acts, x, w, g.reshape(1, d), b.reshape(1, d))


def _router_kernel(x_ref, wt_ref, b_ref, idx_ref, rank_ref, gate_ref, cnt_ref, carry_ref):
    @pl.when(pl.program_id(0) == 0)
    def _():
        carry_ref[...] = jnp.zeros_like(carry_ref)

    tm = x_ref.shape[0]
    logits = lax.dot_general(wt_ref[...], x_ref[...], (((1,), (1,)), ((), ())),
                             precision=lax.Precision.HIGHEST, preferred_element_type=F32) + b_ref[...]
    sub = lax.broadcasted_iota(jnp.int32, (N_EXPERTS, tm), 0).astype(F32)
    work = logits
    vals, sels, idxs = [], [], []
    for _ in range(TOP_K):
        mk = jnp.max(work, axis=0, keepdims=True)
        ik = jnp.min(jnp.where(work == mk, sub, float(N_EXPERTS)), axis=0, keepdims=True)
        sel = sub == ik
        work = jnp.where(sel, -jnp.inf, work)
        vals.append(mk)
        sels.append(sel)
        idxs.append(ik)
    es = [jnp.exp(v - vals[0]) for v in vals]
    z = es[0] + es[1] + es[2] + es[3]
    hot = jnp.zeros((N_EXPERTS, tm), F32)
    for sel in sels:
        hot = jnp.where(sel, 1.0, hot)
    row = lax.broadcasted_iota(jnp.int32, (tm, tm), 0)
    col = lax.broadcasted_iota(jnp.int32, (tm, tm), 1)
    tri = jnp.where(row < col, 1.0, 0.0).astype(BF16)
    before = jnp.dot(hot.astype(BF16), tri, preferred_element_type=F32) + carry_ref[...]
    k_sub = lax.broadcasted_iota(jnp.int32, (TOP_K, tm), 0)
    g_sub = lax.broadcasted_iota(jnp.int32, (LANES, tm), 0)
    idx_o = jnp.zeros((TOP_K, tm), F32)
    rank_o = jnp.zeros((TOP_K, tm), F32)
    gate_t = jnp.zeros((LANES, tm), F32)
    for k in range(TOP_K):
        rk = jnp.sum(jnp.where(sels[k], before, 0.0), axis=0, keepdims=True)
        idx_o = jnp.where(k_sub == k, idxs[k], idx_o)
        rank_o = jnp.where(k_sub == k, rk, rank_o)
        gate_t = jnp.where(g_sub == k, es[k] / z, gate_t)
    idx_ref[...] = idx_o.astype(jnp.int32)
    rank_ref[...] = rank_o.astype(jnp.int32)
    gate_ref[...] = gate_t.T
    carry_ref[...] += jnp.sum(hot, axis=1, keepdims=True)
    cnt_ref[...] = carry_ref[...].astype(jnp.int32)


def _router(x, w_r, b_r, tm=512):
    nt, d = x.shape
    const = lambda i: (0, 0)
    return pl.pallas_call(
        _router_kernel,
        grid=(nt // tm,),
        in_specs=[pl.BlockSpec((tm, d), lambda i: (i, 0)), pl.BlockSpec((N_EXPERTS, d), const),
                  pl.BlockSpec((N_EXPERTS, 1), const)],
        out_specs=[pl.BlockSpec((TOP_K, tm), lambda i: (0, i)), pl.BlockSpec((TOP_K, tm), lambda i: (0, i)),
                   pl.BlockSpec((tm, LANES), lambda i: (i, 0)), pl.BlockSpec((N_EXPERTS, 1), const)],
        out_shape=[jax.ShapeDtypeStruct((TOP_K, nt), jnp.int32),
                   jax.ShapeDtypeStruct((TOP_K, nt), jnp.int32),
                   jax.ShapeDtypeStruct((nt, LANES), F32),
                   jax.ShapeDtypeStruct((N_EXPERTS, 1), jnp.int32)],
        scratch_shapes=[pltpu.VMEM((N_EXPERTS, 1), F32)],
        compiler_params=_cparams(("arbitrary",)),
        name="router",
    )(x, w_r.T, b_r.reshape(N_EXPERTS, 1))


def _row_copy(src, dst, sem):
    return pltpu.make_async_copy(src, dst, sem)


ROW_ISSUE_UNROLL = 4


def _dispatch_kernel(dest_ref, tail_ref, x_ref, xs_ref, zeros_ref, sem, zsem):
    tm = x_ref.shape[0]
    nt = tm * pl.num_programs(0)
    base = pl.program_id(0) * tm

    def tail_copy(e):
        t = pl.multiple_of(jnp.maximum(tail_ref[e], 0), MOE_TM)
        return _row_copy(zeros_ref, xs_ref.at[pl.ds(t, MOE_TM), :], zsem)

    def unused_copy(j):
        return _row_copy(zeros_ref, xs_ref.at[pl.ds(pl.multiple_of(j * MOE_TM, MOE_TM), MOE_TM), :], zsem)

    @pl.when(pl.program_id(0) == 0)
    def _():
        zeros_ref[...] = jnp.zeros_like(zeros_ref)
        n_blocks = xs_ref.shape[0] // MOE_TM
        first_unused = tail_ref[N_EXPERTS]
        for e in range(N_EXPERTS):
            pl.when(tail_ref[e] >= 0)(lambda e=e: tail_copy(e).start())
        lax.fori_loop(first_unused, n_blocks, lambda j, c: (unused_copy(j).start(), c)[1], 0)
        for e in range(N_EXPERTS):
            pl.when(tail_ref[e] >= 0)(lambda e=e: tail_copy(e).wait())
        lax.fori_loop(first_unused, n_blocks, lambda j, c: (unused_copy(j).wait(), c)[1], 0)

    def issue(i, carry):
        for k in range(TOP_K):
            d = dest_ref[k * nt + base + i]
            _row_copy(x_ref.at[pl.ds(i, 1), :], xs_ref.at[pl.ds(d, 1), :], sem).start(priority=k % 2)
        return carry

    lax.fori_loop(0, tm, issue, 0, unroll=ROW_ISSUE_UNROLL)
    for _ in range(TOP_K):
        _row_copy(x_ref, xs_ref.at[pl.ds(0, tm), :], sem).wait()


def _dispatch(dest_flat, tail, x, rows, tm=256):
    nt, d = x.shape
    return pl.pallas_call(
        _dispatch_kernel,
        grid_spec=pltpu.PrefetchScalarGridSpec(
            num_scalar_prefetch=2,
            grid=(nt // tm,),
            in_specs=[pl.BlockSpec((tm, d), lambda i, dest, tail: (i, 0))],
            out_specs=pl.BlockSpec(memory_space=pl.ANY),
            scratch_shapes=[pltpu.VMEM((MOE_TM, d), x.dtype),
                            pltpu.SemaphoreType.DMA(()), pltpu.SemaphoreType.DMA(())],
        ),
        out_shape=jax.ShapeDtypeStruct((rows, d), x.dtype),
        compiler_params=_cparams(("arbitrary",)),
        name="moe_dispatch",
    )(dest_flat, tail, x)


def _expert_kernel(exp_ref, first_ref, active_ref, next_ref, slot_ref, xsrc_ref,
                   xs_ref, bgu_ref, bdn_ref, wgu_hbm, wdn_hbm,
                   y_ref, wgu_f32, wdn_f32, wgu_bf, wdn_bf, sem, *, layer):
    del xsrc_ref
    i = pl.program_id(0)

    def weight_copies(e, s):
        return (pltpu.make_async_copy(wgu_hbm.at[layer, e], wgu_f32.at[s], sem.at[0, s]),
                pltpu.make_async_copy(wdn_hbm.at[layer, e], wdn_f32.at[s], sem.at[1, s]))

    @pl.when(i == 0)
    def _():
        for c in weight_copies(exp_ref[0], slot_ref[0]):
            c.start()

    @pl.when(first_ref[i] == 1)
    def _():
        s = slot_ref[i]
        for c in weight_copies(exp_ref[i], s):
            c.wait()

        @pl.when(next_ref[i] >= 0)
        def _():
            for c in weight_copies(next_ref[i], 1 - s):
                c.start()

        wgu_bf[...] = wgu_f32[s].astype(BF16)
        wdn_bf[...] = wdn_f32[s].astype(BF16)

    @pl.when(active_ref[i] == 1)
    def _():
        hb = jnp.dot(xs_ref[...].astype(BF16), wgu_bf[...], preferred_element_type=F32) + bgu_ref[...]
        gate = jnp.minimum(hb[:, :D_FF], SWIGLU_LIMIT)
        up = jnp.clip(hb[:, D_FF:], -SWIGLU_LIMIT, SWIGLU_LIMIT)
        act = (up + 1.0) * gate * jax.nn.sigmoid(SWIGLU_ALPHA * gate)
        y_ref[...] = jnp.dot(act.astype(BF16), wdn_bf[...], preferred_element_type=F32) + bdn_ref[...]

    @pl.when(active_ref[i] == 0)
    def _():
        y_ref[...] = jnp.zeros_like(y_ref)


def _experts(blocks, xs, w_gu, b_gu, w_dn, b_dn, layer):
    rows = xs.shape[0]
    d = w_dn.shape[-1]
    n_blocks = rows // MOE_TM
    return pl.pallas_call(
        functools.partial(_expert_kernel, layer=layer),
        grid_spec=pltpu.PrefetchScalarGridSpec(
            num_scalar_prefetch=6,
            grid=(n_blocks,),
            in_specs=[
                pl.BlockSpec((MOE_TM, d), lambda i, e, f, a, n, s, x: (x[i], 0)),
                pl.BlockSpec((None, None, 1, 2 * D_FF), lambda i, e, f, a, n, s, x: (layer, e[i], 0, 0)),
                pl.BlockSpec((None, None, 1, d), lambda i, e, f, a, n, s, x: (layer, e[i], 0, 0)),
                pl.BlockSpec(memory_space=pl.ANY),
                pl.BlockSpec(memory_space=pl.ANY),
            ],
            out_specs=pl.BlockSpec((MOE_TM, d), lambda i, e, f, a, n, s, x: (i, 0)),
            scratch_shapes=[pltpu.VMEM((2, d, 2 * D_FF), F32), pltpu.VMEM((2, D_FF, d), F32),
                            pltpu.VMEM((d, 2 * D_FF), BF16), pltpu.VMEM((D_FF, d), BF16),
                            pltpu.SemaphoreType.DMA((2, 2))],
        ),
        out_shape=jax.ShapeDtypeStruct((rows, d), F32),
        compiler_params=_cparams(("arbitrary",), VMEM_LIMIT),
        name="moe_experts",
    )(*blocks, xs, b_gu.reshape(DEPTH, N_EXPERTS, 1, 2 * D_FF), b_dn.reshape(DEPTH, N_EXPERTS, 1, d),
      w_gu, w_dn)


def _combine_kernel(dest_ref, x_ref, gate_ref, g_ref, b_ref, y_ref, out_ref, buf, sem):
    tm = x_ref.shape[0]
    step = pl.program_id(0)
    n_steps = pl.num_programs(0)
    nt = tm * n_steps
    slot = step % 2

    def gather_rows(s, to_slot):
        def issue(i, carry):
            for k in range(TOP_K):
                d = dest_ref[k * nt + s * tm + i]
                _row_copy(y_ref.at[pl.ds(d, 1), :], buf.at[to_slot, k, pl.ds(i, 1), :],
                          sem.at[to_slot]).start(priority=k % 2)
            return carry

        lax.fori_loop(0, tm, issue, 0, unroll=ROW_ISSUE_UNROLL)

    pl.when(step == 0)(lambda: gather_rows(0, 0))
    pl.when(step + 1 < n_steps)(lambda: gather_rows(step + 1, 1 - slot))
    for k in range(TOP_K):
        _row_copy(y_ref.at[pl.ds(0, tm), :], buf.at[slot, k], sem.at[slot]).wait()

    gates = gate_ref[...]
    ffn = gates[:, 0:1] * buf[slot, 0]
    for k in range(1, TOP_K):
        ffn += gates[:, k:k + 1] * buf[slot, k]
    y = DEEPNORM_ALPHA * x_ref[...] + ffn
    out_ref[...] = _layer_norm_rows(y, g_ref[...], b_ref[...])


def _combine(dest_flat, x, gates, g, b, y_rows, tm=256):
    nt, d = x.shape
    return pl.pallas_call(
        _combine_kernel,
        grid_spec=pltpu.PrefetchScalarGridSpec(
            num_scalar_prefetch=1,
            grid=(nt // tm,),
            in_specs=[pl.BlockSpec((tm, d), lambda i, dest: (i, 0)),
                      pl.BlockSpec((tm, LANES), lambda i, dest: (i, 0)),
                      pl.BlockSpec((1, d), lambda i, dest: (0, 0)),
                      pl.BlockSpec((1, d), lambda i, dest: (0, 0)),
                      pl.BlockSpec(memory_space=pl.ANY)],
            out_specs=pl.BlockSpec((tm, d), lambda i, dest: (i, 0)),
            scratch_shapes=[pltpu.VMEM((2, TOP_K, tm, d), F32), pltpu.SemaphoreType.DMA((2,))],
        ),
        out_shape=jax.ShapeDtypeStruct((nt, d), F32),
        compiler_params=_cparams(("arbitrary",), VMEM_LIMIT),
        name="moe_combine",
    )(dest_flat, x, gates, g.reshape(1, d), b.reshape(1, d), y_rows)


def _moe_layer(x, layer, w_router, b_router, w_gate_up, b_gate_up, w_down, b_down, ln_g, ln_b):
    nt, _ = x.shape
    idx, rank, gates, counts = _router(x, w_router[layer], b_router[layer])
    counts = counts[:, 0]
    padded = (counts + MOE_TM - 1) // MOE_TM * MOE_TM
    pend = jnp.cumsum(padded)
    pstart = pend - padded
    experts = jnp.arange(N_EXPERTS, dtype=jnp.int32)
    start_of = jnp.sum(jnp.where(idx[..., None] == experts, pstart, 0), axis=-1)
    dest = (start_of + rank).reshape(TOP_K * nt).astype(jnp.int32)
    n_blocks = nt * TOP_K // MOE_TM + N_EXPERTS
    blk_start = jnp.arange(n_blocks, dtype=jnp.int32) * MOE_TM
    active = blk_start < pend[-1]
    exp_raw = jnp.minimum(jnp.sum(pend[None, :] <= blk_start[:, None], axis=1), N_EXPERTS - 1).astype(jnp.int32)
    last_exp = jnp.max(jnp.where(active, exp_raw, 0))
    block_exp = jnp.where(active, exp_raw, last_exp).astype(jnp.int32)
    prev_exp = jnp.concatenate([jnp.full((1,), -1, jnp.int32), block_exp[:-1]])
    block_first = (active & (block_exp != prev_exp)).astype(jnp.int32)
    has = padded > 0
    slot_e = (jnp.cumsum(has.astype(jnp.int32)) - 1) % 2
    later = lax.cummin(jnp.where(has, experts, N_EXPERTS)[::-1])[::-1]
    next_e = jnp.concatenate([later[1:], jnp.full((1,), N_EXPERTS, jnp.int32)])
    next_e = jnp.where(next_e < N_EXPERTS, next_e, -1)
    n_active = pend[-1] // MOE_TM
    tail = jnp.concatenate([jnp.where(has, pend - MOE_TM, -1), n_active[None]]).astype(jnp.int32)
    blocks = (block_exp, block_first, active.astype(jnp.int32), next_e[block_exp].astype(jnp.int32),
              slot_e[block_exp].astype(jnp.int32),
              jnp.minimum(jnp.arange(n_blocks, dtype=jnp.int32), n_active - 1).astype(jnp.int32))
    xs = _dispatch(dest, tail, x, n_blocks * MOE_TM)
    y_rows = _experts(blocks, xs, w_gate_up, b_gate_up, w_down, b_down, layer)
    return _combine(dest, x, gates, ln_g, ln_b, y_rows)


def _split3(v):
    hi = v.astype(BF16)
    r1 = v - hi.astype(F32)
    mid = r1.astype(BF16)
    lo = (r1 - mid.astype(F32)).astype(BF16)
    return hi, mid, lo


def _shared_proj_kernel(x_ref, w_ref, wfh_ref, wfl_ref, bf_ref, eq_ref, ek_ref, oq_ref, ok_ref,
                        kq_ref, v_ref, qq_ref, qm_ref, carry_ref):
    @pl.when(pl.program_id(1) == 0)
    def _():
        carry_ref[...] = jnp.zeros_like(carry_ref)

    tm = x_ref.shape[0]
    x = x_ref[...]
    xh = x.astype(BF16)
    xl = (x - xh.astype(F32)).astype(BF16)
    h = jnp.dot(xh, w_ref[...], preferred_element_type=F32)
    f = (jnp.dot(xh, wfh_ref[...], preferred_element_type=F32)
         + jnp.dot(xl, wfh_ref[...], preferred_element_type=F32)
         + jnp.dot(xh, wfl_ref[...], preferred_element_type=F32)) + bf_ref[...]
    ls = jnp.minimum(f, 0.0) - jnp.log1p(jnp.exp(-jnp.abs(f)))
    ls = jnp.where(_head_mask(LANES, 0, N_FOX_HEADS), ls, 0.0)
    row = lax.broadcasted_iota(jnp.int32, (tm, tm), 0)
    col = lax.broadcasted_iota(jnp.int32, (tm, tm), 1)
    tri = jnp.where(row >= col, 1.0, 0.0).astype(BF16)
    a, b, c = _split3(ls)
    cum = (jnp.dot(tri, a, preferred_element_type=F32) + jnp.dot(tri, b, preferred_element_type=F32)
           + jnp.dot(tri, c, preferred_element_type=F32)) + carry_ref[...]
    carry_ref[...] = cum[tm - 1:tm, :]
    ch, cm, cl = _split3(cum)
    cat = (ch.astype(F32) + pltpu.roll(cm.astype(F32), 16, 1) + pltpu.roll(cl.astype(F32), 32, 1)).astype(BF16)
    kq = h[:, :FOX_QK_W] + jnp.dot(cat, ek_ref[...], preferred_element_type=F32) + ok_ref[...]
    qq = (h[:, FOX_QK_W + FOX_W:2 * FOX_QK_W + FOX_W]
          + jnp.dot(cat, eq_ref[...], preferred_element_type=F32) + oq_ref[...])
    kq_ref[...] = kq.astype(BF16)
    v_ref[...] = h[:, FOX_QK_W:FOX_QK_W + FOX_W].astype(BF16)
    qq_ref[...] = qq.astype(BF16)
    qm_ref[...] = h[:, 2 * FOX_QK_W + FOX_W:].astype(BF16)


def _widen_heads(w):
    d = w.shape[0]
    w3 = w.reshape(d, N_FOX_HEADS, HEAD_DIM)
    return jnp.concatenate([w3, jnp.zeros_like(w3)], axis=-1).reshape(d, FOX_QK_W)


def _spread_matrices():
    src = jnp.arange(LANES)[:, None]
    dst = jnp.arange(FOX_QK_W)[None, :]
    head = dst // FOX_HEAD_W
    off = dst % FOX_HEAD_W
    part = src // 16
    is_src = (src % 16 == head) & (src % 16 < N_FOX_HEADS) & (part < 3)
    eq = jnp.where(is_src & (off == HEAD_DIM + part), 1.0, 0.0).astype(BF16)
    ek = jnp.where(is_src & (off == HEAD_DIM + 3 + part), -1.0, 0.0).astype(BF16)
    off1 = jnp.arange(FOX_QK_W) % FOX_HEAD_W
    ones_q = ((off1 >= HEAD_DIM + 3) & (off1 < HEAD_DIM + 6)).astype(F32).reshape(1, FOX_QK_W)
    ones_k = ((off1 >= HEAD_DIM) & (off1 < HEAD_DIM + 3)).astype(F32).reshape(1, FOX_QK_W)
    return eq, ek, ones_q, ones_k


def _shared_proj(x, batch, w_shared_kvf, b_forget, w_in_b, tm=256):
    nt, d = x.shape
    per_b = nt // batch // tm
    w_k = _widen_heads(w_shared_kvf[:, :FOX_W])
    w_v = w_shared_kvf[:, FOX_W:2 * FOX_W]
    w_q = _widen_heads(w_in_b[:, :FOX_W] * ATTN_SCALE)
    w_qm = w_in_b[:, FOX_W:] * ATTN_SCALE
    w_big = jnp.concatenate([w_k, w_v, w_q, w_qm], axis=1).astype(BF16)
    w_f = jnp.pad(w_shared_kvf[:, 2 * FOX_W:], ((0, 0), (0, LANES - N_FOX_HEADS)))
    w_fh = w_f.astype(BF16)
    w_fl = (w_f - w_fh.astype(F32)).astype(BF16)
    b_f = jnp.pad(b_forget, (0, LANES - N_FOX_HEADS)).reshape(1, LANES)
    eq, ek, ones_q, ones_k = _spread_matrices()
    nbig = w_big.shape[1]
    row = lambda bi, i: (bi * per_b + i, 0)
    const = lambda bi, i: (0, 0)
    return pl.pallas_call(
        _shared_proj_kernel,
        grid=(batch, per_b),
        in_specs=[pl.BlockSpec((tm, d), row),
                  pl.BlockSpec((d, nbig), const),
                  pl.BlockSpec((d, LANES), const), pl.BlockSpec((d, LANES), const),
                  pl.BlockSpec((1, LANES), const),
                  pl.BlockSpec((LANES, FOX_QK_W), const), pl.BlockSpec((LANES, FOX_QK_W), const),
                  pl.BlockSpec((1, FOX_QK_W), const), pl.BlockSpec((1, FOX_QK_W), const)],
        out_specs=[pl.BlockSpec((tm, FOX_QK_W), row), pl.BlockSpec((tm, FOX_W), row),
                   pl.BlockSpec((tm, FOX_QK_W), row), pl.BlockSpec((tm, MEM_W), row)],
        out_shape=[jax.ShapeDtypeStruct((nt, FOX_QK_W), BF16), jax.ShapeDtypeStruct((nt, FOX_W), BF16),
                   jax.ShapeDtypeStruct((nt, FOX_QK_W), BF16), jax.ShapeDtypeStruct((nt, MEM_W), BF16)],
        scratch_shapes=[pltpu.VMEM((1, LANES), F32)],
        compiler_params=_cparams(("arbitrary", "arbitrary"), VMEM_LIMIT),
        name="shared_proj",
    )(x, w_big, w_fh, w_fl, b_f, eq, ek, ones_q, ones_k)


def _fox_kernel(q_ref, k_ref, v_ref, o_ref):
    tq = q_ref.shape[0]
    i = pl.program_id(2)
    row = lax.broadcasted_iota(jnp.int32, (tq, tq), 0)
    col = lax.broadcasted_iota(jnp.int32, (tq, tq), 1)

    def attend(kv):
        out = jnp.zeros((tq, LANES), F32)
        for hh in range(2):
            qh = q_ref[:, hh * FOX_HEAD_W:(hh + 1) * FOX_HEAD_W]
            kh = k_ref[:kv, hh * FOX_HEAD_W:(hh + 1) * FOX_HEAD_W]
            sc = lax.dot_general(qh, kh, (((1,), (1,)), ((), ())), preferred_element_type=F32)
            diag = jnp.where(col <= row, sc[:, kv - tq:], -jnp.inf)
            sc = diag if kv == tq else jnp.concatenate([sc[:, :kv - tq], diag], axis=1)
            m = jnp.max(sc, axis=-1, keepdims=True)
            p = jnp.exp(sc - m)
            den = jnp.sum(p, axis=-1, keepdims=True)
            acc = jnp.dot(p.astype(BF16), v_ref[:kv, :], preferred_element_type=F32)
            out = jnp.where(_head_mask(LANES, hh * HEAD_DIM, (hh + 1) * HEAD_DIM), acc / den, out)
        o_ref[...] = out.astype(BF16)

    for c in range(k_ref.shape[0] // tq):
        pl.when(i == c)(functools.partial(attend, (c + 1) * tq))


def _fox_attn(qq, kq, v, batch, tq=256):
    nt = qq.shape[0]
    s = nt // batch
    per_b = s // tq
    pairs = N_FOX_HEADS // 2
    return pl.pallas_call(
        _fox_kernel,
        grid=(batch, pairs, per_b),
        in_specs=[pl.BlockSpec((tq, 2 * FOX_HEAD_W), lambda bi, p, i: (bi * per_b + i, p)),
                  pl.BlockSpec((s, 2 * FOX_HEAD_W), lambda bi, p, i: (bi, p)),
                  pl.BlockSpec((s, LANES), lambda bi, p, i: (bi, p))],
        out_specs=pl.BlockSpec((tq, LANES), lambda bi, p, i: (bi * per_b + i, p)),
        out_shape=jax.ShapeDtypeStruct((nt, FOX_W), BF16),
        compiler_params=_cparams(("parallel", "parallel", "parallel")),
        name="fox_attn",
    )(qq, kq, v)


def _rope_tables(seq):
    inv = 1.0 / (ROPE_THETA ** (jnp.arange(0, HEAD_DIM, 2, dtype=F32) / HEAD_DIM))
    ang = jnp.arange(seq, dtype=F32)[:, None] * inv[None, :]
    return jnp.tile(jnp.cos(ang), (1, DIL_HEADS)), jnp.tile(jnp.sin(ang), (1, DIL_HEADS))


def _rotary_layout(w):
    d = w.shape[0]
    w4 = w.reshape(d, DIL_HEADS, 2, HEAD_DIM // 2)
    return w4.transpose(0, 2, 1, 3).reshape(d, GROUP_W)


def _group_weights(w_in, g, with_mem):
    base = g * 3 * GROUP_W
    cols = [_rotary_layout(w_in[:, base:base + GROUP_W] * ATTN_SCALE),
            _rotary_layout(w_in[:, base + GROUP_W:base + 2 * GROUP_W]),
            w_in[:, base + 2 * GROUP_W:base + 3 * GROUP_W]]
    if with_mem:
        cols.append(w_in[:, DIL_QKV_W:] * ATTN_SCALE)
    return jnp.concatenate(cols, axis=1).astype(BF16)


def kernel(x, mem, w_in_a, w_out_a, w_in_b, w_out_b, w_shared_kvf, b_forget, w_mem_kv, ln_mix_g, ln_mix_b,
           ln_ffn_g, ln_ffn_b, w_router, b_router, w_gate_up, b_gate_up, w_down, b_down):
    b, s, d = x.shape
    nt = b * s
    mem2 = mem.reshape(b * mem.shape[1], d)
    cos_t, sin_t = _rope_tables(s)
    moe = functools.partial(_moe_layer, w_router=w_router, b_router=b_router, w_gate_up=w_gate_up,
                            b_gate_up=b_gate_up, w_down=w_down, b_down=b_down)

    outs, lses, q_mem = [], [], None
    for g, (_, dil) in enumerate(DILATED_PATTERNS):
        res = _group_proj(x, _group_weights(w_in_a[0], g, g == 0), cos_t, sin_t, dil)
        if g == 0:
            q_mem = res[3].reshape(nt, MEM_W)
        o, lse = _dilated_attn(res[0], res[1], res[2], dil)
        outs.append(o)
        lses.append(lse)
    mkv0 = _matmul(mem2, w_mem_kv[0].astype(BF16), BF16, 512)
    memo = _mem_attn(q_mem, mkv0, b)
    x2 = x.reshape(nt, d)
    x2 = _out_proj(_out_proj_a_kernel, outs + lses + [memo], x2, w_out_a[0].astype(BF16),
                   ln_mix_g[0], ln_mix_b[0], "out_proj_a")
    x2 = moe(x2, 0, ln_g=ln_ffn_g[0], ln_b=ln_ffn_b[0])

    kq, v_sh, qq, q_mem = _shared_proj(x2, b, w_shared_kvf, b_forget, w_in_b[0])

    fox = _fox_attn(qq, kq, v_sh, b)
    mkv1 = _matmul(mem2, w_mem_kv[1].astype(BF16), BF16, 512)
    memo = _mem_attn(q_mem, mkv1, b)
    x2 = _out_proj(_out_proj_b_kernel, [fox, memo], x2, w_out_b[0].astype(BF16),
                   ln_mix_g[1], ln_mix_b[1], "out_proj_b")
    x2 = moe(x2, 1, ln_g=ln_ffn_g[1], ln_b=ln_ffn_b[1])
    return x2.reshape(b, s, d)
```

```python
import functools

import jax
import jax.numpy as jnp
from jax import lax
from jax.experimental import pallas as pl
from jax.experimental.pallas import tpu as pltpu

F32 = jnp.float32
BF16 = jnp.bfloat16

D_MODEL = 1024
DEPTH = 2
HEAD_DIM = 64
N_MEM_HEADS = 4
DILATED_PATTERNS = ((128, 1), (512, 4), (2048, 16))
N_DIL_GROUPS = 3
DIL_HEADS = 4
N_FOX_HEADS = 12
GROUP_W = DIL_HEADS * HEAD_DIM
MEM_W = N_MEM_HEADS * HEAD_DIM
FOX_W = N_FOX_HEADS * HEAD_DIM
DIL_QKV_W = N_DIL_GROUPS * 3 * GROUP_W
N_EXPERTS = 32
TOP_K = 4
D_FF = D_MODEL
SWIGLU_LIMIT = 7.0
SWIGLU_ALPHA = 1.702
ROPE_THETA = 10000.0
LN_EPS = 1e-5
ATTN_SCALE = HEAD_DIM ** -0.5
DEEPNORM_ALPHA = (2.0 * DEPTH) ** 0.25
DIL_STEPS = 128

LANES = 128
FOX_HEAD_W = LANES
FOX_QK_W = N_FOX_HEADS * FOX_HEAD_W
MOE_TM = 256
VMEM_LIMIT = 56 * 1024 * 1024


def _cparams(sem, vmem=None):
    return pltpu.CompilerParams(dimension_semantics=sem, vmem_limit_bytes=vmem)


def _head_mask(width, lo, hi):
    lane = lax.broadcasted_iota(jnp.int32, (1, width), 1)
    return (lane >= lo) & (lane < hi)


def _layer_norm_rows(y, g, b):
    mu = jnp.mean(y, axis=-1, keepdims=True)
    yc = y - mu
    var = jnp.mean(yc * yc, axis=-1, keepdims=True)
    return yc * lax.rsqrt(var + LN_EPS) * g + b


def _matmul_kernel(x_ref, w_ref, o_ref):
    o_ref[...] = jnp.dot(x_ref[...].astype(BF16), w_ref[...],
                         preferred_element_type=F32).astype(o_ref.dtype)


def _matmul(x, w, out_dtype, tm):
    m, k = x.shape
    n = w.shape[1]
    return pl.pallas_call(
        _matmul_kernel,
        grid=(m // tm,),
        in_specs=[pl.BlockSpec((tm, k), lambda i: (i, 0)),
                  pl.BlockSpec((k, n), lambda i: (0, 0))],
        out_specs=pl.BlockSpec((tm, n), lambda i: (i, 0)),
        out_shape=jax.ShapeDtypeStruct((m, n), out_dtype),
        compiler_params=_cparams(("parallel",)),
        name="matmul",
    )(x, w)


def _group_proj_kernel(x_ref, w_ref, cos_ref, sin_ref, *o_refs):
    n_res = o_refs[0].shape[1]
    d = x_ref.shape[2] // n_res
    half = GROUP_W // 2
    for r in range(n_res):
        h = jnp.dot(x_ref[0, :, r * d:(r + 1) * d].astype(BF16), w_ref[...], preferred_element_type=F32)
        cos = cos_ref[:, r * LANES:(r + 1) * LANES]
        sin = sin_ref[:, r * LANES:(r + 1) * LANES]
        for j in range(2):
            t1 = h[:, j * GROUP_W:j * GROUP_W + half]
            t2 = h[:, j * GROUP_W + half:(j + 1) * GROUP_W]
            o_refs[j][0, r, :, :half] = (t1 * cos - t2 * sin).astype(BF16)
            o_refs[j][0, r, :, half:] = (t2 * cos + t1 * sin).astype(BF16)
        for j in range(2, len(o_refs)):
            o_refs[j][0, r] = h[:, j * GROUP_W:(j + 1) * GROUP_W].astype(BF16)


PROJ_ROWS = 512


def _group_proj(x, w, cos_t, sin_t, dil):
    b, s, d = x.shape
    length = s // dil
    lt = min(length, PROJ_ROWS)
    n_res = min(PROJ_ROWS // lt, dil)
    n_out = w.shape[1] // GROUP_W
    xv = x.reshape(b, length, dil * d)
    cv = cos_t.reshape(length, dil * LANES)
    sv = sin_t.reshape(length, dil * LANES)
    o_spec = pl.BlockSpec((1, n_res, lt, GROUP_W), lambda bi, r, l: (bi, r, l, 0))
    return pl.pallas_call(
        _group_proj_kernel,
        grid=(b, dil // n_res, length // lt),
        in_specs=[pl.BlockSpec((1, lt, n_res * d), lambda bi, r, l: (bi, l, r)),
                  pl.BlockSpec(w.shape, lambda bi, r, l: (0, 0)),
                  pl.BlockSpec((lt, n_res * LANES), lambda bi, r, l: (l, r)),
                  pl.BlockSpec((lt, n_res * LANES), lambda bi, r, l: (l, r))],
        out_specs=[o_spec] * n_out,
        out_shape=[jax.ShapeDtypeStruct((b, dil, length, GROUP_W), BF16)] * n_out,
        compiler_params=_cparams(("parallel", "parallel", "parallel")),
        name=f"group_proj_d{dil}",
    )(xv, w, cv, sv)


def _dilated_attn_kernel(q_ref, kp_ref, kc_ref, vp_ref, vc_ref, o_ref, lse_ref):
    n = pl.program_id(2)
    n_res = q_ref.shape[1]
    n_sub = q_ref.shape[2] // DIL_STEPS
    qi = lax.broadcasted_iota(jnp.int32, (DIL_STEPS, 2 * DIL_STEPS), 0)
    kj = lax.broadcasted_iota(jnp.int32, (DIL_STEPS, 2 * DIL_STEPS), 1)
    band = (kj >= qi) & (kj <= qi + DIL_STEPS)
    half = GROUP_W // 2
    hw = HEAD_DIM // 2
    for r in range(n_res):
        for j in range(n_sub):
            lo, hi = j * DIL_STEPS, (j + 1) * DIL_STEPS
            q = q_ref[0, r, lo:hi, :]
            if j == 0:
                kk = jnp.concatenate([kp_ref[0, r], kc_ref[0, r, lo:hi, :]], axis=0)
                vv = jnp.concatenate([vp_ref[0, r], vc_ref[0, r, lo:hi, :]], axis=0)
                valid = band & ((n > 0) | (kj >= DIL_STEPS))
            else:
                kk = kc_ref[0, r, lo - DIL_STEPS:hi, :]
                vv = vc_ref[0, r, lo - DIL_STEPS:hi, :]
                valid = band
            out = jnp.zeros((DIL_STEPS, GROUP_W), F32)
            lse = jnp.zeros((DIL_STEPS, GROUP_W), F32)
            for h in range(DIL_HEADS):
                qmask = (_head_mask(GROUP_W, h * hw, (h + 1) * hw)
                         | _head_mask(GROUP_W, half + h * hw, half + (h + 1) * hw))
                qm = jnp.where(qmask, q, jnp.zeros_like(q))
                sc = lax.dot_general(qm, kk, (((1,), (1,)), ((), ())), preferred_element_type=F32)
                sc = jnp.where(valid, sc, -jnp.inf)
                m = jnp.max(sc, axis=-1, keepdims=True)
                p = jnp.exp(sc - m)
                den = jnp.sum(p, axis=-1, keepdims=True)
                o_all = jnp.dot(p.astype(BF16), vv, preferred_element_type=F32)
                vmask = _head_mask(GROUP_W, h * HEAD_DIM, (h + 1) * HEAD_DIM)
                out = jnp.where(vmask, o_all / den, out)
                lse = jnp.where(vmask, m + jnp.log(den), lse)
            o_ref[0, lo:hi, r * GROUP_W:(r + 1) * GROUP_W] = out.astype(BF16)
            lse_ref[0, lo:hi, r * GROUP_W:(r + 1) * GROUP_W] = lse


DIL_UNITS = 4


def _dilated_attn(q, k, v, dil):
    b, _, length, _ = q.shape
    n_sub = min(length // DIL_STEPS, DIL_UNITS)
    n_res = DIL_UNITS // n_sub
    rows = n_sub * DIL_STEPS
    cur = pl.BlockSpec((1, n_res, rows, GROUP_W), lambda bi, r, n: (bi, r, n, 0))
    prev = pl.BlockSpec((1, n_res, DIL_STEPS, GROUP_W),
                        lambda bi, r, n: (bi, r, jnp.maximum(n * n_sub - 1, 0), 0))
    out = pl.BlockSpec((1, rows, n_res * GROUP_W), lambda bi, r, n: (bi, n, r))
    o, lse = pl.pallas_call(
        _dilated_attn_kernel,
        grid=(b, dil // n_res, length // rows),
        in_specs=[cur, prev, cur, prev, cur],
        out_specs=[out, out],
        out_shape=[jax.ShapeDtypeStruct((b, length, dil * GROUP_W), BF16),
                   jax.ShapeDtypeStruct((b, length, dil * GROUP_W), F32)],
        compiler_params=_cparams(("parallel", "parallel", "parallel")),
        name=f"dilated_attn_d{dil}",
    )(q, k, k, v, v)
    return o.reshape(b * length * dil, GROUP_W), lse.reshape(b * length * dil, GROUP_W)


def _mem_attn_kernel(q_ref, kv_ref, o_ref):
    q = q_ref[...]
    mk = kv_ref[:, :MEM_W]
    mv = kv_ref[:, MEM_W:]
    out = jnp.zeros(q.shape, F32)
    for h in range(N_MEM_HEADS):
        hmask = _head_mask(MEM_W, h * HEAD_DIM, (h + 1) * HEAD_DIM)
        qm = jnp.where(hmask, q, jnp.zeros_like(q))
        sc = lax.dot_general(qm, mk, (((1,), (1,)), ((), ())), preferred_element_type=F32)
        m = jnp.max(sc, axis=-1, keepdims=True)
        p = jnp.exp(sc - m)
        den = jnp.sum(p, axis=-1, keepdims=True)
        o_all = jnp.dot(p.astype(BF16), mv, preferred_element_type=F32)
        out = jnp.where(hmask, o_all / den, out)
    o_ref[...] = out.astype(BF16)


def _mem_attn(q, mkv, batch, tq=512):
    nt = q.shape[0]
    per_b = nt // batch // tq
    m_len = mkv.shape[0] // batch
    return pl.pallas_call(
        _mem_attn_kernel,
        grid=(batch, per_b),
        in_specs=[pl.BlockSpec((tq, MEM_W), lambda bi, i: (bi * per_b + i, 0)),
                  pl.BlockSpec((m_len, 2 * MEM_W), lambda bi, i: (bi, 0))],
        out_specs=pl.BlockSpec((tq, MEM_W), lambda bi, i: (bi * per_b + i, 0)),
        out_shape=jax.ShapeDtypeStruct((nt, MEM_W), BF16),
        compiler_params=_cparams(("parallel", "parallel")),
        name="mem_attn",
    )(q, mkv)


def _out_proj_a_kernel(o0, o1, o2, l0, l1, l2, memo, x_ref, w_ref, g_ref, b_ref, out_ref):
    la, lb, lc = l0[...], l1[...], l2[...]
    mx = jnp.maximum(jnp.maximum(la, lb), lc)
    ea, eb, ec = jnp.exp(la - mx), jnp.exp(lb - mx), jnp.exp(lc - mx)
    z = ea + eb + ec
    dil = (ea * o0[...].astype(F32) + eb * o1[...].astype(F32) + ec * o2[...].astype(F32)) / z
    mix = jnp.dot(dil.astype(BF16), w_ref[:GROUP_W, :], preferred_element_type=F32)
    mix += jnp.dot(memo[...], w_ref[GROUP_W:, :], preferred_element_type=F32)
    y = DEEPNORM_ALPHA * x_ref[...] + mix
    out_ref[...] = _layer_norm_rows(y, g_ref[...], b_ref[...])


def _out_proj_b_kernel(fox, memo, x_ref, w_ref, g_ref, b_ref, out_ref):
    mix = jnp.dot(fox[...], w_ref[:FOX_W, :], preferred_element_type=F32)
    mix += jnp.dot(memo[...], w_ref[FOX_W:, :], preferred_element_type=F32)
    y = DEEPNORM_ALPHA * x_ref[...] + mix
    out_ref[...] = _layer_norm_rows(y, g_ref[...], b_ref[...])


def _out_proj(kernel_fn, acts, x, w, g, b, name, tm=512):
    nt, d = x.shape
    row = lambda i: (i, 0)
    const = lambda i: (0, 0)
    return pl.pallas_call(
        kernel_fn,
        grid=(nt // tm,),
        in_specs=[pl.BlockSpec((tm, a.shape[1]), row) for a in acts]
        + [pl.BlockSpec((tm, d), row), pl.BlockSpec(w.shape, const),
           pl.BlockSpec((1, d), const), pl.BlockSpec((1, d), const)],
        out_specs=pl.BlockSpec((tm, d), row),
        out_shape=jax.ShapeDtypeStruct((nt, d), F32),
        compiler_params=_cparams(("parallel",)),
        name=name,
    )(*acts, x, w, g.reshape(1, d), b.reshape(1, d))


def _router_kernel(x_ref, wt_ref, b_ref, idx_ref, rank_ref, gate_ref, cnt_ref, carry_ref):
    @pl.when(pl.program_id(0) == 0)
    def _():
        carry_ref[...] = jnp.zeros_like(carry_ref)

    tm = x_ref.shape[0]
    logits = lax.dot_general(wt_ref[...], x_ref[...], (((1,), (1,)), ((), ())),
                             precision=lax.Precision.HIGHEST, preferred_element_type=F32) + b_ref[...]
    sub = lax.broadcasted_iota(jnp.int32, (N_EXPERTS, tm), 0).astype(F32)
    work = logits
    vals, sels, idxs = [], [], []
    for _ in range(TOP_K):
        mk = jnp.max(work, axis=0, keepdims=True)
        ik = jnp.min(jnp.where(work == mk, sub, float(N_EXPERTS)), axis=0, keepdims=True)
        sel = sub == ik
        work = jnp.where(sel, -jnp.inf, work)
        vals.append(mk)
        sels.append(sel)
        idxs.append(ik)
    es = [jnp.exp(v - vals[0]) for v in vals]
    z = es[0] + es[1] + es[2] + es[3]
    hot = jnp.zeros((N_EXPERTS, tm), F32)
    for sel in sels:
        hot = jnp.where(sel, 1.0, hot)
    row = lax.broadcasted_iota(jnp.int32, (tm, tm), 0)
    col = lax.broadcasted_iota(jnp.int32, (tm, tm), 1)
    tri = jnp.where(row < col, 1.0, 0.0).astype(BF16)
    before = jnp.dot(hot.astype(BF16), tri, preferred_element_type=F32) + carry_ref[...]
    k_sub = lax.broadcasted_iota(jnp.int32, (TOP_K, tm), 0)
    g_sub = lax.broadcasted_iota(jnp.int32, (LANES, tm), 0)
    idx_o = jnp.zeros((TOP_K, tm), F32)
    rank_o = jnp.zeros((TOP_K, tm), F32)
    gate_t = jnp.zeros((LANES, tm), F32)
    for k in range(TOP_K):
        rk = jnp.sum(jnp.where(sels[k], before, 0.0), axis=0, keepdims=True)
        idx_o = jnp.where(k_sub == k, idxs[k], idx_o)
        rank_o = jnp.where(k_sub == k, rk, rank_o)
        gate_t = jnp.where(g_sub == k, es[k] / z, gate_t)
    idx_ref[...] = idx_o.astype(jnp.int32)
    rank_ref[...] = rank_o.astype(jnp.int32)
    gate_ref[...] = gate_t.T
    carry_ref[...] += jnp.sum(hot, axis=1, keepdims=True)
    cnt_ref[...] = carry_ref[...].astype(jnp.int32)


def _router(x, w_r, b_r, tm=512):
    nt, d = x.shape
    const = lambda i: (0, 0)
    return pl.pallas_call(
        _router_kernel,
        grid=(nt // tm,),
        in_specs=[pl.BlockSpec((tm, d), lambda i: (i, 0)), pl.BlockSpec((N_EXPERTS, d), const),
                  pl.BlockSpec((N_EXPERTS, 1), const)],
        out_specs=[pl.BlockSpec((TOP_K, tm), lambda i: (0, i)), pl.BlockSpec((TOP_K, tm), lambda i: (0, i)),
                   pl.BlockSpec((tm, LANES), lambda i: (i, 0)), pl.BlockSpec((N_EXPERTS, 1), const)],
        out_shape=[jax.ShapeDtypeStruct((TOP_K, nt), jnp.int32),
                   jax.ShapeDtypeStruct((TOP_K, nt), jnp.int32),
                   jax.ShapeDtypeStruct((nt, LANES), F32),
                   jax.ShapeDtypeStruct((N_EXPERTS, 1), jnp.int32)],
        scratch_shapes=[pltpu.VMEM((N_EXPERTS, 1), F32)],
        compiler_params=_cparams(("arbitrary",)),
        name="router",
    )(x, w_r.T, b_r.reshape(N_EXPERTS, 1))


def _row_copy(src, dst, sem):
    return pltpu.make_async_copy(src, dst, sem)


ROW_ISSUE_UNROLL = 4


def _dispatch_kernel(dest_ref, tail_ref, x_ref, xs_ref, zeros_ref, sem, zsem):
    tm = x_ref.shape[0]
    nt = tm * pl.num_programs(0)
    base = pl.program_id(0) * tm

    def tail_copy(e):
        t = pl.multiple_of(jnp.maximum(tail_ref[e], 0), MOE_TM)
        return _row_copy(zeros_ref, xs_ref.at[pl.ds(t, MOE_TM), :], zsem)

    def unused_copy(j):
        return _row_copy(zeros_ref, xs_ref.at[pl.ds(pl.multiple_of(j * MOE_TM, MOE_TM), MOE_TM), :], zsem)

    @pl.when(pl.program_id(0) == 0)
    def _():
        zeros_ref[...] = jnp.zeros_like(zeros_ref)
        n_blocks = xs_ref.shape[0] // MOE_TM
        first_unused = tail_ref[N_EXPERTS]
        for e in range(N_EXPERTS):
            pl.when(tail_ref[e] >= 0)(lambda e=e: tail_copy(e).start())
        lax.fori_loop(first_unused, n_blocks, lambda j, c: (unused_copy(j).start(), c)[1], 0)
        for e in range(N_EXPERTS):
            pl.when(tail_ref[e] >= 0)(lambda e=e: tail_copy(e).wait())
        lax.fori_loop(first_unused, n_blocks, lambda j, c: (unused_copy(j).wait(), c)[1], 0)

    def issue(i, carry):
        for k in range(TOP_K):
            d = dest_ref[k * nt + base + i]
            _row_copy(x_ref.at[pl.ds(i, 1), :], xs_ref.at[pl.ds(d, 1), :], sem).start(priority=k % 2)
        return carry

    lax.fori_loop(0, tm, issue, 0, unroll=ROW_ISSUE_UNROLL)
    for _ in range(TOP_K):
        _row_copy(x_ref, xs_ref.at[pl.ds(0, tm), :], sem).wait()


def _dispatch(dest_flat, tail, x, rows, tm=256):
    nt, d = x.shape
    return pl.pallas_call(
        _dispatch_kernel,
        grid_spec=pltpu.PrefetchScalarGridSpec(
            num_scalar_prefetch=2,
            grid=(nt // tm,),
            in_specs=[pl.BlockSpec((tm, d), lambda i, dest, tail: (i, 0))],
            out_specs=pl.BlockSpec(memory_space=pl.ANY),
            scratch_shapes=[pltpu.VMEM((MOE_TM, d), x.dtype),
                            pltpu.SemaphoreType.DMA(()), pltpu.SemaphoreType.DMA(())],
        ),
        out_shape=jax.ShapeDtypeStruct((rows, d), x.dtype),
        compiler_params=_cparams(("arbitrary",)),
        name="moe_dispatch",
    )(dest_flat, tail, x)


def _expert_kernel(exp_ref, first_ref, active_ref, next_ref, slot_ref, xsrc_ref,
                   xs_ref, bgu_ref, bdn_ref, wgu_hbm, wdn_hbm,
                   y_ref, wgu_f32, wdn_f32, wgu_bf, wdn_bf, sem, *, layer):
    del xsrc_ref
    i = pl.program_id(0)

    def weight_copies(e, s):
        return (pltpu.make_async_copy(wgu_hbm.at[layer, e], wgu_f32.at[s], sem.at[0, s]),
                pltpu.make_async_copy(wdn_hbm.at[layer, e], wdn_f32.at[s], sem.at[1, s]))

    @pl.when(i == 0)
    def _():
        for c in weight_copies(exp_ref[0], slot_ref[0]):
            c.start()

    @pl.when(first_ref[i] == 1)
    def _():
        s = slot_ref[i]
        for c in weight_copies(exp_ref[i], s):
            c.wait()

        @pl.when(next_ref[i] >= 0)
        def _():
            for c in weight_copies(next_ref[i], 1 - s):
                c.start()

        wgu_bf[...] = wgu_f32[s].astype(BF16)
        wdn_bf[...] = wdn_f32[s].astype(BF16)

    @pl.when(active_ref[i] == 1)
    def _():
        hb = jnp.dot(xs_ref[...].astype(BF16), wgu_bf[...], preferred_element_type=F32) + bgu_ref[...]
        gate = jnp.minimum(hb[:, :D_FF], SWIGLU_LIMIT)
        up = jnp.clip(hb[:, D_FF:], -SWIGLU_LIMIT, SWIGLU_LIMIT)
        act = (up + 1.0) * gate * jax.nn.sigmoid(SWIGLU_ALPHA * gate)
        y_ref[...] = jnp.dot(act.astype(BF16), wdn_bf[...], preferred_element_type=F32) + bdn_ref[...]

    @pl.when(active_ref[i] == 0)
    def _():
        y_ref[...] = jnp.zeros_like(y_ref)


def _experts(blocks, xs, w_gu, b_gu, w_dn, b_dn, layer):
    rows = xs.shape[0]
    d = w_dn.shape[-1]
    n_blocks = rows // MOE_TM
    return pl.pallas_call(
        functools.partial(_expert_kernel, layer=layer),
        grid_spec=pltpu.PrefetchScalarGridSpec(
            num_scalar_prefetch=6,
            grid=(n_blocks,),
            in_specs=[
                pl.BlockSpec((MOE_TM, d), lambda i, e, f, a, n, s, x: (x[i], 0)),
                pl.BlockSpec((None, None, 1, 2 * D_FF), lambda i, e, f, a, n, s, x: (layer, e[i], 0, 0)),
                pl.BlockSpec((None, None, 1, d), lambda i, e, f, a, n, s, x: (layer, e[i], 0, 0)),
                pl.BlockSpec(memory_space=pl.ANY),
                pl.BlockSpec(memory_space=pl.ANY),
            ],
            out_specs=pl.BlockSpec((MOE_TM, d), lambda i, e, f, a, n, s, x: (i, 0)),
            scratch_shapes=[pltpu.VMEM((2, d, 2 * D_FF), F32), pltpu.VMEM((2, D_FF, d), F32),
                            pltpu.VMEM((d, 2 * D_FF), BF16), pltpu.VMEM((D_FF, d), BF16),
                            pltpu.SemaphoreType.DMA((2, 2))],
        ),
        out_shape=jax.ShapeDtypeStruct((rows, d), F32),
        compiler_params=_cparams(("arbitrary",), VMEM_LIMIT),
        name="moe_experts",
    )(*blocks, xs, b_gu.reshape(DEPTH, N_EXPERTS, 1, 2 * D_FF), b_dn.reshape(DEPTH, N_EXPERTS, 1, d),
      w_gu, w_dn)


def _combine_kernel(dest_ref, x_ref, gate_ref, g_ref, b_ref, y_ref, out_ref, buf, sem):
    tm = x_ref.shape[0]
    step = pl.program_id(0)
    n_steps = pl.num_programs(0)
    nt = tm * n_steps
    slot = step % 2

    def gather_rows(s, to_slot):
        def issue(i, carry):
            for k in range(TOP_K):
                d = dest_ref[k * nt + s * tm + i]
                _row_copy(y_ref.at[pl.ds(d, 1), :], buf.at[to_slot, k, pl.ds(i, 1), :],
                          sem.at[to_slot]).start(priority=k % 2)
            return carry

        lax.fori_loop(0, tm, issue, 0, unroll=ROW_ISSUE_UNROLL)

    pl.when(step == 0)(lambda: gather_rows(0, 0))
    pl.when(step + 1 < n_steps)(lambda: gather_rows(step + 1, 1 - slot))
    for k in range(TOP_K):
        _row_copy(y_ref.at[pl.ds(0, tm), :], buf.at[slot, k], sem.at[slot]).wait()

    gates = gate_ref[...]
    ffn = gates[:, 0:1] * buf[slot, 0]
    for k in range(1, TOP_K):
        ffn += gates[:, k:k + 1] * buf[slot, k]
    y = DEEPNORM_ALPHA * x_ref[...] + ffn
    out_ref[...] = _layer_norm_rows(y, g_ref[...], b_ref[...])


def _combine(dest_flat, x, gates, g, b, y_rows, tm=256):
    nt, d = x.shape
    return pl.pallas_call(
        _combine_kernel,
        grid_spec=pltpu.PrefetchScalarGridSpec(
            num_scalar_prefetch=1,
            grid=(nt // tm,),
            in_specs=[pl.BlockSpec((tm, d), lambda i, dest: (i, 0)),
                      pl.BlockSpec((tm, LANES), lambda i, dest: (i, 0)),
                      pl.BlockSpec((1, d), lambda i, dest: (0, 0)),
                      pl.BlockSpec((1, d), lambda i, dest: (0, 0)),
                      pl.BlockSpec(memory_space=pl.ANY)],
            out_specs=pl.BlockSpec((tm, d), lambda i, dest: (i, 0)),
            scratch_shapes=[pltpu.VMEM((2, TOP_K, tm, d), F32), pltpu.SemaphoreType.DMA((2,))],
        ),
        out_shape=jax.ShapeDtypeStruct((nt, d), F32),
        compiler_params=_cparams(("arbitrary",), VMEM_LIMIT),
        name="moe_combine",
    )(dest_flat, x, gates, g.reshape(1, d), b.reshape(1, d), y_rows)


def _moe_layer(x, layer, w_router, b_router, w_gate_up, b_gate_up, w_down, b_down, ln_g, ln_b):
    nt, _ = x.shape
    idx, rank, gates, counts = _router(x, w_router[layer], b_router[layer])
    counts = counts[:, 0]
    padded = (counts + MOE_TM - 1) // MOE_TM * MOE_TM
    pend = jnp.cumsum(padded)
    pstart = pend - padded
    experts = jnp.arange(N_EXPERTS, dtype=jnp.int32)
    start_of = jnp.sum(jnp.where(idx[..., None] == experts, pstart, 0), axis=-1)
    dest = (start_of + rank).reshape(TOP_K * nt).astype(jnp.int32)
    n_blocks = nt * TOP_K // MOE_TM + N_EXPERTS
    blk_start = jnp.arange(n_blocks, dtype=jnp.int32) * MOE_TM
    active = blk_start < pend[-1]
    exp_raw = jnp.minimum(jnp.sum(pend[None, :] <= blk_start[:, None], axis=1), N_EXPERTS - 1).astype(jnp.int32)
    last_exp = jnp.max(jnp.where(active, exp_raw, 0))
    block_exp = jnp.where(active, exp_raw, last_exp).astype(jnp.int32)
    prev_exp = jnp.concatenate([jnp.full((1,), -1, jnp.int32), block_exp[:-1]])
    block_first = (active & (block_exp != prev_exp)).astype(jnp.int32)
    has = padded > 0
    slot_e = (jnp.cumsum(has.astype(jnp.int32)) - 1) % 2
    later = lax.cummin(jnp.where(has, experts, N_EXPERTS)[::-1])[::-1]
    next_e = jnp.concatenate([later[1:], jnp.full((1,), N_EXPERTS, jnp.int32)])
    next_e = jnp.where(next_e < N_EXPERTS, next_e, -1)
    n_active = pend[-1] // MOE_TM
    tail = jnp.concatenate([jnp.where(has, pend - MOE_TM, -1), n_active[None]]).astype(jnp.int32)
    blocks = (block_exp, block_first, active.astype(jnp.int32), next_e[block_exp].astype(jnp.int32),
              slot_e[block_exp].astype(jnp.int32),
              jnp.minimum(jnp.arange(n_blocks, dtype=jnp.int32), n_active - 1).astype(jnp.int32))
    xs = _dispatch(dest, tail, x, n_blocks * MOE_TM)
    y_rows = _experts(blocks, xs, w_gate_up, b_gate_up, w_down, b_down, layer)
    return _combine(dest, x, gates, ln_g, ln_b, y_rows)


def _split3(v):
    hi = v.astype(BF16)
    r1 = v - hi.astype(F32)
    mid = r1.astype(BF16)
    lo = (r1 - mid.astype(F32)).astype(BF16)
    return hi, mid, lo


def _shared_proj_kernel(x_ref, w_ref, wfh_ref, wfl_ref, bf_ref, eq_ref, ek_ref, oq_ref, ok_ref,
                        kq_ref, v_ref, qq_ref, qm_ref, carry_ref):
    @pl.when(pl.program_id(1) == 0)
    def _():
        carry_ref[...] = jnp.zeros_like(carry_ref)

    tm = x_ref.shape[0]
    x = x_ref[...]
    xh = x.astype(BF16)
    xl = (x - xh.astype(F32)).astype(BF16)
    h = jnp.dot(xh, w_ref[...], preferred_element_type=F32)
    f = (jnp.dot(xh, wfh_ref[...], preferred_element_type=F32)
         + jnp.dot(xl, wfh_ref[...], preferred_element_type=F32)
         + jnp.dot(xh, wfl_ref[...], preferred_element_type=F32)) + bf_ref[...]
    ls = jnp.minimum(f, 0.0) - jnp.log1p(jnp.exp(-jnp.abs(f)))
    ls = jnp.where(_head_mask(LANES, 0, N_FOX_HEADS), ls, 0.0)
    row = lax.broadcasted_iota(jnp.int32, (tm, tm), 0)
    col = lax.broadcasted_iota(jnp.int32, (tm, tm), 1)
    tri = jnp.where(row >= col, 1.0, 0.0).astype(BF16)
    a, b, c = _split3(ls)
    cum = (jnp.dot(tri, a, preferred_element_type=F32) + jnp.dot(tri, b, preferred_element_type=F32)
           + jnp.dot(tri, c, preferred_element_type=F32)) + carry_ref[...]
    carry_ref[...] = cum[tm - 1:tm, :]
    ch, cm, cl = _split3(cum)
    cat = (ch.astype(F32) + pltpu.roll(cm.astype(F32), 16, 1) + pltpu.roll(cl.astype(F32), 32, 1)).astype(BF16)
    kq = h[:, :FOX_QK_W] + jnp.dot(cat, ek_ref[...], preferred_element_type=F32) + ok_ref[...]
    qq = (h[:, FOX_QK_W + FOX_W:2 * FOX_QK_W + FOX_W]
          + jnp.dot(cat, eq_ref[...], preferred_element_type=F32) + oq_ref[...])
    kq_ref[...] = kq.astype(BF16)
    v_ref[...] = h[:, FOX_QK_W:FOX_QK_W + FOX_W].astype(BF16)
    qq_ref[...] = qq.astype(BF16)
    qm_ref[...] = h[:, 2 * FOX_QK_W + FOX_W:].astype(BF16)


def _widen_heads(w):
    d = w.shape[0]
    w3 = w.reshape(d, N_FOX_HEADS, HEAD_DIM)
    return jnp.concatenate([w3, jnp.zeros_like(w3)], axis=-1).reshape(d, FOX_QK_W)


def _spread_matrices():
    src = jnp.arange(LANES)[:, None]
    dst = jnp.arange(FOX_QK_W)[None, :]
    head = dst // FOX_HEAD_W
    off = dst % FOX_HEAD_W
    part = src // 16
    is_src = (src % 16 == head) & (src % 16 < N_FOX_HEADS) & (part < 3)
    eq = jnp.where(is_src & (off == HEAD_DIM + part), 1.0, 0.0).astype(BF16)
    ek = jnp.where(is_src & (off == HEAD_DIM + 3 + part), -1.0, 0.0).astype(BF16)
    off1 = jnp.arange(FOX_QK_W) % FOX_HEAD_W
    ones_q = ((off1 >= HEAD_DIM + 3) & (off1 < HEAD_DIM + 6)).astype(F32).reshape(1, FOX_QK_W)
    ones_k = ((off1 >= HEAD_DIM) & (off1 < HEAD_DIM + 3)).astype(F32).reshape(1, FOX_QK_W)
    return eq, ek, ones_q, ones_k


def _shared_proj(x, batch, w_shared_kvf, b_forget, w_in_b, tm=512):
    nt, d = x.shape
    per_b = nt // batch // tm
    w_k = _widen_heads(w_shared_kvf[:, :FOX_W])
    w_v = w_shared_kvf[:, FOX_W:2 * FOX_W]
    w_q = _widen_heads(w_in_b[:, :FOX_W] * ATTN_SCALE)
    w_qm = w_in_b[:, FOX_W:] * ATTN_SCALE
    w_big = jnp.concatenate([w_k, w_v, w_q, w_qm], axis=1).astype(BF16)
    w_f = jnp.pad(w_shared_kvf[:, 2 * FOX_W:], ((0, 0), (0, LANES - N_FOX_HEADS)))
    w_fh = w_f.astype(BF16)
    w_fl = (w_f - w_fh.astype(F32)).astype(BF16)
    b_f = jnp.pad(b_forget, (0, LANES - N_FOX_HEADS)).reshape(1, LANES)
    eq, ek, ones_q, ones_k = _spread_matrices()
    nbig = w_big.shape[1]
    row = lambda bi, i: (bi * per_b + i, 0)
    const = lambda bi, i: (0, 0)
    return pl.pallas_call(
        _shared_proj_kernel,
        grid=(batch, per_b),
        in_specs=[pl.BlockSpec((tm, d), row),
                  pl.BlockSpec((d, nbig), const),
                  pl.BlockSpec((d, LANES), const), pl.BlockSpec((d, LANES), const),
                  pl.BlockSpec((1, LANES), const),
                  pl.BlockSpec((LANES, FOX_QK_W), const), pl.BlockSpec((LANES, FOX_QK_W), const),
                  pl.BlockSpec((1, FOX_QK_W), const), pl.BlockSpec((1, FOX_QK_W), const)],
        out_specs=[pl.BlockSpec((tm, FOX_QK_W), row), pl.BlockSpec((tm, FOX_W), row),
                   pl.BlockSpec((tm, FOX_QK_W), row), pl.BlockSpec((tm, MEM_W), row)],
        out_shape=[jax.ShapeDtypeStruct((nt, FOX_QK_W), BF16), jax.ShapeDtypeStruct((nt, FOX_W), BF16),
                   jax.ShapeDtypeStruct((nt, FOX_QK_W), BF16), jax.ShapeDtypeStruct((nt, MEM_W), BF16)],
        scratch_shapes=[pltpu.VMEM((1, LANES), F32)],
        compiler_params=_cparams(("arbitrary", "arbitrary"), VMEM_LIMIT),
        name="shared_proj",
    )(x, w_big, w_fh, w_fl, b_f, eq, ek, ones_q, ones_k)


def _fox_kernel(q_ref, k_ref, v_ref, o_ref):
    tq = q_ref.shape[0]
    i = pl.program_id(2)
    row = lax.broadcasted_iota(jnp.int32, (tq, tq), 0)
    col = lax.broadcasted_iota(jnp.int32, (tq, tq), 1)

    def attend(kv):
        out = jnp.zeros((tq, LANES), F32)
        for hh in range(2):
            qh = q_ref[:, hh * FOX_HEAD_W:(hh + 1) * FOX_HEAD_W]
            kh = k_ref[:kv, hh * FOX_HEAD_W:(hh + 1) * FOX_HEAD_W]
            sc = lax.dot_general(qh, kh, (((1,), (1,)), ((), ())), preferred_element_type=F32)
            diag = jnp.where(col <= row, sc[:, kv - tq:], -jnp.inf)
            sc = diag if kv == tq else jnp.concatenate([sc[:, :kv - tq], diag], axis=1)
            m = jnp.max(sc, axis=-1, keepdims=True)
            p = jnp.exp(sc - m)
            den = jnp.sum(p, axis=-1, keepdims=True)
            acc = jnp.dot(p.astype(BF16), v_ref[:kv, :], preferred_element_type=F32)
            out = jnp.where(_head_mask(LANES, hh * HEAD_DIM, (hh + 1) * HEAD_DIM), acc / den, out)
        o_ref[...] = out.astype(BF16)

    for c in range(k_ref.shape[0] // tq):
        pl.when(i == c)(functools.partial(attend, (c + 1) * tq))


def _fox_attn(qq, kq, v, batch, tq=512):
    nt = qq.shape[0]
    s = nt // batch
    per_b = s // tq
    pairs = N_FOX_HEADS // 2
    return pl.pallas_call(
        _fox_kernel,
        grid=(batch, pairs, per_b),
        in_specs=[pl.BlockSpec((tq, 2 * FOX_HEAD_W), lambda bi, p, i: (bi * per_b + i, p)),
                  pl.BlockSpec((s, 2 * FOX_HEAD_W), lambda bi, p, i: (bi, p)),
                  pl.BlockSpec((s, LANES), lambda bi, p, i: (bi, p))],
        out_specs=pl.BlockSpec((tq, LANES), lambda bi, p, i: (bi * per_b + i, p)),
        out_shape=jax.ShapeDtypeStruct((nt, FOX_W), BF16),
        compiler_params=_cparams(("parallel", "parallel", "parallel")),
        name="fox_attn",
    )(qq, kq, v)


def _rope_tables(seq):
    inv = 1.0 / (ROPE_THETA ** (jnp.arange(0, HEAD_DIM, 2, dtype=F32) / HEAD_DIM))
    ang = jnp.arange(seq, dtype=F32)[:, None] * inv[None, :]
    return jnp.tile(jnp.cos(ang), (1, DIL_HEADS)), jnp.tile(jnp.sin(ang), (1, DIL_HEADS))


def _rotary_layout(w):
    d = w.shape[0]
    w4 = w.reshape(d, DIL_HEADS, 2, HEAD_DIM // 2)
    return w4.transpose(0, 2, 1, 3).reshape(d, GROUP_W)


def _group_weights(w_in, g, with_mem):
    base = g * 3 * GROUP_W
    cols = [_rotary_layout(w_in[:, base:base + GROUP_W] * ATTN_SCALE),
            _rotary_layout(w_in[:, base + GROUP_W:base + 2 * GROUP_W]),
            w_in[:, base + 2 * GROUP_W:base + 3 * GROUP_W]]
    if with_mem:
        cols.append(w_in[:, DIL_QKV_W:] * ATTN_SCALE)
    return jnp.concatenate(cols, axis=1).astype(BF16)


def kernel(x, mem, w_in_a, w_out_a, w_in_b, w_out_b, w_shared_kvf, b_forget, w_mem_kv, ln_mix_g, ln_mix_b,
           ln_ffn_g, ln_ffn_b, w_router, b_router, w_gate_up, b_gate_up, w_down, b_down):
    b, s, d = x.shape
    nt = b * s
    mem2 = mem.reshape(b * mem.shape[1], d)
    cos_t, sin_t = _rope_tables(s)
    moe = functools.partial(_moe_layer, w_router=w_router, b_router=b_router, w_gate_up=w_gate_up,
                            b_gate_up=b_gate_up, w_down=w_down, b_down=b_down)

    outs, lses, q_mem = [], [], None
    for g, (_, dil) in enumerate(DILATED_PATTERNS):
        res = _group_proj(x, _group_weights(w_in_a[0], g, g == 0), cos_t, sin_t, dil)
        if g == 0:
            q_mem = res[3].reshape(nt, MEM_W)
        o, lse = _dilated_attn(res[0], res[1], res[2], dil)
        outs.append(o)
        lses.append(lse)
    mkv0 = _matmul(mem2, w_mem_kv[0].astype(BF16), BF16, 512)
    memo = _mem_attn(q_mem, mkv0, b)
    x2 = x.reshape(nt, d)
    x2 = _out_proj(_out_proj_a_kernel, outs + lses + [memo], x2, w_out_a[0].astype(BF16),
                   ln_mix_g[0], ln_mix_b[0], "out_proj_a")
    x2 = moe(x2, 0, ln_g=ln_ffn_g[0], ln_b=ln_ffn_b[0])

    kq, v_sh, qq, q_mem = _shared_proj(x2, b, w_shared_kvf, b_forget, w_in_b[0])

    fox = _fox_attn(qq, kq, v_sh, b)
    mkv1 = _matmul(mem2, w_mem_kv[1].astype(BF16), BF16, 512)
    memo = _mem_attn(q_mem, mkv1, b)
    x2 = _out_proj(_out_proj_b_kernel, [fox, memo], x2, w_out_b[0].astype(BF16),
                   ln_mix_g[1], ln_mix_b[1], "out_proj_b")
    x2 = moe(x2, 1, ln_g=ln_ffn_g[1], ln_b=ln_ffn_b[1])
    return x2.reshape(b, s, d)
```

```python
import functools

import jax
import jax.numpy as jnp
from jax import lax
from jax.experimental import pallas as pl
from jax.experimental.pallas import tpu as pltpu

F32 = jnp.float32
BF16 = jnp.bfloat16

D_MODEL = 1024
DEPTH = 2
HEAD_DIM = 64
N_MEM_HEADS = 4
DILATED_PATTERNS = ((128, 1), (512, 4), (2048, 16))
N_DIL_GROUPS = 3
DIL_HEADS = 4
N_FOX_HEADS = 12
GROUP_W = DIL_HEADS * HEAD_DIM
MEM_W = N_MEM_HEADS * HEAD_DIM
FOX_W = N_FOX_HEADS * HEAD_DIM
DIL_QKV_W = N_DIL_GROUPS * 3 * GROUP_W
N_EXPERTS = 32
TOP_K = 4
D_FF = D_MODEL
SWIGLU_LIMIT = 7.0
SWIGLU_ALPHA = 1.702
ROPE_THETA = 10000.0
LN_EPS = 1e-5
ATTN_SCALE = HEAD_DIM ** -0.5
DEEPNORM_ALPHA = (2.0 * DEPTH) ** 0.25
DIL_STEPS = 128

LANES = 128
FOX_HEAD_W = LANES
FOX_QK_W = N_FOX_HEADS * FOX_HEAD_W
MOE_TM = 256
VMEM_LIMIT = 56 * 1024 * 1024


def _cparams(sem, vmem=None):
    return pltpu.CompilerParams(dimension_semantics=sem, vmem_limit_bytes=vmem)


def _head_mask(width, lo, hi):
    lane = lax.broadcasted_iota(jnp.int32, (1, width), 1)
    return (lane >= lo) & (lane < hi)


def _layer_norm_rows(y, g, b):
    mu = jnp.mean(y, axis=-1, keepdims=True)
    yc = y - mu
    var = jnp.mean(yc * yc, axis=-1, keepdims=True)
    return yc * lax.rsqrt(var + LN_EPS) * g + b


def _matmul_kernel(x_ref, w_ref, o_ref):
    o_ref[...] = jnp.dot(x_ref[...].astype(BF16), w_ref[...],
                         preferred_element_type=F32).astype(o_ref.dtype)


def _matmul(x, w, out_dtype, tm):
    m, k = x.shape
    n = w.shape[1]
    return pl.pallas_call(
        _matmul_kernel,
        grid=(m // tm,),
        in_specs=[pl.BlockSpec((tm, k), lambda i: (i, 0)),
                  pl.BlockSpec((k, n), lambda i: (0, 0))],
        out_specs=pl.BlockSpec((tm, n), lambda i: (i, 0)),
        out_shape=jax.ShapeDtypeStruct((m, n), out_dtype),
        compiler_params=_cparams(("parallel",)),
        name="matmul",
    )(x, w)


def _group_proj_kernel(x_ref, w_ref, cos_ref, sin_ref, *o_refs):
    n_res = o_refs[0].shape[1]
    d = x_ref.shape[2] // n_res
    half = GROUP_W // 2
    for r in range(n_res):
        h = jnp.dot(x_ref[0, :, r * d:(r + 1) * d].astype(BF16), w_ref[...], preferred_element_type=F32)
        cos = cos_ref[:, r * LANES:(r + 1) * LANES]
        sin = sin_ref[:, r * LANES:(r + 1) * LANES]
        for j in range(2):
            t1 = h[:, j * GROUP_W:j * GROUP_W + half]
            t2 = h[:, j * GROUP_W + half:(j + 1) * GROUP_W]
            o_refs[j][0, r, :, :half] = (t1 * cos - t2 * sin).astype(BF16)
            o_refs[j][0, r, :, half:] = (t2 * cos + t1 * sin).astype(BF16)
        for j in range(2, len(o_refs)):
            o_refs[j][0, r] = h[:, j * GROUP_W:(j + 1) * GROUP_W].astype(BF16)


PROJ_ROWS = 512


def _group_proj(x, w, cos_t, sin_t, dil):
    b, s, d = x.shape
    length = s // dil
    lt = min(length, PROJ_ROWS)
    n_res = min(PROJ_ROWS // lt, dil)
    n_out = w.shape[1] // GROUP_W
    xv = x.reshape(b, length, dil * d)
    cv = cos_t.reshape(length, dil * LANES)
    sv = sin_t.reshape(length, dil * LANES)
    o_spec = pl.BlockSpec((1, n_res, lt, GROUP_W), lambda bi, r, l: (bi, r, l, 0))
    return pl.pallas_call(
        _group_proj_kernel,
        grid=(b, dil // n_res, length // lt),
        in_specs=[pl.BlockSpec((1, lt, n_res * d), lambda bi, r, l: (bi, l, r)),
                  pl.BlockSpec(w.shape, lambda bi, r, l: (0, 0)),
                  pl.BlockSpec((lt, n_res * LANES), lambda bi, r, l: (l, r)),
                  pl.BlockSpec((lt, n_res * LANES), lambda bi, r, l: (l, r))],
        out_specs=[o_spec] * n_out,
        out_shape=[jax.ShapeDtypeStruct((b, dil, length, GROUP_W), BF16)] * n_out,
        compiler_params=_cparams(("parallel", "parallel", "parallel")),
        name=f"group_proj_d{dil}",
    )(xv, w, cv, sv)


def _dilated_attn_kernel(q_ref, kp_ref, kc_ref, vp_ref, vc_ref, o_ref, lse_ref):
    n = pl.program_id(2)
    n_res = q_ref.shape[1]
    n_sub = q_ref.shape[2] // DIL_STEPS
    qi = lax.broadcasted_iota(jnp.int32, (DIL_STEPS, 2 * DIL_STEPS), 0)
    kj = lax.broadcasted_iota(jnp.int32, (DIL_STEPS, 2 * DIL_STEPS), 1)
    band = (kj >= qi) & (kj <= qi + DIL_STEPS)
    half = GROUP_W // 2
    hw = HEAD_DIM // 2
    for r in range(n_res):
        for j in range(n_sub):
            lo, hi = j * DIL_STEPS, (j + 1) * DIL_STEPS
            q = q_ref[0, r, lo:hi, :]
            if j == 0:
                kk = jnp.concatenate([kp_ref[0, r], kc_ref[0, r, lo:hi, :]], axis=0)
                vv = jnp.concatenate([vp_ref[0, r], vc_ref[0, r, lo:hi, :]], axis=0)
                valid = band & ((n > 0) | (kj >= DIL_STEPS))
            else:
                kk = kc_ref[0, r, lo - DIL_STEPS:hi, :]
                vv = vc_ref[0, r, lo - DIL_STEPS:hi, :]
                valid = band
            qs = []
            for h in range(DIL_HEADS):
                qmask = (_head_mask(GROUP_W, h * hw, (h + 1) * hw)
                         | _head_mask(GROUP_W, half + h * hw, half + (h + 1) * hw))
                qs.append(jnp.where(qmask, q, jnp.zeros_like(q)))
            sc = lax.dot_general(jnp.concatenate(qs, axis=0), kk, (((1,), (1,)), ((), ())),
                                 preferred_element_type=F32)
            sc = jnp.where(jnp.concatenate([valid] * DIL_HEADS, axis=0), sc, -jnp.inf)
            m = jnp.max(sc, axis=-1, keepdims=True)
            p = jnp.exp(sc - m)
            den = jnp.sum(p, axis=-1, keepdims=True)
            o_all = jnp.dot(p.astype(BF16), vv, preferred_element_type=F32) / den
            lse_all = m + jnp.log(den)
            out = jnp.zeros((DIL_STEPS, GROUP_W), F32)
            lse = jnp.zeros((DIL_STEPS, GROUP_W), F32)
            for h in range(DIL_HEADS):
                vmask = _head_mask(GROUP_W, h * HEAD_DIM, (h + 1) * HEAD_DIM)
                out = jnp.where(vmask, o_all[h * DIL_STEPS:(h + 1) * DIL_STEPS], out)
                lse = jnp.where(vmask, lse_all[h * DIL_STEPS:(h + 1) * DIL_STEPS], lse)
            o_ref[0, lo:hi, r * GROUP_W:(r + 1) * GROUP_W] = out.astype(BF16)
            lse_ref[0, lo:hi, r * GROUP_W:(r + 1) * GROUP_W] = lse


DIL_UNITS = 4


def _dilated_attn(q, k, v, dil):
    b, _, length, _ = q.shape
    n_sub = min(length // DIL_STEPS, DIL_UNITS)
    n_res = DIL_UNITS // n_sub
    rows = n_sub * DIL_STEPS
    cur = pl.BlockSpec((1, n_res, rows, GROUP_W), lambda bi, r, n: (bi, r, n, 0))
    prev = pl.BlockSpec((1, n_res, DIL_STEPS, GROUP_W),
                        lambda bi, r, n: (bi, r, jnp.maximum(n * n_sub - 1, 0), 0))
    out = pl.BlockSpec((1, rows, n_res * GROUP_W), lambda bi, r, n: (bi, n, r))
    o, lse = pl.pallas_call(
        _dilated_attn_kernel,
        grid=(b, dil // n_res, length // rows),
        in_specs=[cur, prev, cur, prev, cur],
        out_specs=[out, out],
        out_shape=[jax.ShapeDtypeStruct((b, length, dil * GROUP_W), BF16),
                   jax.ShapeDtypeStruct((b, length, dil * GROUP_W), F32)],
        compiler_params=_cparams(("parallel", "parallel", "parallel")),
        name=f"dilated_attn_d{dil}",
    )(q, k, k, v, v)
    return o.reshape(b * length * dil, GROUP_W), lse.reshape(b * length * dil, GROUP_W)


def _mem_attn_kernel(q_ref, kv_ref, o_ref):
    q = q_ref[...]
    mk = kv_ref[:, :MEM_W]
    mv = kv_ref[:, MEM_W:]
    out = jnp.zeros(q.shape, F32)
    for h in range(N_MEM_HEADS):
        hmask = _head_mask(MEM_W, h * HEAD_DIM, (h + 1) * HEAD_DIM)
        qm = jnp.where(hmask, q, jnp.zeros_like(q))
        sc = lax.dot_general(qm, mk, (((1,), (1,)), ((), ())), preferred_element_type=F32)
        m = jnp.max(sc, axis=-1, keepdims=True)
        p = jnp.exp(sc - m)
        den = jnp.sum(p, axis=-1, keepdims=True)
        o_all = jnp.dot(p.astype(BF16), mv, preferred_element_type=F32)
        out = jnp.where(hmask, o_all / den, out)
    o_ref[...] = out.astype(BF16)


def _mem_attn(q, mkv, batch, tq=512):
    nt = q.shape[0]
    per_b = nt // batch // tq
    m_len = mkv.shape[0] // batch
    return pl.pallas_call(
        _mem_attn_kernel,
        grid=(batch, per_b),
        in_specs=[pl.BlockSpec((tq, MEM_W), lambda bi, i: (bi * per_b + i, 0)),
                  pl.BlockSpec((m_len, 2 * MEM_W), lambda bi, i: (bi, 0))],
        out_specs=pl.BlockSpec((tq, MEM_W), lambda bi, i: (bi * per_b + i, 0)),
        out_shape=jax.ShapeDtypeStruct((nt, MEM_W), BF16),
        compiler_params=_cparams(("parallel", "parallel")),
        name="mem_attn",
    )(q, mkv)


def _out_proj_a_kernel(o0, o1, o2, l0, l1, l2, memo, x_ref, w_ref, g_ref, b_ref, out_ref):
    la, lb, lc = l0[...], l1[...], l2[...]
    mx = jnp.maximum(jnp.maximum(la, lb), lc)
    ea, eb, ec = jnp.exp(la - mx), jnp.exp(lb - mx), jnp.exp(lc - mx)
    z = ea + eb + ec
    dil = (ea * o0[...].astype(F32) + eb * o1[...].astype(F32) + ec * o2[...].astype(F32)) / z
    mix = jnp.dot(dil.astype(BF16), w_ref[:GROUP_W, :], preferred_element_type=F32)
    mix += jnp.dot(memo[...], w_ref[GROUP_W:, :], preferred_element_type=F32)
    y = DEEPNORM_ALPHA * x_ref[...] + mix
    out_ref[...] = _layer_norm_rows(y, g_ref[...], b_ref[...])


def _out_proj_b_kernel(fox, memo, x_ref, w_ref, g_ref, b_ref, out_ref):
    mix = jnp.dot(fox[...], w_ref[:FOX_W, :], preferred_element_type=F32)
    mix += jnp.dot(memo[...], w_ref[FOX_W:, :], preferred_element_type=F32)
    y = DEEPNORM_ALPHA * x_ref[...] + mix
    out_ref[...] = _layer_norm_rows(y, g_ref[...], b_ref[...])


def _out_proj(kernel_fn, acts, x, w, g, b, name, tm=512):
    nt, d = x.shape
    row = lambda i: (i, 0)
    const = lambda i: (0, 0)
    return pl.pallas_call(
        kernel_fn,
        grid=(nt // tm,),
        in_specs=[pl.BlockSpec((tm, a.shape[1]), row) for a in acts]
        + [pl.BlockSpec((tm, d), row), pl.BlockSpec(w.shape, const),
           pl.BlockSpec((1, d), const), pl.BlockSpec((1, d), const)],
        out_specs=pl.BlockSpec((tm, d), row),
        out_shape=jax.ShapeDtypeStruct((nt, d), F32),
        compiler_params=_cparams(("parallel",)),
        name=name,
    )(*acts, x, w, g.reshape(1, d), b.reshape(1, d))


def _router_kernel(x_ref, wt_ref, b_ref, idx_ref, rank_ref, gate_ref, cnt_ref, carry_ref):
    @pl.when(pl.program_id(0) == 0)
    def _():
        carry_ref[...] = jnp.zeros_like(carry_ref)

    tm = x_ref.shape[0]
    logits = lax.dot_general(wt_ref[...], x_ref[...], (((1,), (1,)), ((), ())),
                             precision=lax.Precision.HIGHEST, preferred_element_type=F32) + b_ref[...]
    sub = lax.broadcasted_iota(jnp.int32, (N_EXPERTS, tm), 0).astype(F32)
    work = logits
    vals, sels, idxs = [], [], []
    for _ in range(TOP_K):
        mk = jnp.max(work, axis=0, keepdims=True)
        ik = jnp.min(jnp.where(work == mk, sub, float(N_EXPERTS)), axis=0, keepdims=True)
        sel = sub == ik
        work = jnp.where(sel, -jnp.inf, work)
        vals.append(mk)
        sels.append(sel)
        idxs.append(ik)
    es = [jnp.exp(v - vals[0]) for v in vals]
    z = es[0] + es[1] + es[2] + es[3]
    hot = jnp.zeros((N_EXPERTS, tm), F32)
    for sel in sels:
        hot = jnp.where(sel, 1.0, hot)
    row = lax.broadcasted_iota(jnp.int32, (tm, tm), 0)
    col = lax.broadcasted_iota(jnp.int32, (tm, tm), 1)
    tri = jnp.where(row < col, 1.0, 0.0).astype(BF16)
    before = jnp.dot(hot.astype(BF16), tri, preferred_element_type=F32) + carry_ref[...]
    k_sub = lax.broadcasted_iota(jnp.int32, (TOP_K, tm), 0)
    g_sub = lax.broadcasted_iota(jnp.int32, (LANES, tm), 0)
    idx_o = jnp.zeros((TOP_K, tm), F32)
    rank_o = jnp.zeros((TOP_K, tm), F32)
    gate_t = jnp.zeros((LANES, tm), F32)
    for k in range(TOP_K):
        rk = jnp.sum(jnp.where(sels[k], before, 0.0), axis=0, keepdims=True)
        idx_o = jnp.where(k_sub == k, idxs[k], idx_o)
        rank_o = jnp.where(k_sub == k, rk, rank_o)
        gate_t = jnp.where(g_sub == k, es[k] / z, gate_t)
    idx_ref[...] = idx_o.astype(jnp.int32)
    rank_ref[...] = rank_o.astype(jnp.int32)
    gate_ref[...] = gate_t.T
    carry_ref[...] += jnp.sum(hot, axis=1, keepdims=True)
    cnt_ref[...] = carry_ref[...].astype(jnp.int32)


def _router(x, w_r, b_r, tm=512):
    nt, d = x.shape
    const = lambda i: (0, 0)
    return pl.pallas_call(
        _router_kernel,
        grid=(nt // tm,),
        in_specs=[pl.BlockSpec((tm, d), lambda i: (i, 0)), pl.BlockSpec((N_EXPERTS, d), const),
                  pl.BlockSpec((N_EXPERTS, 1), const)],
        out_specs=[pl.BlockSpec((TOP_K, tm), lambda i: (0, i)), pl.BlockSpec((TOP_K, tm), lambda i: (0, i)),
                   pl.BlockSpec((tm, LANES), lambda i: (i, 0)), pl.BlockSpec((N_EXPERTS, 1), const)],
        out_shape=[jax.ShapeDtypeStruct((TOP_K, nt), jnp.int32),
                   jax.ShapeDtypeStruct((TOP_K, nt), jnp.int32),
                   jax.ShapeDtypeStruct((nt, LANES), F32),
                   jax.ShapeDtypeStruct((N_EXPERTS, 1), jnp.int32)],
        scratch_shapes=[pltpu.VMEM((N_EXPERTS, 1), F32)],
        compiler_params=_cparams(("arbitrary",)),
        name="router",
    )(x, w_r.T, b_r.reshape(N_EXPERTS, 1))


def _row_copy(src, dst, sem):
    return pltpu.make_async_copy(src, dst, sem)


ROW_ISSUE_UNROLL = 4


def _dispatch_kernel(dest_ref, tail_ref, x_ref, xs_ref, zeros_ref, sem, zsem):
    tm = x_ref.shape[0]
    nt = tm * pl.num_programs(0)
    base = pl.program_id(0) * tm

    def tail_copy(e):
        t = pl.multiple_of(jnp.maximum(tail_ref[e], 0), MOE_TM)
        return _row_copy(zeros_ref, xs_ref.at[pl.ds(t, MOE_TM), :], zsem)

    def unused_copy(j):
        return _row_copy(zeros_ref, xs_ref.at[pl.ds(pl.multiple_of(j * MOE_TM, MOE_TM), MOE_TM), :], zsem)

    @pl.when(pl.program_id(0) == 0)
    def _():
        zeros_ref[...] = jnp.zeros_like(zeros_ref)
        n_blocks = xs_ref.shape[0] // MOE_TM
        first_unused = tail_ref[N_EXPERTS]
        for e in range(N_EXPERTS):
            pl.when(tail_ref[e] >= 0)(lambda e=e: tail_copy(e).start())
        lax.fori_loop(first_unused, n_blocks, lambda j, c: (unused_copy(j).start(), c)[1], 0)
        for e in range(N_EXPERTS):
            pl.when(tail_ref[e] >= 0)(lambda e=e: tail_copy(e).wait())
        lax.fori_loop(first_unused, n_blocks, lambda j, c: (unused_copy(j).wait(), c)[1], 0)

    def issue(i, carry):
        for k in range(TOP_K):
            d = dest_ref[k * nt + base + i]
            _row_copy(x_ref.at[pl.ds(i, 1), :], xs_ref.at[pl.ds(d, 1), :], sem).start(priority=k % 2)
        return carry

    lax.fori_loop(0, tm, issue, 0, unroll=ROW_ISSUE_UNROLL)
    for _ in range(TOP_K):
        _row_copy(x_ref, xs_ref.at[pl.ds(0, tm), :], sem).wait()


def _dispatch(dest_flat, tail, x, rows, tm=256):
    nt, d = x.shape
    return pl.pallas_call(
        _dispatch_kernel,
        grid_spec=pltpu.PrefetchScalarGridSpec(
            num_scalar_prefetch=2,
            grid=(nt // tm,),
            in_specs=[pl.BlockSpec((tm, d), lambda i, dest, tail: (i, 0))],
            out_specs=pl.BlockSpec(memory_space=pl.ANY),
            scratch_shapes=[pltpu.VMEM((MOE_TM, d), x.dtype),
                            pltpu.SemaphoreType.DMA(()), pltpu.SemaphoreType.DMA(())],
        ),
        out_shape=jax.ShapeDtypeStruct((rows, d), x.dtype),
        compiler_params=_cparams(("arbitrary",)),
        name="moe_dispatch",
    )(dest_flat, tail, x)


def _expert_kernel(exp_ref, first_ref, active_ref, next_ref, slot_ref, xsrc_ref,
                   xs_ref, bgu_ref, bdn_ref, wgu_hbm, wdn_hbm,
                   y_ref, wgu_f32, wdn_f32, wgu_bf, wdn_bf, sem, *, layer):
    del xsrc_ref
    i = pl.program_id(0)

    def weight_copies(e, s):
        return (pltpu.make_async_copy(wgu_hbm.at[layer, e], wgu_f32.at[s], sem.at[0, s]),
                pltpu.make_async_copy(wdn_hbm.at[layer, e], wdn_f32.at[s], sem.at[1, s]))

    @pl.when(i == 0)
    def _():
        for c in weight_copies(exp_ref[0], slot_ref[0]):
            c.start()

    @pl.when(first_ref[i] == 1)
    def _():
        s = slot_ref[i]
        for c in weight_copies(exp_ref[i], s):
            c.wait()

        @pl.when(next_ref[i] >= 0)
        def _():
            for c in weight_copies(next_ref[i], 1 - s):
                c.start()

        wgu_bf[...] = wgu_f32[s].astype(BF16)
        wdn_bf[...] = wdn_f32[s].astype(BF16)

    @pl.when(active_ref[i] == 1)
    def _():
        hb = jnp.dot(xs_ref[...].astype(BF16), wgu_bf[...], preferred_element_type=F32) + bgu_ref[...]
        gate = jnp.minimum(hb[:, :D_FF], SWIGLU_LIMIT)
        up = jnp.clip(hb[:, D_FF:], -SWIGLU_LIMIT, SWIGLU_LIMIT)
        act = (up + 1.0) * gate * jax.nn.sigmoid(SWIGLU_ALPHA * gate)
        y_ref[...] = jnp.dot(act.astype(BF16), wdn_bf[...], preferred_element_type=F32) + bdn_ref[...]

    @pl.when(active_ref[i] == 0)
    def _():
        y_ref[...] = jnp.zeros_like(y_ref)


def _experts(blocks, xs, w_gu, b_gu, w_dn, b_dn, layer):
    rows = xs.shape[0]
    d = w_dn.shape[-1]
    n_blocks = rows // MOE_TM
    return pl.pallas_call(
        functools.partial(_expert_kernel, layer=layer),
        grid_spec=pltpu.PrefetchScalarGridSpec(
            num_scalar_prefetch=6,
            grid=(n_blocks,),
            in_specs=[
                pl.BlockSpec((MOE_TM, d), lambda i, e, f, a, n, s, x: (x[i], 0)),
                pl.BlockSpec((None, None, 1, 2 * D_FF), lambda i, e, f, a, n, s, x: (layer, e[i], 0, 0)),
                pl.BlockSpec((None, None, 1, d), lambda i, e, f, a, n, s, x: (layer, e[i], 0, 0)),
                pl.BlockSpec(memory_space=pl.ANY),
                pl.BlockSpec(memory_space=pl.ANY),
            ],
            out_specs=pl.BlockSpec((MOE_TM, d), lambda i, e, f, a, n, s, x: (i, 0)),
            scratch_shapes=[pltpu.VMEM((2, d, 2 * D_FF), F32), pltpu.VMEM((2, D_FF, d), F32),
                            pltpu.VMEM((d, 2 * D_FF), BF16), pltpu.VMEM((D_FF, d), BF16),
                            pltpu.SemaphoreType.DMA((2, 2))],
        ),
        out_shape=jax.ShapeDtypeStruct((rows, d), F32),
        compiler_params=_cparams(("arbitrary",), VMEM_LIMIT),
        name="moe_experts",
    )(*blocks, xs, b_gu.reshape(DEPTH, N_EXPERTS, 1, 2 * D_FF), b_dn.reshape(DEPTH, N_EXPERTS, 1, d),
      w_gu, w_dn)


def _combine_kernel(dest_ref, x_ref, gate_ref, g_ref, b_ref, y_ref, out_ref, buf, sem):
    tm = x_ref.shape[0]
    step = pl.program_id(0)
    n_steps = pl.num_programs(0)
    nt = tm * n_steps
    slot = step % 2

    def gather_rows(s, to_slot):
        def issue(i, carry):
            for k in range(TOP_K):
                d = dest_ref[k * nt + s * tm + i]
                _row_copy(y_ref.at[pl.ds(d, 1), :], buf.at[to_slot, k, pl.ds(i, 1), :],
                          sem.at[to_slot]).start(priority=k % 2)
            return carry

        lax.fori_loop(0, tm, issue, 0, unroll=ROW_ISSUE_UNROLL)

    def consume(cur):
        pl.when(step + 1 < n_steps)(lambda: gather_rows(step + 1, 1 - cur))
        for k in range(TOP_K):
            _row_copy(y_ref.at[pl.ds(0, tm), :], buf.at[cur, k], sem.at[cur]).wait()
        gates = gate_ref[...]
        ffn = gates[:, 0:1] * buf[cur, 0]
        for k in range(1, TOP_K):
            ffn += gates[:, k:k + 1] * buf[cur, k]
        y = DEEPNORM_ALPHA * x_ref[...] + ffn
        out_ref[...] = _layer_norm_rows(y, g_ref[...], b_ref[...])

    pl.when(step == 0)(lambda: gather_rows(0, 0))
    for cur in range(2):
        pl.when(slot == cur)(functools.partial(consume, cur))


def _combine(dest_flat, x, gates, g, b, y_rows, tm=256):
    nt, d = x.shape
    return pl.pallas_call(
        _combine_kernel,
        grid_spec=pltpu.PrefetchScalarGridSpec(
            num_scalar_prefetch=1,
            grid=(nt // tm,),
            in_specs=[pl.BlockSpec((tm, d), lambda i, dest: (i, 0)),
                      pl.BlockSpec((tm, LANES), lambda i, dest: (i, 0)),
                      pl.BlockSpec((1, d), lambda i, dest: (0, 0)),
                      pl.BlockSpec((1, d), lambda i, dest: (0, 0)),
                      pl.BlockSpec(memory_space=pl.ANY)],
            out_specs=pl.BlockSpec((tm, d), lambda i, dest: (i, 0)),
            scratch_shapes=[pltpu.VMEM((2, TOP_K, tm, d), F32), pltpu.SemaphoreType.DMA((2,))],
        ),
        out_shape=jax.ShapeDtypeStruct((nt, d), F32),
        compiler_params=_cparams(("arbitrary",), VMEM_LIMIT),
        name="moe_combine",
    )(dest_flat, x, gates, g.reshape(1, d), b.reshape(1, d), y_rows)


def _moe_layer(x, layer, w_router, b_router, w_gate_up, b_gate_up, w_down, b_down, ln_g, ln_b):
    nt, _ = x.shape
    idx, rank, gates, counts = _router(x, w_router[layer], b_router[layer])
    counts = counts[:, 0]
    padded = (counts + MOE_TM - 1) // MOE_TM * MOE_TM
    pend = jnp.cumsum(padded)
    pstart = pend - padded
    experts = jnp.arange(N_EXPERTS, dtype=jnp.int32)
    start_of = jnp.sum(jnp.where(idx[..., None] == experts, pstart, 0), axis=-1)
    dest = (start_of + rank).reshape(TOP_K * nt).astype(jnp.int32)
    n_blocks = nt * TOP_K // MOE_TM + N_EXPERTS
    blk_start = jnp.arange(n_blocks, dtype=jnp.int32) * MOE_TM
    active = blk_start < pend[-1]
    exp_raw = jnp.minimum(jnp.sum(pend[None, :] <= blk_start[:, None], axis=1), N_EXPERTS - 1).astype(jnp.int32)
    last_exp = jnp.max(jnp.where(active, exp_raw, 0))
    block_exp = jnp.where(active, exp_raw, last_exp).astype(jnp.int32)
    prev_exp = jnp.concatenate([jnp.full((1,), -1, jnp.int32), block_exp[:-1]])
    block_first = (active & (block_exp != prev_exp)).astype(jnp.int32)
    has = padded > 0
    slot_e = (jnp.cumsum(has.astype(jnp.int32)) - 1) % 2
    later = lax.cummin(jnp.where(has, experts, N_EXPERTS)[::-1])[::-1]
    next_e = jnp.concatenate([later[1:], jnp.full((1,), N_EXPERTS, jnp.int32)])
    next_e = jnp.where(next_e < N_EXPERTS, next_e, -1)
    n_active = pend[-1] // MOE_TM
    tail = jnp.concatenate([jnp.where(has, pend - MOE_TM, -1), n_active[None]]).astype(jnp.int32)
    blocks = (block_exp, block_first, active.astype(jnp.int32), next_e[block_exp].astype(jnp.int32),
              slot_e[block_exp].astype(jnp.int32),
              jnp.minimum(jnp.arange(n_blocks, dtype=jnp.int32), n_active - 1).astype(jnp.int32))
    xs = _dispatch(dest, tail, x, n_blocks * MOE_TM)
    y_rows = _experts(blocks, xs, w_gate_up, b_gate_up, w_down, b_down, layer)
    return _combine(dest, x, gates, ln_g, ln_b, y_rows)


def _split3(v):
    hi = v.astype(BF16)
    r1 = v - hi.astype(F32)
    mid = r1.astype(BF16)
    lo = (r1 - mid.astype(F32)).astype(BF16)
    return hi, mid, lo


def _shared_proj_kernel(x_ref, w_ref, wfh_ref, wfl_ref, bf_ref, eq_ref, ek_ref, oq_ref, ok_ref,
                        kq_ref, v_ref, qq_ref, qm_ref, carry_ref):
    @pl.when(pl.program_id(1) == 0)
    def _():
        carry_ref[...] = jnp.zeros_like(carry_ref)

    tm = x_ref.shape[0]
    x = x_ref[...]
    xh = x.astype(BF16)
    xl = (x - xh.astype(F32)).astype(BF16)
    h = jnp.dot(xh, w_ref[...], preferred_element_type=F32)
    f = (jnp.dot(xh, wfh_ref[...], preferred_element_type=F32)
         + jnp.dot(xl, wfh_ref[...], preferred_element_type=F32)
         + jnp.dot(xh, wfl_ref[...], preferred_element_type=F32)) + bf_ref[...]
    ls = jnp.minimum(f, 0.0) - jnp.log1p(jnp.exp(-jnp.abs(f)))
    ls = jnp.where(_head_mask(LANES, 0, N_FOX_HEADS), ls, 0.0)
    row = lax.broadcasted_iota(jnp.int32, (tm, tm), 0)
    col = lax.broadcasted_iota(jnp.int32, (tm, tm), 1)
    tri = jnp.where(row >= col, 1.0, 0.0).astype(BF16)
    a, b, c = _split3(ls)
    cum = (jnp.dot(tri, a, preferred_element_type=F32) + jnp.dot(tri, b, preferred_element_type=F32)
           + jnp.dot(tri, c, preferred_element_type=F32)) + carry_ref[...]
    carry_ref[...] = cum[tm - 1:tm, :]
    ch, cm, cl = _split3(cum)
    cat = (ch.astype(F32) + pltpu.roll(cm.astype(F32), 16, 1) + pltpu.roll(cl.astype(F32), 32, 1)).astype(BF16)
    kq = h[:, :FOX_QK_W] + jnp.dot(cat, ek_ref[...], preferred_element_type=F32) + ok_ref[...]
    qq = (h[:, FOX_QK_W + FOX_W:2 * FOX_QK_W + FOX_W]
          + jnp.dot(cat, eq_ref[...], preferred_element_type=F32) + oq_ref[...])
    kq_ref[...] = kq.astype(BF16)
    v_ref[...] = h[:, FOX_QK_W:FOX_QK_W + FOX_W].astype(BF16)
    qq_ref[...] = qq.astype(BF16)
    qm_ref[...] = h[:, 2 * FOX_QK_W + FOX_W:].astype(BF16)


def _widen_heads(w):
    d = w.shape[0]
    w3 = w.reshape(d, N_FOX_HEADS, HEAD_DIM)
    return jnp.concatenate([w3, jnp.zeros_like(w3)], axis=-1).reshape(d, FOX_QK_W)


def _spread_matrices():
    src = jnp.arange(LANES)[:, None]
    dst = jnp.arange(FOX_QK_W)[None, :]
    head = dst // FOX_HEAD_W
    off = dst % FOX_HEAD_W
    part = src // 16
    is_src = (src % 16 == head) & (src % 16 < N_FOX_HEADS) & (part < 3)
    eq = jnp.where(is_src & (off == HEAD_DIM + part), 1.0, 0.0).astype(BF16)
    ek = jnp.where(is_src & (off == HEAD_DIM + 3 + part), -1.0, 0.0).astype(BF16)
    off1 = jnp.arange(FOX_QK_W) % FOX_HEAD_W
    ones_q = ((off1 >= HEAD_DIM + 3) & (off1 < HEAD_DIM + 6)).astype(F32).reshape(1, FOX_QK_W)
    ones_k = ((off1 >= HEAD_DIM) & (off1 < HEAD_DIM + 3)).astype(F32).reshape(1, FOX_QK_W)
    return eq, ek, ones_q, ones_k


def _shared_proj(x, batch, w_shared_kvf, b_forget, w_in_b, tm=512):
    nt, d = x.shape
    per_b = nt // batch // tm
    w_k = _widen_heads(w_shared_kvf[:, :FOX_W])
    w_v = w_shared_kvf[:, FOX_W:2 * FOX_W]
    w_q = _widen_heads(w_in_b[:, :FOX_W] * ATTN_SCALE)
    w_qm = w_in_b[:, FOX_W:] * ATTN_SCALE
    w_big = jnp.concatenate([w_k, w_v, w_q, w_qm], axis=1).astype(BF16)
    w_f = jnp.pad(w_shared_kvf[:, 2 * FOX_W:], ((0, 0), (0, LANES - N_FOX_HEADS)))
    w_fh = w_f.astype(BF16)
    w_fl = (w_f - w_fh.astype(F32)).astype(BF16)
    b_f = jnp.pad(b_forget, (0, LANES - N_FOX_HEADS)).reshape(1, LANES)
    eq, ek, ones_q, ones_k = _spread_matrices()
    nbig = w_big.shape[1]
    row = lambda bi, i: (bi * per_b + i, 0)
    const = lambda bi, i: (0, 0)
    return pl.pallas_call(
        _shared_proj_kernel,
        grid=(batch, per_b),
        in_specs=[pl.BlockSpec((tm, d), row),
                  pl.BlockSpec((d, nbig), const),
                  pl.BlockSpec((d, LANES), const), pl.BlockSpec((d, LANES), const),
                  pl.BlockSpec((1, LANES), const),
                  pl.BlockSpec((LANES, FOX_QK_W), const), pl.BlockSpec((LANES, FOX_QK_W), const),
                  pl.BlockSpec((1, FOX_QK_W), const), pl.BlockSpec((1, FOX_QK_W), const)],
        out_specs=[pl.BlockSpec((tm, FOX_QK_W), row), pl.BlockSpec((tm, FOX_W), row),
                   pl.BlockSpec((tm, FOX_QK_W), row), pl.BlockSpec((tm, MEM_W), row)],
        out_shape=[jax.ShapeDtypeStruct((nt, FOX_QK_W), BF16), jax.ShapeDtypeStruct((nt, FOX_W), BF16),
                   jax.ShapeDtypeStruct((nt, FOX_QK_W), BF16), jax.ShapeDtypeStruct((nt, MEM_W), BF16)],
        scratch_shapes=[pltpu.VMEM((1, LANES), F32)],
        compiler_params=_cparams(("arbitrary", "arbitrary"), VMEM_LIMIT),
        name="shared_proj",
    )(x, w_big, w_fh, w_fl, b_f, eq, ek, ones_q, ones_k)


def _fox_kernel(q_ref, k_ref, v_ref, o_ref):
    tq = q_ref.shape[0]
    i = pl.program_id(2)
    row = lax.broadcasted_iota(jnp.int32, (tq, tq), 0)
    col = lax.broadcasted_iota(jnp.int32, (tq, tq), 1)

    def attend(kv):
        out = jnp.zeros((tq, LANES), F32)
        for hh in range(2):
            qh = q_ref[:, hh * FOX_HEAD_W:(hh + 1) * FOX_HEAD_W]
            kh = k_ref[:kv, hh * FOX_HEAD_W:(hh + 1) * FOX_HEAD_W]
            sc = lax.dot_general(qh, kh, (((1,), (1,)), ((), ())), preferred_element_type=F32)
            diag = jnp.where(col <= row, sc[:, kv - tq:], -jnp.inf)
            sc = diag if kv == tq else jnp.concatenate([sc[:, :kv - tq], diag], axis=1)
            m = jnp.max(sc, axis=-1, keepdims=True)
            p = jnp.exp(sc - m)
            den = jnp.sum(p, axis=-1, keepdims=True)
            acc = jnp.dot(p.astype(BF16), v_ref[:kv, :], preferred_element_type=F32)
            out = jnp.where(_head_mask(LANES, hh * HEAD_DIM, (hh + 1) * HEAD_DIM), acc / den, out)
        o_ref[...] = out.astype(BF16)

    for c in range(k_ref.shape[0] // tq):
        pl.when(i == c)(functools.partial(attend, (c + 1) * tq))


def _fox_attn(qq, kq, v, batch, tq=512):
    nt = qq.shape[0]
    s = nt // batch
    per_b = s // tq
    pairs = N_FOX_HEADS // 2
    return pl.pallas_call(
        _fox_kernel,
        grid=(batch, pairs, per_b),
        in_specs=[pl.BlockSpec((tq, 2 * FOX_HEAD_W), lambda bi, p, i: (bi * per_b + i, p)),
                  pl.BlockSpec((s, 2 * FOX_HEAD_W), lambda bi, p, i: (bi, p)),
                  pl.BlockSpec((s, LANES), lambda bi, p, i: (bi, p))],
        out_specs=pl.BlockSpec((tq, LANES), lambda bi, p, i: (bi * per_b + i, p)),
        out_shape=jax.ShapeDtypeStruct((nt, FOX_W), BF16),
        compiler_params=_cparams(("parallel", "parallel", "parallel")),
        name="fox_attn",
    )(qq, kq, v)


def _rope_tables(seq):
    inv = 1.0 / (ROPE_THETA ** (jnp.arange(0, HEAD_DIM, 2, dtype=F32) / HEAD_DIM))
    ang = jnp.arange(seq, dtype=F32)[:, None] * inv[None, :]
    return jnp.tile(jnp.cos(ang), (1, DIL_HEADS)), jnp.tile(jnp.sin(ang), (1, DIL_HEADS))


def _rotary_layout(w):
    d = w.shape[0]
    w4 = w.reshape(d, DIL_HEADS, 2, HEAD_DIM // 2)
    return w4.transpose(0, 2, 1, 3).reshape(d, GROUP_W)


def _group_weights(w_in, g, with_mem):
    base = g * 3 * GROUP_W
    cols = [_rotary_layout(w_in[:, base:base + GROUP_W] * ATTN_SCALE),
            _rotary_layout(w_in[:, base + GROUP_W:base + 2 * GROUP_W]),
            w_in[:, base + 2 * GROUP_W:base + 3 * GROUP_W]]
    if with_mem:
        cols.append(w_in[:, DIL_QKV_W:] * ATTN_SCALE)
    return jnp.concatenate(cols, axis=1).astype(BF16)


def kernel(x, mem, w_in_a, w_out_a, w_in_b, w_out_b, w_shared_kvf, b_forget, w_mem_kv, ln_mix_g, ln_mix_b,
           ln_ffn_g, ln_ffn_b, w_router, b_router, w_gate_up, b_gate_up, w_down, b_down):
    b, s, d = x.shape
    nt = b * s
    mem2 = mem.reshape(b * mem.shape[1], d)
    cos_t, sin_t = _rope_tables(s)
    moe = functools.partial(_moe_layer, w_router=w_router, b_router=b_router, w_gate_up=w_gate_up,
                            b_gate_up=b_gate_up, w_down=w_down, b_down=b_down)

    outs, lses, q_mem = [], [], None
    for g, (_, dil) in enumerate(DILATED_PATTERNS):
        res = _group_proj(x, _group_weights(w_in_a[0], g, g == 0), cos_t, sin_t, dil)
        if g == 0:
            q_mem = res[3].reshape(nt, MEM_W)
        o, lse = _dilated_attn(res[0], res[1], res[2], dil)
        outs.append(o)
        lses.append(lse)
    mkv0 = _matmul(mem2, w_mem_kv[0].astype(BF16), BF16, 512)
    memo = _mem_attn(q_mem, mkv0, b)
    x2 = x.reshape(nt, d)
    x2 = _out_proj(_out_proj_a_kernel, outs + lses + [memo], x2, w_out_a[0].astype(BF16),
                   ln_mix_g[0], ln_mix_b[0], "out_proj_a")
    x2 = moe(x2, 0, ln_g=ln_ffn_g[0], ln_b=ln_ffn_b[0])

    kq, v_sh, qq, q_mem = _shared_proj(x2, b, w_shared_kvf, b_forget, w_in_b[0])

    fox = _fox_attn(qq, kq, v_sh, b)
    mkv1 = _matmul(mem2, w_mem_kv[1].astype(BF16), BF16, 512)
    memo = _mem_attn(q_mem, mkv1, b)
    x2 = _out_proj(_out_proj_b_kernel, [fox, memo], x2, w_out_b[0].astype(BF16),
                   ln_mix_g[1], ln_mix_b[1], "out_proj_b")
    x2 = moe(x2, 1, ln_g=ln_ffn_g[1], ln_b=ln_ffn_b[1])
    return x2.reshape(b, s, d)
```

```python
import functools

import jax
import jax.numpy as jnp
from jax import lax
from jax.experimental import pallas as pl
from jax.experimental.pallas import tpu as pltpu

F32 = jnp.float32
BF16 = jnp.bfloat16

D_MODEL = 1024
DEPTH = 2
HEAD_DIM = 64
N_MEM_HEADS = 4
DILATED_PATTERNS = ((128, 1), (512, 4), (2048, 16))
N_DIL_GROUPS = 3
DIL_HEADS = 4
N_FOX_HEADS = 12
GROUP_W = DIL_HEADS * HEAD_DIM
MEM_W = N_MEM_HEADS * HEAD_DIM
FOX_W = N_FOX_HEADS * HEAD_DIM
DIL_QKV_W = N_DIL_GROUPS * 3 * GROUP_W
N_EXPERTS = 32
TOP_K = 4
D_FF = D_MODEL
SWIGLU_LIMIT = 7.0
SWIGLU_ALPHA = 1.702
ROPE_THETA = 10000.0
LN_EPS = 1e-5
ATTN_SCALE = HEAD_DIM ** -0.5
DEEPNORM_ALPHA = (2.0 * DEPTH) ** 0.25
DIL_STEPS = 128

LANES = 128
FOX_HEAD_W = LANES
FOX_QK_W = N_FOX_HEADS * FOX_HEAD_W
MOE_TM = 256
VMEM_LIMIT = 56 * 1024 * 1024


def _cparams(sem, vmem=None):
    return pltpu.CompilerParams(dimension_semantics=sem, vmem_limit_bytes=vmem)


def _head_mask(width, lo, hi):
    lane = lax.broadcasted_iota(jnp.int32, (1, width), 1)
    return (lane >= lo) & (lane < hi)


def _layer_norm_rows(y, g, b):
    mu = jnp.mean(y, axis=-1, keepdims=True)
    yc = y - mu
    var = jnp.mean(yc * yc, axis=-1, keepdims=True)
    return yc * lax.rsqrt(var + LN_EPS) * g + b


def _matmul_kernel(x_ref, w_ref, o_ref):
    o_ref[...] = jnp.dot(x_ref[...].astype(BF16), w_ref[...],
                         preferred_element_type=F32).astype(o_ref.dtype)


def _matmul(x, w, out_dtype, tm):
    m, k = x.shape
    n = w.shape[1]
    return pl.pallas_call(
        _matmul_kernel,
        grid=(m // tm,),
        in_specs=[pl.BlockSpec((tm, k), lambda i: (i, 0)),
                  pl.BlockSpec((k, n), lambda i: (0, 0))],
        out_specs=pl.BlockSpec((tm, n), lambda i: (i, 0)),
        out_shape=jax.ShapeDtypeStruct((m, n), out_dtype),
        compiler_params=_cparams(("parallel",)),
        name="matmul",
    )(x, w)


def _group_proj_kernel(x_ref, w_ref, cos_ref, sin_ref, *o_refs):
    n_res = o_refs[0].shape[1]
    d = x_ref.shape[2] // n_res
    half = GROUP_W // 2
    for r in range(n_res):
        h = jnp.dot(x_ref[0, :, r * d:(r + 1) * d].astype(BF16), w_ref[...], preferred_element_type=F32)
        cos = cos_ref[:, r * LANES:(r + 1) * LANES]
        sin = sin_ref[:, r * LANES:(r + 1) * LANES]
        for j in range(2):
            t1 = h[:, j * GROUP_W:j * GROUP_W + half]
            t2 = h[:, j * GROUP_W + half:(j + 1) * GROUP_W]
            o_refs[j][0, r, :, :half] = (t1 * cos - t2 * sin).astype(BF16)
            o_refs[j][0, r, :, half:] = (t2 * cos + t1 * sin).astype(BF16)
        for j in range(2, len(o_refs)):
            o_refs[j][0, r] = h[:, j * GROUP_W:(j + 1) * GROUP_W].astype(BF16)


PROJ_ROWS = 512


def _group_proj(x, w, cos_t, sin_t, dil):
    b, s, d = x.shape
    length = s // dil
    lt = min(length, PROJ_ROWS)
    n_res = min(PROJ_ROWS // lt, dil)
    n_out = w.shape[1] // GROUP_W
    xv = x.reshape(b, length, dil * d)
    cv = cos_t.reshape(length, dil * LANES)
    sv = sin_t.reshape(length, dil * LANES)
    o_spec = pl.BlockSpec((1, n_res, lt, GROUP_W), lambda bi, r, l: (bi, r, l, 0))
    return pl.pallas_call(
        _group_proj_kernel,
        grid=(b, dil // n_res, length // lt),
        in_specs=[pl.BlockSpec((1, lt, n_res * d), lambda bi, r, l: (bi, l, r)),
                  pl.BlockSpec(w.shape, lambda bi, r, l: (0, 0)),
                  pl.BlockSpec((lt, n_res * LANES), lambda bi, r, l: (l, r)),
                  pl.BlockSpec((lt, n_res * LANES), lambda bi, r, l: (l, r))],
        out_specs=[o_spec] * n_out,
        out_shape=[jax.ShapeDtypeStruct((b, dil, length, GROUP_W), BF16)] * n_out,
        compiler_params=_cparams(("parallel", "parallel", "parallel")),
        name=f"group_proj_d{dil}",
    )(xv, w, cv, sv)


def _dilated_attn_kernel(q_ref, kp_ref, kc_ref, vp_ref, vc_ref, o_ref, lse_ref):
    n = pl.program_id(2)
    n_res = q_ref.shape[1]
    n_sub = q_ref.shape[2] // DIL_STEPS
    qi = lax.broadcasted_iota(jnp.int32, (DIL_STEPS, 2 * DIL_STEPS), 0)
    kj = lax.broadcasted_iota(jnp.int32, (DIL_STEPS, 2 * DIL_STEPS), 1)
    band = (kj >= qi) & (kj <= qi + DIL_STEPS)
    half = GROUP_W // 2
    hw = HEAD_DIM // 2
    for r in range(n_res):
        for j in range(n_sub):
            lo, hi = j * DIL_STEPS, (j + 1) * DIL_STEPS
            q = q_ref[0, r, lo:hi, :]
            if j == 0:
                kk = jnp.concatenate([kp_ref[0, r], kc_ref[0, r, lo:hi, :]], axis=0)
                vv = jnp.concatenate([vp_ref[0, r], vc_ref[0, r, lo:hi, :]], axis=0)
                valid = band & ((n > 0) | (kj >= DIL_STEPS))
            else:
                kk = kc_ref[0, r, lo - DIL_STEPS:hi, :]
                vv = vc_ref[0, r, lo - DIL_STEPS:hi, :]
                valid = band
            qs = []
            for h in range(DIL_HEADS):
                qmask = (_head_mask(GROUP_W, h * hw, (h + 1) * hw)
                         | _head_mask(GROUP_W, half + h * hw, half + (h + 1) * hw))
                qs.append(jnp.where(qmask, q, jnp.zeros_like(q)))
            sc = lax.dot_general(jnp.concatenate(qs, axis=0), kk, (((1,), (1,)), ((), ())),
                                 preferred_element_type=F32)
            sc = jnp.where(jnp.concatenate([valid] * DIL_HEADS, axis=0), sc, -jnp.inf)
            m = jnp.max(sc, axis=-1, keepdims=True)
            p = jnp.exp(sc - m)
            den = jnp.sum(p, axis=-1, keepdims=True)
            o_all = jnp.dot(p.astype(BF16), vv, preferred_element_type=F32) / den
            lse_all = m + jnp.log(den)
            out = jnp.zeros((DIL_STEPS, GROUP_W), F32)
            lse = jnp.zeros((DIL_STEPS, GROUP_W), F32)
            for h in range(DIL_HEADS):
                vmask = _head_mask(GROUP_W, h * HEAD_DIM, (h + 1) * HEAD_DIM)
                out = jnp.where(vmask, o_all[h * DIL_STEPS:(h + 1) * DIL_STEPS], out)
                lse = jnp.where(vmask, lse_all[h * DIL_STEPS:(h + 1) * DIL_STEPS], lse)
            o_ref[0, lo:hi, r * GROUP_W:(r + 1) * GROUP_W] = out.astype(BF16)
            lse_ref[0, lo:hi, r * GROUP_W:(r + 1) * GROUP_W] = lse


DIL_UNITS = 4


def _dilated_attn(q, k, v, dil):
    b, _, length, _ = q.shape
    n_sub = min(length // DIL_STEPS, DIL_UNITS)
    n_res = DIL_UNITS // n_sub
    rows = n_sub * DIL_STEPS
    cur = pl.BlockSpec((1, n_res, rows, GROUP_W), lambda bi, r, n: (bi, r, n, 0))
    prev = pl.BlockSpec((1, n_res, DIL_STEPS, GROUP_W),
                        lambda bi, r, n: (bi, r, jnp.maximum(n * n_sub - 1, 0), 0))
    out = pl.BlockSpec((1, rows, n_res * GROUP_W), lambda bi, r, n: (bi, n, r))
    o, lse = pl.pallas_call(
        _dilated_attn_kernel,
        grid=(b, dil // n_res, length // rows),
        in_specs=[cur, prev, cur, prev, cur],
        out_specs=[out, out],
        out_shape=[jax.ShapeDtypeStruct((b, length, dil * GROUP_W), BF16),
                   jax.ShapeDtypeStruct((b, length, dil * GROUP_W), F32)],
        compiler_params=_cparams(("parallel", "parallel", "parallel")),
        name=f"dilated_attn_d{dil}",
    )(q, k, k, v, v)
    return o.reshape(b * length * dil, GROUP_W), lse.reshape(b * length * dil, GROUP_W)


def _mem_attn_kernel(q_ref, kv_ref, o_ref):
    q = q_ref[...]
    mk = kv_ref[:, :MEM_W]
    mv = kv_ref[:, MEM_W:]
    tq = q.shape[0]
    hmasks = [_head_mask(MEM_W, h * HEAD_DIM, (h + 1) * HEAD_DIM) for h in range(N_MEM_HEADS)]
    qs = jnp.concatenate([jnp.where(hm, q, jnp.zeros_like(q)) for hm in hmasks], axis=0)
    sc = lax.dot_general(qs, mk, (((1,), (1,)), ((), ())), preferred_element_type=F32)
    m = jnp.max(sc, axis=-1, keepdims=True)
    p = jnp.exp(sc - m)
    den = jnp.sum(p, axis=-1, keepdims=True)
    o_all = jnp.dot(p.astype(BF16), mv, preferred_element_type=F32) / den
    out = jnp.zeros(q.shape, F32)
    for h, hm in enumerate(hmasks):
        out = jnp.where(hm, o_all[h * tq:(h + 1) * tq], out)
    o_ref[...] = out.astype(BF16)


def _mem_attn(q, mkv, batch, tq=512):
    nt = q.shape[0]
    per_b = nt // batch // tq
    m_len = mkv.shape[0] // batch
    return pl.pallas_call(
        _mem_attn_kernel,
        grid=(batch, per_b),
        in_specs=[pl.BlockSpec((tq, MEM_W), lambda bi, i: (bi * per_b + i, 0)),
                  pl.BlockSpec((m_len, 2 * MEM_W), lambda bi, i: (bi, 0))],
        out_specs=pl.BlockSpec((tq, MEM_W), lambda bi, i: (bi * per_b + i, 0)),
        out_shape=jax.ShapeDtypeStruct((nt, MEM_W), BF16),
        compiler_params=_cparams(("parallel", "parallel")),
        name="mem_attn",
    )(q, mkv)


def _out_proj_a_kernel(o0, o1, o2, l0, l1, l2, memo, x_ref, w_ref, g_ref, b_ref, out_ref):
    la, lb, lc = l0[...], l1[...], l2[...]
    mx = jnp.maximum(jnp.maximum(la, lb), lc)
    ea, eb, ec = jnp.exp(la - mx), jnp.exp(lb - mx), jnp.exp(lc - mx)
    z = ea + eb + ec
    dil = (ea * o0[...].astype(F32) + eb * o1[...].astype(F32) + ec * o2[...].astype(F32)) / z
    mix = jnp.dot(dil.astype(BF16), w_ref[:GROUP_W, :], preferred_element_type=F32)
    mix += jnp.dot(memo[...], w_ref[GROUP_W:, :], preferred_element_type=F32)
    y = DEEPNORM_ALPHA * x_ref[...] + mix
    out_ref[...] = _layer_norm_rows(y, g_ref[...], b_ref[...])


def _out_proj_b_kernel(fox, memo, x_ref, w_ref, g_ref, b_ref, out_ref):
    mix = jnp.dot(fox[...], w_ref[:FOX_W, :], preferred_element_type=F32)
    mix += jnp.dot(memo[...], w_ref[FOX_W:, :], preferred_element_type=F32)
    y = DEEPNORM_ALPHA * x_ref[...] + mix
    out_ref[...] = _layer_norm_rows(y, g_ref[...], b_ref[...])


def _out_proj(kernel_fn, acts, x, w, g, b, name, tm=512):
    nt, d = x.shape
    row = lambda i: (i, 0)
    const = lambda i: (0, 0)
    return pl.pallas_call(
        kernel_fn,
        grid=(nt // tm,),
        in_specs=[pl.BlockSpec((tm, a.shape[1]), row) for a in acts]
        + [pl.BlockSpec((tm, d), row), pl.BlockSpec(w.shape, const),
           pl.BlockSpec((1, d), const), pl.BlockSpec((1, d), const)],
        out_specs=pl.BlockSpec((tm, d), row),
        out_shape=jax.ShapeDtypeStruct((nt, d), F32),
        compiler_params=_cparams(("parallel",)),
        name=name,
    )(*acts, x, w, g.reshape(1, d), b.reshape(1, d))


def _router_kernel(x_ref, wt_ref, b_ref, idx_ref, rank_ref, gate_ref, cnt_ref, carry_ref):
    @pl.when(pl.program_id(0) == 0)
    def _():
        carry_ref[...] = jnp.zeros_like(carry_ref)

    tm = x_ref.shape[0]
    logits = lax.dot_general(wt_ref[...], x_ref[...], (((1,), (1,)), ((), ())),
                             precision=lax.Precision.HIGHEST, preferred_element_type=F32) + b_ref[...]
    sub = lax.broadcasted_iota(jnp.int32, (N_EXPERTS, tm), 0).astype(F32)
    work = logits
    vals, sels, idxs = [], [], []
    for _ in range(TOP_K):
        mk = jnp.max(work, axis=0, keepdims=True)
        ik = jnp.min(jnp.where(work == mk, sub, float(N_EXPERTS)), axis=0, keepdims=True)
        sel = sub == ik
        work = jnp.where(sel, -jnp.inf, work)
        vals.append(mk)
        sels.append(sel)
        idxs.append(ik)
    es = [jnp.exp(v - vals[0]) for v in vals]
    z = es[0] + es[1] + es[2] + es[3]
    hot = jnp.zeros((N_EXPERTS, tm), F32)
    for sel in sels:
        hot = jnp.where(sel, 1.0, hot)
    row = lax.broadcasted_iota(jnp.int32, (tm, tm), 0)
    col = lax.broadcasted_iota(jnp.int32, (tm, tm), 1)
    tri = jnp.where(row < col, 1.0, 0.0).astype(BF16)
    before = jnp.dot(hot.astype(BF16), tri, preferred_element_type=F32) + carry_ref[...]
    k_sub = lax.broadcasted_iota(jnp.int32, (TOP_K, tm), 0)
    g_sub = lax.broadcasted_iota(jnp.int32, (LANES, tm), 0)
    idx_o = jnp.zeros((TOP_K, tm), F32)
    rank_o = jnp.zeros((TOP_K, tm), F32)
    gate_t = jnp.zeros((LANES, tm), F32)
    for k in range(TOP_K):
        rk = jnp.sum(jnp.where(sels[k], before, 0.0), axis=0, keepdims=True)
        idx_o = jnp.where(k_sub == k, idxs[k], idx_o)
        rank_o = jnp.where(k_sub == k, rk, rank_o)
        gate_t = jnp.where(g_sub == k, es[k] / z, gate_t)
    idx_ref[...] = idx_o.astype(jnp.int32)
    rank_ref[...] = rank_o.astype(jnp.int32)
    gate_ref[...] = gate_t.T
    carry_ref[...] += jnp.sum(hot, axis=1, keepdims=True)
    cnt_ref[...] = carry_ref[...].astype(jnp.int32)


def _router(x, w_r, b_r, tm=512):
    nt, d = x.shape
    const = lambda i: (0, 0)
    return pl.pallas_call(
        _router_kernel,
        grid=(nt // tm,),
        in_specs=[pl.BlockSpec((tm, d), lambda i: (i, 0)), pl.BlockSpec((N_EXPERTS, d), const),
                  pl.BlockSpec((N_EXPERTS, 1), const)],
        out_specs=[pl.BlockSpec((TOP_K, tm), lambda i: (0, i)), pl.BlockSpec((TOP_K, tm), lambda i: (0, i)),
                   pl.BlockSpec((tm, LANES), lambda i: (i, 0)), pl.BlockSpec((N_EXPERTS, 1), const)],
        out_shape=[jax.ShapeDtypeStruct((TOP_K, nt), jnp.int32),
                   jax.ShapeDtypeStruct((TOP_K, nt), jnp.int32),
                   jax.ShapeDtypeStruct((nt, LANES), F32),
                   jax.ShapeDtypeStruct((N_EXPERTS, 1), jnp.int32)],
        scratch_shapes=[pltpu.VMEM((N_EXPERTS, 1), F32)],
        compiler_params=_cparams(("arbitrary",)),
        name="router",
    )(x, w_r.T, b_r.reshape(N_EXPERTS, 1))


def _row_copy(src, dst, sem):
    return pltpu.make_async_copy(src, dst, sem)


ROW_ISSUE_UNROLL = 4


def _dispatch_kernel(dest_ref, tail_ref, x_ref, xs_ref, zeros_ref, sem, zsem):
    tm = x_ref.shape[0]
    nt = tm * pl.num_programs(0)
    base = pl.program_id(0) * tm

    def tail_copy(e):
        t = pl.multiple_of(jnp.maximum(tail_ref[e], 0), MOE_TM)
        return _row_copy(zeros_ref, xs_ref.at[pl.ds(t, MOE_TM), :], zsem)

    def unused_copy(j):
        return _row_copy(zeros_ref, xs_ref.at[pl.ds(pl.multiple_of(j * MOE_TM, MOE_TM), MOE_TM), :], zsem)

    @pl.when(pl.program_id(0) == 0)
    def _():
        zeros_ref[...] = jnp.zeros_like(zeros_ref)
        n_blocks = xs_ref.shape[0] // MOE_TM
        first_unused = tail_ref[N_EXPERTS]
        for e in range(N_EXPERTS):
            pl.when(tail_ref[e] >= 0)(lambda e=e: tail_copy(e).start())
        lax.fori_loop(first_unused, n_blocks, lambda j, c: (unused_copy(j).start(), c)[1], 0)
        for e in range(N_EXPERTS):
            pl.when(tail_ref[e] >= 0)(lambda e=e: tail_copy(e).wait())
        lax.fori_loop(first_unused, n_blocks, lambda j, c: (unused_copy(j).wait(), c)[1], 0)

    def issue(i, carry):
        for k in range(TOP_K):
            d = dest_ref[k * nt + base + i]
            _row_copy(x_ref.at[pl.ds(i, 1), :], xs_ref.at[pl.ds(d, 1), :], sem).start(priority=k % 2)
        return carry

    lax.fori_loop(0, tm, issue, 0, unroll=ROW_ISSUE_UNROLL)
    for _ in range(TOP_K):
        _row_copy(x_ref, xs_ref.at[pl.ds(0, tm), :], sem).wait()


def _dispatch(dest_flat, tail, x, rows, tm=256):
    nt, d = x.shape
    return pl.pallas_call(
        _dispatch_kernel,
        grid_spec=pltpu.PrefetchScalarGridSpec(
            num_scalar_prefetch=2,
            grid=(nt // tm,),
            in_specs=[pl.BlockSpec((tm, d), lambda i, dest, tail: (i, 0))],
            out_specs=pl.BlockSpec(memory_space=pl.ANY),
            scratch_shapes=[pltpu.VMEM((MOE_TM, d), x.dtype),
                            pltpu.SemaphoreType.DMA(()), pltpu.SemaphoreType.DMA(())],
        ),
        out_shape=jax.ShapeDtypeStruct((rows, d), x.dtype),
        compiler_params=_cparams(("arbitrary",)),
        name="moe_dispatch",
    )(dest_flat, tail, x)


def _expert_kernel(exp_ref, first_ref, active_ref, next_ref, slot_ref, xsrc_ref,
                   xs_ref, bgu_ref, bdn_ref, wgu_hbm, wdn_hbm,
                   y_ref, wgu_f32, wdn_f32, wgu_bf, wdn_bf, sem, *, layer):
    del xsrc_ref
    i = pl.program_id(0)

    def weight_copies(e, s):
        return (pltpu.make_async_copy(wgu_hbm.at[layer, e], wgu_f32.at[s], sem.at[0, s]),
                pltpu.make_async_copy(wdn_hbm.at[layer, e], wdn_f32.at[s], sem.at[1, s]))

    @pl.when(i == 0)
    def _():
        for c in weight_copies(exp_ref[0], slot_ref[0]):
            c.start()

    @pl.when(first_ref[i] == 1)
    def _():
        s = slot_ref[i]
        for c in weight_copies(exp_ref[i], s):
            c.wait()

        @pl.when(next_ref[i] >= 0)
        def _():
            for c in weight_copies(next_ref[i], 1 - s):
                c.start()

        wgu_bf[...] = wgu_f32[s].astype(BF16)
        wdn_bf[...] = wdn_f32[s].astype(BF16)

    @pl.when(active_ref[i] == 1)
    def _():
        hb = jnp.dot(xs_ref[...].astype(BF16), wgu_bf[...], preferred_element_type=F32) + bgu_ref[...]
        gate = jnp.minimum(hb[:, :D_FF], SWIGLU_LIMIT)
        up = jnp.clip(hb[:, D_FF:], -SWIGLU_LIMIT, SWIGLU_LIMIT)
        act = (up + 1.0) * gate * jax.nn.sigmoid(SWIGLU_ALPHA * gate)
        y_ref[...] = jnp.dot(act.astype(BF16), wdn_bf[...], preferred_element_type=F32) + bdn_ref[...]

    @pl.when(active_ref[i] == 0)
    def _():
        y_ref[...] = jnp.zeros_like(y_ref)


def _experts(blocks, xs, w_gu, b_gu, w_dn, b_dn, layer):
    rows = xs.shape[0]
    d = w_dn.shape[-1]
    n_blocks = rows // MOE_TM
    return pl.pallas_call(
        functools.partial(_expert_kernel, layer=layer),
        grid_spec=pltpu.PrefetchScalarGridSpec(
            num_scalar_prefetch=6,
            grid=(n_blocks,),
            in_specs=[
                pl.BlockSpec((MOE_TM, d), lambda i, e, f, a, n, s, x: (x[i], 0)),
                pl.BlockSpec((None, None, 1, 2 * D_FF), lambda i, e, f, a, n, s, x: (layer, e[i], 0, 0)),
                pl.BlockSpec((None, None, 1, d), lambda i, e, f, a, n, s, x: (layer, e[i], 0, 0)),
                pl.BlockSpec(memory_space=pl.ANY),
                pl.BlockSpec(memory_space=pl.ANY),
            ],
            out_specs=pl.BlockSpec((MOE_TM, d), lambda i, e, f, a, n, s, x: (i, 0)),
            scratch_shapes=[pltpu.VMEM((2, d, 2 * D_FF), F32), pltpu.VMEM((2, D_FF, d), F32),
                            pltpu.VMEM((d, 2 * D_FF), BF16), pltpu.VMEM((D_FF, d), BF16),
                            pltpu.SemaphoreType.DMA((2, 2))],
        ),
        out_shape=jax.ShapeDtypeStruct((rows, d), F32),
        compiler_params=_cparams(("arbitrary",), VMEM_LIMIT),
        name="moe_experts",
    )(*blocks, xs, b_gu.reshape(DEPTH, N_EXPERTS, 1, 2 * D_FF), b_dn.reshape(DEPTH, N_EXPERTS, 1, d),
      w_gu, w_dn)


def _combine_kernel(dest_ref, x_ref, gate_ref, g_ref, b_ref, y_ref, out_ref, buf, sem):
    tm = x_ref.shape[0]
    step = pl.program_id(0)
    n_steps = pl.num_programs(0)
    nt = tm * n_steps
    slot = step % 2

    def gather_rows(s, to_slot):
        def issue(i, carry):
            for k in range(TOP_K):
                d = dest_ref[k * nt + s * tm + i]
                _row_copy(y_ref.at[pl.ds(d, 1), :], buf.at[to_slot, k, pl.ds(i, 1), :],
                          sem.at[to_slot]).start(priority=k % 2)
            return carry

        lax.fori_loop(0, tm, issue, 0, unroll=ROW_ISSUE_UNROLL)

    def consume(cur):
        pl.when(step + 1 < n_steps)(lambda: gather_rows(step + 1, 1 - cur))
        for k in range(TOP_K):
            _row_copy(y_ref.at[pl.ds(0, tm), :], buf.at[cur, k], sem.at[cur]).wait()
        gates = gate_ref[...]
        ffn = gates[:, 0:1] * buf[cur, 0]
        for k in range(1, TOP_K):
            ffn += gates[:, k:k + 1] * buf[cur, k]
        y = DEEPNORM_ALPHA * x_ref[...] + ffn
        out_ref[...] = _layer_norm_rows(y, g_ref[...], b_ref[...])

    pl.when(step == 0)(lambda: gather_rows(0, 0))
    for cur in range(2):
        pl.when(slot == cur)(functools.partial(consume, cur))


def _combine(dest_flat, x, gates, g, b, y_rows, tm=256):
    nt, d = x.shape
    return pl.pallas_call(
        _combine_kernel,
        grid_spec=pltpu.PrefetchScalarGridSpec(
            num_scalar_prefetch=1,
            grid=(nt // tm,),
            in_specs=[pl.BlockSpec((tm, d), lambda i, dest: (i, 0)),
                      pl.BlockSpec((tm, LANES), lambda i, dest: (i, 0)),
                      pl.BlockSpec((1, d), lambda i, dest: (0, 0)),
                      pl.BlockSpec((1, d), lambda i, dest: (0, 0)),
                      pl.BlockSpec(memory_space=pl.ANY)],
            out_specs=pl.BlockSpec((tm, d), lambda i, dest: (i, 0)),
            scratch_shapes=[pltpu.VMEM((2, TOP_K, tm, d), F32), pltpu.SemaphoreType.DMA((2,))],
        ),
        out_shape=jax.ShapeDtypeStruct((nt, d), F32),
        compiler_params=_cparams(("arbitrary",), VMEM_LIMIT),
        name="moe_combine",
    )(dest_flat, x, gates, g.reshape(1, d), b.reshape(1, d), y_rows)


def _moe_layer(x, layer, w_router, b_router, w_gate_up, b_gate_up, w_down, b_down, ln_g, ln_b):
    nt, _ = x.shape
    idx, rank, gates, counts = _router(x, w_router[layer], b_router[layer])
    counts = counts[:, 0]
    padded = (counts + MOE_TM - 1) // MOE_TM * MOE_TM
    pend = jnp.cumsum(padded)
    pstart = pend - padded
    experts = jnp.arange(N_EXPERTS, dtype=jnp.int32)
    start_of = jnp.sum(jnp.where(idx[..., None] == experts, pstart, 0), axis=-1)
    dest = (start_of + rank).reshape(TOP_K * nt).astype(jnp.int32)
    n_blocks = nt * TOP_K // MOE_TM + N_EXPERTS
    blk_start = jnp.arange(n_blocks, dtype=jnp.int32) * MOE_TM
    active = blk_start < pend[-1]
    exp_raw = jnp.minimum(jnp.sum(pend[None, :] <= blk_start[:, None], axis=1), N_EXPERTS - 1).astype(jnp.int32)
    last_exp = jnp.max(jnp.where(active, exp_raw, 0))
    block_exp = jnp.where(active, exp_raw, last_exp).astype(jnp.int32)
    prev_exp = jnp.concatenate([jnp.full((1,), -1, jnp.int32), block_exp[:-1]])
    block_first = (active & (block_exp != prev_exp)).astype(jnp.int32)
    has = padded > 0
    slot_e = (jnp.cumsum(has.astype(jnp.int32)) - 1) % 2
    later = lax.cummin(jnp.where(has, experts, N_EXPERTS)[::-1])[::-1]
    next_e = jnp.concatenate([later[1:], jnp.full((1,), N_EXPERTS, jnp.int32)])
    next_e = jnp.where(next_e < N_EXPERTS, next_e, -1)
    n_active = pend[-1] // MOE_TM
    tail = jnp.concatenate([jnp.where(has, pend - MOE_TM, -1), n_active[None]]).astype(jnp.int32)
    blocks = (block_exp, block_first, active.astype(jnp.int32), next_e[block_exp].astype(jnp.int32),
              slot_e[block_exp].astype(jnp.int32),
              jnp.minimum(jnp.arange(n_blocks, dtype=jnp.int32), n_active - 1).astype(jnp.int32))
    xs = _dispatch(dest, tail, x, n_blocks * MOE_TM)
    y_rows = _experts(blocks, xs, w_gate_up, b_gate_up, w_down, b_down, layer)
    return _combine(dest, x, gates, ln_g, ln_b, y_rows)


def _split3(v):
    hi = v.astype(BF16)
    r1 = v - hi.astype(F32)
    mid = r1.astype(BF16)
    lo = (r1 - mid.astype(F32)).astype(BF16)
    return hi, mid, lo


def _shared_proj_kernel(x_ref, w_ref, wfh_ref, wfl_ref, bf_ref, eq_ref, ek_ref, oq_ref, ok_ref,
                        kq_ref, v_ref, qq_ref, qm_ref, carry_ref):
    @pl.when(pl.program_id(1) == 0)
    def _():
        carry_ref[...] = jnp.zeros_like(carry_ref)

    tm = x_ref.shape[0]
    x = x_ref[...]
    xh = x.astype(BF16)
    xl = (x - xh.astype(F32)).astype(BF16)
    h = jnp.dot(xh, w_ref[...], preferred_element_type=F32)
    f = (jnp.dot(xh, wfh_ref[...], preferred_element_type=F32)
         + jnp.dot(xl, wfh_ref[...], preferred_element_type=F32)
         + jnp.dot(xh, wfl_ref[...], preferred_element_type=F32)) + bf_ref[...]
    ls = jnp.minimum(f, 0.0) - jnp.log1p(jnp.exp(-jnp.abs(f)))
    ls = jnp.where(_head_mask(LANES, 0, N_FOX_HEADS), ls, 0.0)
    row = lax.broadcasted_iota(jnp.int32, (tm, tm), 0)
    col = lax.broadcasted_iota(jnp.int32, (tm, tm), 1)
    tri = jnp.where(row >= col, 1.0, 0.0).astype(BF16)
    a, b, c = _split3(ls)
    cum = (jnp.dot(tri, a, preferred_element_type=F32) + jnp.dot(tri, b, preferred_element_type=F32)
           + jnp.dot(tri, c, preferred_element_type=F32)) + carry_ref[...]
    carry_ref[...] = cum[tm - 1:tm, :]
    ch, cm, cl = _split3(cum)
    cat = (ch.astype(F32) + pltpu.roll(cm.astype(F32), 16, 1) + pltpu.roll(cl.astype(F32), 32, 1)).astype(BF16)
    kq = h[:, :FOX_QK_W] + jnp.dot(cat, ek_ref[...], preferred_element_type=F32) + ok_ref[...]
    qq = (h[:, FOX_QK_W + FOX_W:2 * FOX_QK_W + FOX_W]
          + jnp.dot(cat, eq_ref[...], preferred_element_type=F32) + oq_ref[...])
    kq_ref[...] = kq.astype(BF16)
    v_ref[...] = h[:, FOX_QK_W:FOX_QK_W + FOX_W].astype(BF16)
    qq_ref[...] = qq.astype(BF16)
    qm_ref[...] = h[:, 2 * FOX_QK_W + FOX_W:].astype(BF16)


def _widen_heads(w):
    d = w.shape[0]
    w3 = w.reshape(d, N_FOX_HEADS, HEAD_DIM)
    return jnp.concatenate([w3, jnp.zeros_like(w3)], axis=-1).reshape(d, FOX_QK_W)


def _spread_matrices():
    src = jnp.arange(LANES)[:, None]
    dst = jnp.arange(FOX_QK_W)[None, :]
    head = dst // FOX_HEAD_W
    off = dst % FOX_HEAD_W
    part = src // 16
    is_src = (src % 16 == head) & (src % 16 < N_FOX_HEADS) & (part < 3)
    eq = jnp.where(is_src & (off == HEAD_DIM + part), 1.0, 0.0).astype(BF16)
    ek = jnp.where(is_src & (off == HEAD_DIM + 3 + part), -1.0, 0.0).astype(BF16)
    off1 = jnp.arange(FOX_QK_W) % FOX_HEAD_W
    ones_q = ((off1 >= HEAD_DIM + 3) & (off1 < HEAD_DIM + 6)).astype(F32).reshape(1, FOX_QK_W)
    ones_k = ((off1 >= HEAD_DIM) & (off1 < HEAD_DIM + 3)).astype(F32).reshape(1, FOX_QK_W)
    return eq, ek, ones_q, ones_k


def _shared_proj(x, batch, w_shared_kvf, b_forget, w_in_b, tm=512):
    nt, d = x.shape
    per_b = nt // batch // tm
    w_k = _widen_heads(w_shared_kvf[:, :FOX_W])
    w_v = w_shared_kvf[:, FOX_W:2 * FOX_W]
    w_q = _widen_heads(w_in_b[:, :FOX_W] * ATTN_SCALE)
    w_qm = w_in_b[:, FOX_W:] * ATTN_SCALE
    w_big = jnp.concatenate([w_k, w_v, w_q, w_qm], axis=1).astype(BF16)
    w_f = jnp.pad(w_shared_kvf[:, 2 * FOX_W:], ((0, 0), (0, LANES - N_FOX_HEADS)))
    w_fh = w_f.astype(BF16)
    w_fl = (w_f - w_fh.astype(F32)).astype(BF16)
    b_f = jnp.pad(b_forget, (0, LANES - N_FOX_HEADS)).reshape(1, LANES)
    eq, ek, ones_q, ones_k = _spread_matrices()
    nbig = w_big.shape[1]
    row = lambda bi, i: (bi * per_b + i, 0)
    const = lambda bi, i: (0, 0)
    return pl.pallas_call(
        _shared_proj_kernel,
        grid=(batch, per_b),
        in_specs=[pl.BlockSpec((tm, d), row),
                  pl.BlockSpec((d, nbig), const),
                  pl.BlockSpec((d, LANES), const), pl.BlockSpec((d, LANES), const),
                  pl.BlockSpec((1, LANES), const),
                  pl.BlockSpec((LANES, FOX_QK_W), const), pl.BlockSpec((LANES, FOX_QK_W), const),
                  pl.BlockSpec((1, FOX_QK_W), const), pl.BlockSpec((1, FOX_QK_W), const)],
        out_specs=[pl.BlockSpec((tm, FOX_QK_W), row), pl.BlockSpec((tm, FOX_W), row),
                   pl.BlockSpec((tm, FOX_QK_W), row), pl.BlockSpec((tm, MEM_W), row)],
        out_shape=[jax.ShapeDtypeStruct((nt, FOX_QK_W), BF16), jax.ShapeDtypeStruct((nt, FOX_W), BF16),
                   jax.ShapeDtypeStruct((nt, FOX_QK_W), BF16), jax.ShapeDtypeStruct((nt, MEM_W), BF16)],
        scratch_shapes=[pltpu.VMEM((1, LANES), F32)],
        compiler_params=_cparams(("arbitrary", "arbitrary"), VMEM_LIMIT),
        name="shared_proj",
    )(x, w_big, w_fh, w_fl, b_f, eq, ek, ones_q, ones_k)


def _fox_kernel(q_ref, k_ref, v_ref, o_ref):
    tq = q_ref.shape[0]
    i = pl.program_id(2)
    row = lax.broadcasted_iota(jnp.int32, (tq, tq), 0)
    col = lax.broadcasted_iota(jnp.int32, (tq, tq), 1)

    def attend(kv):
        out = jnp.zeros((tq, LANES), F32)
        for hh in range(2):
            qh = q_ref[:, hh * FOX_HEAD_W:(hh + 1) * FOX_HEAD_W]
            kh = k_ref[:kv, hh * FOX_HEAD_W:(hh + 1) * FOX_HEAD_W]
            sc = lax.dot_general(qh, kh, (((1,), (1,)), ((), ())), preferred_element_type=F32)
            diag = jnp.where(col <= row, sc[:, kv - tq:], -jnp.inf)
            sc = diag if kv == tq else jnp.concatenate([sc[:, :kv - tq], diag], axis=1)
            m = jnp.max(sc, axis=-1, keepdims=True)
            p = jnp.exp(sc - m)
            den = jnp.sum(p, axis=-1, keepdims=True)
            acc = jnp.dot(p.astype(BF16), v_ref[:kv, :], preferred_element_type=F32)
            out = jnp.where(_head_mask(LANES, hh * HEAD_DIM, (hh + 1) * HEAD_DIM), acc / den, out)
        o_ref[...] = out.astype(BF16)

    for c in range(k_ref.shape[0] // tq):
        pl.when(i == c)(functools.partial(attend, (c + 1) * tq))


def _fox_attn(qq, kq, v, batch, tq=512):
    nt = qq.shape[0]
    s = nt // batch
    per_b = s // tq
    pairs = N_FOX_HEADS // 2
    return pl.pallas_call(
        _fox_kernel,
        grid=(batch, pairs, per_b),
        in_specs=[pl.BlockSpec((tq, 2 * FOX_HEAD_W), lambda bi, p, i: (bi * per_b + i, p)),
                  pl.BlockSpec((s, 2 * FOX_HEAD_W), lambda bi, p, i: (bi, p)),
                  pl.BlockSpec((s, LANES), lambda bi, p, i: (bi, p))],
        out_specs=pl.BlockSpec((tq, LANES), lambda bi, p, i: (bi * per_b + i, p)),
        out_shape=jax.ShapeDtypeStruct((nt, FOX_W), BF16),
        compiler_params=_cparams(("parallel", "parallel", "parallel")),
        name="fox_attn",
    )(qq, kq, v)


def _rope_tables(seq):
    inv = 1.0 / (ROPE_THETA ** (jnp.arange(0, HEAD_DIM, 2, dtype=F32) / HEAD_DIM))
    ang = jnp.arange(seq, dtype=F32)[:, None] * inv[None, :]
    return jnp.tile(jnp.cos(ang), (1, DIL_HEADS)), jnp.tile(jnp.sin(ang), (1, DIL_HEADS))


def _rotary_layout(w):
    d = w.shape[0]
    w4 = w.reshape(d, DIL_HEADS, 2, HEAD_DIM // 2)
    return w4.transpose(0, 2, 1, 3).reshape(d, GROUP_W)


def _group_weights(w_in, g, with_mem):
    base = g * 3 * GROUP_W
    cols = [_rotary_layout(w_in[:, base:base + GROUP_W] * ATTN_SCALE),
            _rotary_layout(w_in[:, base + GROUP_W:base + 2 * GROUP_W]),
            w_in[:, base + 2 * GROUP_W:base + 3 * GROUP_W]]
    if with_mem:
        cols.append(w_in[:, DIL_QKV_W:] * ATTN_SCALE)
    return jnp.concatenate(cols, axis=1).astype(BF16)


def kernel(x, mem, w_in_a, w_out_a, w_in_b, w_out_b, w_shared_kvf, b_forget, w_mem_kv, ln_mix_g, ln_mix_b,
           ln_ffn_g, ln_ffn_b, w_router, b_router, w_gate_up, b_gate_up, w_down, b_down):
    b, s, d = x.shape
    nt = b * s
    mem2 = mem.reshape(b * mem.shape[1], d)
    cos_t, sin_t = _rope_tables(s)
    moe = functools.partial(_moe_layer, w_router=w_router, b_router=b_router, w_gate_up=w_gate_up,
                            b_gate_up=b_gate_up, w_down=w_down, b_down=b_down)

    outs, lses, q_mem = [], [], None
    for g, (_, dil) in enumerate(DILATED_PATTERNS):
        res = _group_proj(x, _group_weights(w_in_a[0], g, g == 0), cos_t, sin_t, dil)
        if g == 0:
            q_mem = res[3].reshape(nt, MEM_W)
        o, lse = _dilated_attn(res[0], res[1], res[2], dil)
        outs.append(o)
        lses.append(lse)
    mkv0 = _matmul(mem2, w_mem_kv[0].astype(BF16), BF16, 512)
    memo = _mem_attn(q_mem, mkv0, b)
    x2 = x.reshape(nt, d)
    x2 = _out_proj(_out_proj_a_kernel, outs + lses + [memo], x2, w_out_a[0].astype(BF16),
                   ln_mix_g[0], ln_mix_b[0], "out_proj_a")
    x2 = moe(x2, 0, ln_g=ln_ffn_g[0], ln_b=ln_ffn_b[0])

    kq, v_sh, qq, q_mem = _shared_proj(x2, b, w_shared_kvf, b_forget, w_in_b[0])

    fox = _fox_attn(qq, kq, v_sh, b)
    mkv1 = _matmul(mem2, w_mem_kv[1].astype(BF16), BF16, 512)
    memo = _mem_attn(q_mem, mkv1, b)
    x2 = _out_proj(_out_proj_b_kernel, [fox, memo], x2, w_out_b[0].astype(BF16),
                   ln_mix_g[1], ln_mix_b[1], "out_proj_b")
    x2 = moe(x2, 1, ln_g=ln_ffn_g[1], ln_b=ln_ffn_b[1])
    return x2.reshape(b, s, d)
```

```python
import functools

import jax
import jax.numpy as jnp
from jax import lax
from jax.experimental import pallas as pl
from jax.experimental.pallas import tpu as pltpu

F32 = jnp.float32
BF16 = jnp.bfloat16

D_MODEL = 1024
DEPTH = 2
HEAD_DIM = 64
N_MEM_HEADS = 4
DILATED_PATTERNS = ((128, 1), (512, 4), (2048, 16))
N_DIL_GROUPS = 3
DIL_HEADS = 4
N_FOX_HEADS = 12
GROUP_W = DIL_HEADS * HEAD_DIM
MEM_W = N_MEM_HEADS * HEAD_DIM
FOX_W = N_FOX_HEADS * HEAD_DIM
DIL_QKV_W = N_DIL_GROUPS * 3 * GROUP_W
N_EXPERTS = 32
TOP_K = 4
D_FF = D_MODEL
SWIGLU_LIMIT = 7.0
SWIGLU_ALPHA = 1.702
ROPE_THETA = 10000.0
LN_EPS = 1e-5
ATTN_SCALE = HEAD_DIM ** -0.5
DEEPNORM_ALPHA = (2.0 * DEPTH) ** 0.25
DIL_STEPS = 128

LANES = 128
FOX_HEAD_W = LANES
FOX_QK_W = N_FOX_HEADS * FOX_HEAD_W
MOE_TM = 256
VMEM_LIMIT = 56 * 1024 * 1024


def _cparams(sem, vmem=None):
    return pltpu.CompilerParams(dimension_semantics=sem, vmem_limit_bytes=vmem)


def _head_mask(width, lo, hi):
    lane = lax.broadcasted_iota(jnp.int32, (1, width), 1)
    return (lane >= lo) & (lane < hi)


def _layer_norm_rows(y, g, b):
    mu = jnp.mean(y, axis=-1, keepdims=True)
    yc = y - mu
    var = jnp.mean(yc * yc, axis=-1, keepdims=True)
    return yc * lax.rsqrt(var + LN_EPS) * g + b


def _matmul_kernel(x_ref, w_ref, o_ref):
    o_ref[...] = jnp.dot(x_ref[...].astype(BF16), w_ref[...],
                         preferred_element_type=F32).astype(o_ref.dtype)


def _matmul(x, w, out_dtype, tm):
    m, k = x.shape
    n = w.shape[1]
    return pl.pallas_call(
        _matmul_kernel,
        grid=(m // tm,),
        in_specs=[pl.BlockSpec((tm, k), lambda i: (i, 0)),
                  pl.BlockSpec((k, n), lambda i: (0, 0))],
        out_specs=pl.BlockSpec((tm, n), lambda i: (i, 0)),
        out_shape=jax.ShapeDtypeStruct((m, n), out_dtype),
        compiler_params=_cparams(("parallel",)),
        name="matmul",
    )(x, w)


def _group_proj_kernel(x_ref, w_ref, cos_ref, sin_ref, *o_refs):
    n_res = o_refs[0].shape[1]
    d = x_ref.shape[2] // n_res
    half = GROUP_W // 2
    for r in range(n_res):
        h = jnp.dot(x_ref[0, :, r * d:(r + 1) * d].astype(BF16), w_ref[...], preferred_element_type=F32)
        cos = cos_ref[:, r * LANES:(r + 1) * LANES]
        sin = sin_ref[:, r * LANES:(r + 1) * LANES]
        for j in range(2):
            t1 = h[:, j * GROUP_W:j * GROUP_W + half]
            t2 = h[:, j * GROUP_W + half:(j + 1) * GROUP_W]
            o_refs[j][0, r, :, :half] = (t1 * cos - t2 * sin).astype(BF16)
            o_refs[j][0, r, :, half:] = (t2 * cos + t1 * sin).astype(BF16)
        for j in range(2, len(o_refs)):
            o_refs[j][0, r] = h[:, j * GROUP_W:(j + 1) * GROUP_W].astype(BF16)


PROJ_ROWS = 512


def _group_proj(x, w, cos_t, sin_t, dil):
    b, s, d = x.shape
    length = s // dil
    lt = min(length, PROJ_ROWS)
    n_res = min(PROJ_ROWS // lt, dil)
    n_out = w.shape[1] // GROUP_W
    xv = x.reshape(b, length, dil * d)
    cv = cos_t.reshape(length, dil * LANES)
    sv = sin_t.reshape(length, dil * LANES)
    o_spec = pl.BlockSpec((1, n_res, lt, GROUP_W), lambda bi, r, l: (bi, r, l, 0))
    return pl.pallas_call(
        _group_proj_kernel,
        grid=(b, dil // n_res, length // lt),
        in_specs=[pl.BlockSpec((1, lt, n_res * d), lambda bi, r, l: (bi, l, r)),
                  pl.BlockSpec(w.shape, lambda bi, r, l: (0, 0)),
                  pl.BlockSpec((lt, n_res * LANES), lambda bi, r, l: (l, r)),
                  pl.BlockSpec((lt, n_res * LANES), lambda bi, r, l: (l, r))],
        out_specs=[o_spec] * n_out,
        out_shape=[jax.ShapeDtypeStruct((b, dil, length, GROUP_W), BF16)] * n_out,
        compiler_params=_cparams(("parallel", "parallel", "parallel")),
        name=f"group_proj_d{dil}",
    )(xv, w, cv, sv)


def _dilated_attn_kernel(q_ref, kp_ref, kc_ref, vp_ref, vc_ref, o_ref, lse_ref):
    n = pl.program_id(2)
    n_res = q_ref.shape[1]
    n_sub = q_ref.shape[2] // DIL_STEPS
    qi = lax.broadcasted_iota(jnp.int32, (DIL_STEPS, 2 * DIL_STEPS), 0)
    kj = lax.broadcasted_iota(jnp.int32, (DIL_STEPS, 2 * DIL_STEPS), 1)
    band = (kj >= qi) & (kj <= qi + DIL_STEPS)
    half = GROUP_W // 2
    hw = HEAD_DIM // 2
    for r in range(n_res):
        for j in range(n_sub):
            lo, hi = j * DIL_STEPS, (j + 1) * DIL_STEPS
            q = q_ref[0, r, lo:hi, :]
            if j == 0:
                kk = jnp.concatenate([kp_ref[0, r], kc_ref[0, r, lo:hi, :]], axis=0)
                vv = jnp.concatenate([vp_ref[0, r], vc_ref[0, r, lo:hi, :]], axis=0)
                valid = band & ((n > 0) | (kj >= DIL_STEPS))
            else:
                kk = kc_ref[0, r, lo - DIL_STEPS:hi, :]
                vv = vc_ref[0, r, lo - DIL_STEPS:hi, :]
                valid = band
            qs = []
            for h in range(DIL_HEADS):
                qmask = (_head_mask(GROUP_W, h * hw, (h + 1) * hw)
                         | _head_mask(GROUP_W, half + h * hw, half + (h + 1) * hw))
                qs.append(jnp.where(qmask, q, jnp.zeros_like(q)))
            sc = lax.dot_general(jnp.concatenate(qs, axis=0), kk, (((1,), (1,)), ((), ())),
                                 preferred_element_type=F32)
            sc = jnp.where(jnp.concatenate([valid] * DIL_HEADS, axis=0), sc, -jnp.inf)
            m = jnp.max(sc, axis=-1, keepdims=True)
            p = jnp.exp(sc - m)
            den = jnp.sum(p, axis=-1, keepdims=True)
            o_all = jnp.dot(p.astype(BF16), vv, preferred_element_type=F32) / den
            lse_all = m + jnp.log(den)
            out = jnp.zeros((DIL_STEPS, GROUP_W), F32)
            lse = jnp.zeros((DIL_STEPS, GROUP_W), F32)
            for h in range(DIL_HEADS):
                vmask = _head_mask(GROUP_W, h * HEAD_DIM, (h + 1) * HEAD_DIM)
                out = jnp.where(vmask, o_all[h * DIL_STEPS:(h + 1) * DIL_STEPS], out)
                lse = jnp.where(vmask, lse_all[h * DIL_STEPS:(h + 1) * DIL_STEPS], lse)
            o_ref[0, lo:hi, r * GROUP_W:(r + 1) * GROUP_W] = out.astype(BF16)
            lse_ref[0, lo:hi, r * GROUP_W:(r + 1) * GROUP_W] = lse


DIL_UNITS = 4


def _dilated_attn(q, k, v, dil):
    b, _, length, _ = q.shape
    n_sub = min(length // DIL_STEPS, DIL_UNITS)
    n_res = DIL_UNITS // n_sub
    rows = n_sub * DIL_STEPS
    cur = pl.BlockSpec((1, n_res, rows, GROUP_W), lambda bi, r, n: (bi, r, n, 0))
    prev = pl.BlockSpec((1, n_res, DIL_STEPS, GROUP_W),
                        lambda bi, r, n: (bi, r, jnp.maximum(n * n_sub - 1, 0), 0))
    out = pl.BlockSpec((1, rows, n_res * GROUP_W), lambda bi, r, n: (bi, n, r))
    o, lse = pl.pallas_call(
        _dilated_attn_kernel,
        grid=(b, dil // n_res, length // rows),
        in_specs=[cur, prev, cur, prev, cur],
        out_specs=[out, out],
        out_shape=[jax.ShapeDtypeStruct((b, length, dil * GROUP_W), BF16),
                   jax.ShapeDtypeStruct((b, length, dil * GROUP_W), F32)],
        compiler_params=_cparams(("parallel", "parallel", "parallel")),
        name=f"dilated_attn_d{dil}",
    )(q, k, k, v, v)
    return o.reshape(b * length * dil, GROUP_W), lse.reshape(b * length * dil, GROUP_W)


def _mem_attn_kernel(q_ref, kv_ref, o_ref):
    q = q_ref[...]
    mk = kv_ref[:, :MEM_W]
    mv = kv_ref[:, MEM_W:]
    out = jnp.zeros(q.shape, F32)
    for h in range(N_MEM_HEADS):
        hmask = _head_mask(MEM_W, h * HEAD_DIM, (h + 1) * HEAD_DIM)
        qm = jnp.where(hmask, q, jnp.zeros_like(q))
        sc = lax.dot_general(qm, mk, (((1,), (1,)), ((), ())), preferred_element_type=F32)
        m = jnp.max(sc, axis=-1, keepdims=True)
        p = jnp.exp(sc - m)
        den = jnp.sum(p, axis=-1, keepdims=True)
        o_all = jnp.dot(p.astype(BF16), mv, preferred_element_type=F32)
        out = jnp.where(hmask, o_all / den, out)
    o_ref[...] = out.astype(BF16)


def _mem_attn(q, mkv, batch, tq=512):
    nt = q.shape[0]
    per_b = nt // batch // tq
    m_len = mkv.shape[0] // batch
    return pl.pallas_call(
        _mem_attn_kernel,
        grid=(batch, per_b),
        in_specs=[pl.BlockSpec((tq, MEM_W), lambda bi, i: (bi * per_b + i, 0)),
                  pl.BlockSpec((m_len, 2 * MEM_W), lambda bi, i: (bi, 0))],
        out_specs=pl.BlockSpec((tq, MEM_W), lambda bi, i: (bi * per_b + i, 0)),
        out_shape=jax.ShapeDtypeStruct((nt, MEM_W), BF16),
        compiler_params=_cparams(("parallel", "parallel")),
        name="mem_attn",
    )(q, mkv)


def _out_proj_a_kernel(o0, o1, o2, l0, l1, l2, memo, x_ref, w_ref, g_ref, b_ref, out_ref):
    la, lb, lc = l0[...], l1[...], l2[...]
    mx = jnp.maximum(jnp.maximum(la, lb), lc)
    ea, eb, ec = jnp.exp(la - mx), jnp.exp(lb - mx), jnp.exp(lc - mx)
    z = ea + eb + ec
    dil = (ea * o0[...].astype(F32) + eb * o1[...].astype(F32) + ec * o2[...].astype(F32)) / z
    mix = jnp.dot(dil.astype(BF16), w_ref[:GROUP_W, :], preferred_element_type=F32)
    mix += jnp.dot(memo[...], w_ref[GROUP_W:, :], preferred_element_type=F32)
    y = DEEPNORM_ALPHA * x_ref[...] + mix
    out_ref[...] = _layer_norm_rows(y, g_ref[...], b_ref[...])


def _out_proj_b_kernel(fox, memo, x_ref, w_ref, g_ref, b_ref, out_ref):
    mix = jnp.dot(fox[...], w_ref[:FOX_W, :], preferred_element_type=F32)
    mix += jnp.dot(memo[...], w_ref[FOX_W:, :], preferred_element_type=F32)
    y = DEEPNORM_ALPHA * x_ref[...] + mix
    out_ref[...] = _layer_norm_rows(y, g_ref[...], b_ref[...])


def _out_proj(kernel_fn, acts, x, w, g, b, name, tm=512):
    nt, d = x.shape
    row = lambda i: (i, 0)
    const = lambda i: (0, 0)
    return pl.pallas_call(
        kernel_fn,
        grid=(nt // tm,),
        in_specs=[pl.BlockSpec((tm, a.shape[1]), row) for a in acts]
        + [pl.BlockSpec((tm, d), row), pl.BlockSpec(w.shape, const),
           pl.BlockSpec((1, d), const), pl.BlockSpec((1, d), const)],
        out_specs=pl.BlockSpec((tm, d), row),
        out_shape=jax.ShapeDtypeStruct((nt, d), F32),
        compiler_params=_cparams(("parallel",)),
        name=name,
    )(*acts, x, w, g.reshape(1, d), b.reshape(1, d))


def _router_kernel(x_ref, wt_ref, b_ref, idx_ref, rank_ref, gate_ref, cnt_ref, carry_ref):
    @pl.when(pl.program_id(0) == 0)
    def _():
        carry_ref[...] = jnp.zeros_like(carry_ref)

    tm = x_ref.shape[0]
    logits = lax.dot_general(wt_ref[...], x_ref[...], (((1,), (1,)), ((), ())),
                             precision=lax.Precision.HIGHEST, preferred_element_type=F32) + b_ref[...]
    sub = lax.broadcasted_iota(jnp.int32, (N_EXPERTS, tm), 0).astype(F32)
    work = logits
    vals, sels, idxs = [], [], []
    for _ in range(TOP_K):
        mk = jnp.max(work, axis=0, keepdims=True)
        ik = jnp.min(jnp.where(work == mk, sub, float(N_EXPERTS)), axis=0, keepdims=True)
        sel = sub == ik
        work = jnp.where(sel, -jnp.inf, work)
        vals.append(mk)
        sels.append(sel)
        idxs.append(ik)
    es = [jnp.exp(v - vals[0]) for v in vals]
    z = es[0] + es[1] + es[2] + es[3]
    hot = jnp.zeros((N_EXPERTS, tm), F32)
    for sel in sels:
        hot = jnp.where(sel, 1.0, hot)
    row = lax.broadcasted_iota(jnp.int32, (tm, tm), 0)
    col = lax.broadcasted_iota(jnp.int32, (tm, tm), 1)
    tri = jnp.where(row < col, 1.0, 0.0).astype(BF16)
    before = jnp.dot(hot.astype(BF16), tri, preferred_element_type=F32) + carry_ref[...]
    k_sub = lax.broadcasted_iota(jnp.int32, (TOP_K, tm), 0)
    g_sub = lax.broadcasted_iota(jnp.int32, (LANES, tm), 0)
    idx_o = jnp.zeros((TOP_K, tm), F32)
    rank_o = jnp.zeros((TOP_K, tm), F32)
    gate_t = jnp.zeros((LANES, tm), F32)
    for k in range(TOP_K):
        rk = jnp.sum(jnp.where(sels[k], before, 0.0), axis=0, keepdims=True)
        idx_o = jnp.where(k_sub == k, idxs[k], idx_o)
        rank_o = jnp.where(k_sub == k, rk, rank_o)
        gate_t = jnp.where(g_sub == k, es[k] / z, gate_t)
    idx_ref[...] = idx_o.astype(jnp.int32)
    rank_ref[...] = rank_o.astype(jnp.int32)
    gate_ref[...] = gate_t.T
    carry_ref[...] += jnp.sum(hot, axis=1, keepdims=True)
    cnt_ref[...] = carry_ref[...].astype(jnp.int32)


def _router(x, w_r, b_r, tm=512):
    nt, d = x.shape
    const = lambda i: (0, 0)
    return pl.pallas_call(
        _router_kernel,
        grid=(nt // tm,),
        in_specs=[pl.BlockSpec((tm, d), lambda i: (i, 0)), pl.BlockSpec((N_EXPERTS, d), const),
                  pl.BlockSpec((N_EXPERTS, 1), const)],
        out_specs=[pl.BlockSpec((TOP_K, tm), lambda i: (0, i)), pl.BlockSpec((TOP_K, tm), lambda i: (0, i)),
                   pl.BlockSpec((tm, LANES), lambda i: (i, 0)), pl.BlockSpec((N_EXPERTS, 1), const)],
        out_shape=[jax.ShapeDtypeStruct((TOP_K, nt), jnp.int32),
                   jax.ShapeDtypeStruct((TOP_K, nt), jnp.int32),
                   jax.ShapeDtypeStruct((nt, LANES), F32),
                   jax.ShapeDtypeStruct((N_EXPERTS, 1), jnp.int32)],
        scratch_shapes=[pltpu.VMEM((N_EXPERTS, 1), F32)],
        compiler_params=_cparams(("arbitrary",)),
        name="router",
    )(x, w_r.T, b_r.reshape(N_EXPERTS, 1))


def _row_copy(src, dst, sem):
    return pltpu.make_async_copy(src, dst, sem)


ROW_ISSUE_UNROLL = 4


def _dispatch_kernel(dest_ref, tail_ref, x_ref, xs_ref, zeros_ref, sem, zsem):
    tm = x_ref.shape[0]
    nt = tm * pl.num_programs(0)
    base = pl.program_id(0) * tm

    def tail_copy(e):
        t = pl.multiple_of(jnp.maximum(tail_ref[e], 0), MOE_TM)
        return _row_copy(zeros_ref, xs_ref.at[pl.ds(t, MOE_TM), :], zsem)

    def unused_copy(j):
        return _row_copy(zeros_ref, xs_ref.at[pl.ds(pl.multiple_of(j * MOE_TM, MOE_TM), MOE_TM), :], zsem)

    @pl.when(pl.program_id(0) == 0)
    def _():
        zeros_ref[...] = jnp.zeros_like(zeros_ref)
        n_blocks = xs_ref.shape[0] // MOE_TM
        first_unused = tail_ref[N_EXPERTS]
        for e in range(N_EXPERTS):
            pl.when(tail_ref[e] >= 0)(lambda e=e: tail_copy(e).start())
        lax.fori_loop(first_unused, n_blocks, lambda j, c: (unused_copy(j).start(), c)[1], 0)
        for e in range(N_EXPERTS):
            pl.when(tail_ref[e] >= 0)(lambda e=e: tail_copy(e).wait())
        lax.fori_loop(first_unused, n_blocks, lambda j, c: (unused_copy(j).wait(), c)[1], 0)

    def issue(i, carry):
        for k in range(TOP_K):
            d = dest_ref[k * nt + base + i]
            _row_copy(x_ref.at[pl.ds(i, 1), :], xs_ref.at[pl.ds(d, 1), :], sem).start(priority=k % 2)
        return carry

    lax.fori_loop(0, tm, issue, 0, unroll=ROW_ISSUE_UNROLL)
    for _ in range(TOP_K):
        _row_copy(x_ref, xs_ref.at[pl.ds(0, tm), :], sem).wait()


def _dispatch(dest_flat, tail, x, rows, tm=256):
    nt, d = x.shape
    return pl.pallas_call(
        _dispatch_kernel,
        grid_spec=pltpu.PrefetchScalarGridSpec(
            num_scalar_prefetch=2,
            grid=(nt // tm,),
            in_specs=[pl.BlockSpec((tm, d), lambda i, dest, tail: (i, 0))],
            out_specs=pl.BlockSpec(memory_space=pl.ANY),
            scratch_shapes=[pltpu.VMEM((MOE_TM, d), x.dtype),
                            pltpu.SemaphoreType.DMA(()), pltpu.SemaphoreType.DMA(())],
        ),
        out_shape=jax.ShapeDtypeStruct((rows, d), x.dtype),
        compiler_params=_cparams(("arbitrary",)),
        name="moe_dispatch",
    )(dest_flat, tail, x)


ROW_RING = 3


def _expert_kernel(exp_ref, first_ref, active_ref, next_ref, slot_ref, xsrc_ref,
                   bgu_ref, bdn_ref, wgu_hbm, wdn_hbm, xs_hbm,
                   y_ref, wgu_f32, wdn_f32, wgu_bf, wdn_bf, xbuf, sem, xsem, *, layer):
    i = pl.program_id(0)
    n_steps = pl.num_programs(0)

    def row_copy(j):
        start = pl.multiple_of(xsrc_ref[j] * MOE_TM, MOE_TM)
        return pltpu.make_async_copy(xs_hbm.at[pl.ds(start, MOE_TM), :], xbuf.at[j % ROW_RING],
                                     xsem.at[j % ROW_RING])

    @pl.when(i == 0)
    def _():
        for j in range(ROW_RING - 1):
            row_copy(j).start()

    pl.when(i + ROW_RING - 1 < n_steps)(lambda: row_copy(i + ROW_RING - 1).start())

    def weight_copies(e, s):
        return (pltpu.make_async_copy(wgu_hbm.at[layer, e], wgu_f32.at[s], sem.at[0, s]),
                pltpu.make_async_copy(wdn_hbm.at[layer, e], wdn_f32.at[s], sem.at[1, s]))

    @pl.when(i == 0)
    def _():
        for c in weight_copies(exp_ref[0], slot_ref[0]):
            c.start()

    @pl.when(first_ref[i] == 1)
    def _():
        s = slot_ref[i]
        for c in weight_copies(exp_ref[i], s):
            c.wait()

        @pl.when(next_ref[i] >= 0)
        def _():
            for c in weight_copies(next_ref[i], 1 - s):
                c.start()

        wgu_bf[...] = wgu_f32[s].astype(BF16)
        wdn_bf[...] = wdn_f32[s].astype(BF16)

    row_copy(i).wait()

    @pl.when(active_ref[i] == 1)
    def _():
        hb = jnp.dot(xbuf[i % ROW_RING].astype(BF16), wgu_bf[...], preferred_element_type=F32) + bgu_ref[...]
        gate = jnp.minimum(hb[:, :D_FF], SWIGLU_LIMIT)
        up = jnp.clip(hb[:, D_FF:], -SWIGLU_LIMIT, SWIGLU_LIMIT)
        act = (up + 1.0) * gate * jax.nn.sigmoid(SWIGLU_ALPHA * gate)
        y_ref[...] = jnp.dot(act.astype(BF16), wdn_bf[...], preferred_element_type=F32) + bdn_ref[...]

    @pl.when(active_ref[i] == 0)
    def _():
        y_ref[...] = jnp.zeros_like(y_ref)


def _experts(blocks, xs, w_gu, b_gu, w_dn, b_dn, layer):
    rows = xs.shape[0]
    d = w_dn.shape[-1]
    n_blocks = rows // MOE_TM
    return pl.pallas_call(
        functools.partial(_expert_kernel, layer=layer),
        grid_spec=pltpu.PrefetchScalarGridSpec(
            num_scalar_prefetch=6,
            grid=(n_blocks,),
            in_specs=[
                pl.BlockSpec((None, None, 1, 2 * D_FF), lambda i, e, f, a, n, s, x: (layer, e[i], 0, 0)),
                pl.BlockSpec((None, None, 1, d), lambda i, e, f, a, n, s, x: (layer, e[i], 0, 0)),
                pl.BlockSpec(memory_space=pl.ANY),
                pl.BlockSpec(memory_space=pl.ANY),
                pl.BlockSpec(memory_space=pl.ANY),
            ],
            out_specs=pl.BlockSpec((MOE_TM, d), lambda i, e, f, a, n, s, x: (i, 0)),
            scratch_shapes=[pltpu.VMEM((2, d, 2 * D_FF), F32), pltpu.VMEM((2, D_FF, d), F32),
                            pltpu.VMEM((d, 2 * D_FF), BF16), pltpu.VMEM((D_FF, d), BF16),
                            pltpu.VMEM((ROW_RING, MOE_TM, d), F32),
                            pltpu.SemaphoreType.DMA((2, 2)), pltpu.SemaphoreType.DMA((ROW_RING,))],
        ),
        out_shape=jax.ShapeDtypeStruct((rows, d), F32),
        compiler_params=_cparams(("arbitrary",), VMEM_LIMIT),
        name="moe_experts",
    )(*blocks, b_gu.reshape(DEPTH, N_EXPERTS, 1, 2 * D_FF), b_dn.reshape(DEPTH, N_EXPERTS, 1, d),
      w_gu, w_dn, xs)


def _combine_kernel(dest_ref, x_ref, gate_ref, g_ref, b_ref, y_ref, out_ref, buf, sem):
    tm = x_ref.shape[0]
    step = pl.program_id(0)
    n_steps = pl.num_programs(0)
    nt = tm * n_steps
    slot = step % 2

    def gather_rows(s, to_slot):
        def issue(i, carry):
            for k in range(TOP_K):
                d = dest_ref[k * nt + s * tm + i]
                _row_copy(y_ref.at[pl.ds(d, 1), :], buf.at[to_slot, k, pl.ds(i, 1), :],
                          sem.at[to_slot]).start(priority=k % 2)
            return carry

        lax.fori_loop(0, tm, issue, 0, unroll=ROW_ISSUE_UNROLL)

    def consume(cur):
        pl.when(step + 1 < n_steps)(lambda: gather_rows(step + 1, 1 - cur))
        for k in range(TOP_K):
            _row_copy(y_ref.at[pl.ds(0, tm), :], buf.at[cur, k], sem.at[cur]).wait()
        gates = gate_ref[...]
        ffn = gates[:, 0:1] * buf[cur, 0]
        for k in range(1, TOP_K):
            ffn += gates[:, k:k + 1] * buf[cur, k]
        y = DEEPNORM_ALPHA * x_ref[...] + ffn
        out_ref[...] = _layer_norm_rows(y, g_ref[...], b_ref[...])

    pl.when(step == 0)(lambda: gather_rows(0, 0))
    for cur in range(2):
        pl.when(slot == cur)(functools.partial(consume, cur))


def _combine(dest_flat, x, gates, g, b, y_rows, tm=256):
    nt, d = x.shape
    return pl.pallas_call(
        _combine_kernel,
        grid_spec=pltpu.PrefetchScalarGridSpec(
            num_scalar_prefetch=1,
            grid=(nt // tm,),
            in_specs=[pl.BlockSpec((tm, d), lambda i, dest: (i, 0)),
                      pl.BlockSpec((tm, LANES), lambda i, dest: (i, 0)),
                      pl.BlockSpec((1, d), lambda i, dest: (0, 0)),
                      pl.BlockSpec((1, d), lambda i, dest: (0, 0)),
                      pl.BlockSpec(memory_space=pl.ANY)],
            out_specs=pl.BlockSpec((tm, d), lambda i, dest: (i, 0)),
            scratch_shapes=[pltpu.VMEM((2, TOP_K, tm, d), F32), pltpu.SemaphoreType.DMA((2,))],
        ),
        out_shape=jax.ShapeDtypeStruct((nt, d), F32),
        compiler_params=_cparams(("arbitrary",), VMEM_LIMIT),
        name="moe_combine",
    )(dest_flat, x, gates, g.reshape(1, d), b.reshape(1, d), y_rows)


def _moe_layer(x, layer, w_router, b_router, w_gate_up, b_gate_up, w_down, b_down, ln_g, ln_b):
    nt, _ = x.shape
    idx, rank, gates, counts = _router(x, w_router[layer], b_router[layer])
    counts = counts[:, 0]
    padded = (counts + MOE_TM - 1) // MOE_TM * MOE_TM
    pend = jnp.cumsum(padded)
    pstart = pend - padded
    experts = jnp.arange(N_EXPERTS, dtype=jnp.int32)
    start_of = jnp.sum(jnp.where(idx[..., None] == experts, pstart, 0), axis=-1)
    dest = (start_of + rank).reshape(TOP_K * nt).astype(jnp.int32)
    n_blocks = nt * TOP_K // MOE_TM + N_EXPERTS
    blk_start = jnp.arange(n_blocks, dtype=jnp.int32) * MOE_TM
    active = blk_start < pend[-1]
    exp_raw = jnp.minimum(jnp.sum(pend[None, :] <= blk_start[:, None], axis=1), N_EXPERTS - 1).astype(jnp.int32)
    last_exp = jnp.max(jnp.where(active, exp_raw, 0))
    block_exp = jnp.where(active, exp_raw, last_exp).astype(jnp.int32)
    prev_exp = jnp.concatenate([jnp.full((1,), -1, jnp.int32), block_exp[:-1]])
    block_first = (active & (block_exp != prev_exp)).astype(jnp.int32)
    has = padded > 0
    slot_e = (jnp.cumsum(has.astype(jnp.int32)) - 1) % 2
    later = lax.cummin(jnp.where(has, experts, N_EXPERTS)[::-1])[::-1]
    next_e = jnp.concatenate([later[1:], jnp.full((1,), N_EXPERTS, jnp.int32)])
    next_e = jnp.where(next_e < N_EXPERTS, next_e, -1)
    n_active = pend[-1] // MOE_TM
    tail = jnp.concatenate([jnp.where(has, pend - MOE_TM, -1), n_active[None]]).astype(jnp.int32)
    blocks = (block_exp, block_first, active.astype(jnp.int32), next_e[block_exp].astype(jnp.int32),
              slot_e[block_exp].astype(jnp.int32),
              jnp.minimum(jnp.arange(n_blocks, dtype=jnp.int32), n_active - 1).astype(jnp.int32))
    xs = _dispatch(dest, tail, x, n_blocks * MOE_TM)
    y_rows = _experts(blocks, xs, w_gate_up, b_gate_up, w_down, b_down, layer)
    return _combine(dest, x, gates, ln_g, ln_b, y_rows)


def _split3(v):
    hi = v.astype(BF16)
    r1 = v - hi.astype(F32)
    mid = r1.astype(BF16)
    lo = (r1 - mid.astype(F32)).astype(BF16)
    return hi, mid, lo


def _shared_proj_kernel(x_ref, w_ref, wfh_ref, wfl_ref, bf_ref, eq_ref, ek_ref, oq_ref, ok_ref,
                        kq_ref, v_ref, qq_ref, qm_ref, carry_ref):
    @pl.when(pl.program_id(1) == 0)
    def _():
        carry_ref[...] = jnp.zeros_like(carry_ref)

    tm = x_ref.shape[0]
    x = x_ref[...]
    xh = x.astype(BF16)
    xl = (x - xh.astype(F32)).astype(BF16)
    h = jnp.dot(xh, w_ref[...], preferred_element_type=F32)
    f = (jnp.dot(xh, wfh_ref[...], preferred_element_type=F32)
         + jnp.dot(xl, wfh_ref[...], preferred_element_type=F32)
         + jnp.dot(xh, wfl_ref[...], preferred_element_type=F32)) + bf_ref[...]
    ls = jnp.minimum(f, 0.0) - jnp.log1p(jnp.exp(-jnp.abs(f)))
    ls = jnp.where(_head_mask(LANES, 0, N_FOX_HEADS), ls, 0.0)
    row = lax.broadcasted_iota(jnp.int32, (tm, tm), 0)
    col = lax.broadcasted_iota(jnp.int32, (tm, tm), 1)
    tri = jnp.where(row >= col, 1.0, 0.0).astype(BF16)
    a, b, c = _split3(ls)
    cum = (jnp.dot(tri, a, preferred_element_type=F32) + jnp.dot(tri, b, preferred_element_type=F32)
           + jnp.dot(tri, c, preferred_element_type=F32)) + carry_ref[...]
    carry_ref[...] = cum[tm - 1:tm, :]
    ch, cm, cl = _split3(cum)
    cat = (ch.astype(F32) + pltpu.roll(cm.astype(F32), 16, 1) + pltpu.roll(cl.astype(F32), 32, 1)).astype(BF16)
    kq = h[:, :FOX_QK_W] + jnp.dot(cat, ek_ref[...], preferred_element_type=F32) + ok_ref[...]
    qq = (h[:, FOX_QK_W + FOX_W:2 * FOX_QK_W + FOX_W]
          + jnp.dot(cat, eq_ref[...], preferred_element_type=F32) + oq_ref[...])
    kq_ref[...] = kq.astype(BF16)
    v_ref[...] = h[:, FOX_QK_W:FOX_QK_W + FOX_W].astype(BF16)
    qq_ref[...] = qq.astype(BF16)
    qm_ref[...] = h[:, 2 * FOX_QK_W + FOX_W:].astype(BF16)


def _widen_heads(w):
    d = w.shape[0]
    w3 = w.reshape(d, N_FOX_HEADS, HEAD_DIM)
    return jnp.concatenate([w3, jnp.zeros_like(w3)], axis=-1).reshape(d, FOX_QK_W)


def _spread_matrices():
    src = jnp.arange(LANES)[:, None]
    dst = jnp.arange(FOX_QK_W)[None, :]
    head = dst // FOX_HEAD_W
    off = dst % FOX_HEAD_W
    part = src // 16
    is_src = (src % 16 == head) & (src % 16 < N_FOX_HEADS) & (part < 3)
    eq = jnp.where(is_src & (off == HEAD_DIM + part), 1.0, 0.0).astype(BF16)
    ek = jnp.where(is_src & (off == HEAD_DIM + 3 + part), -1.0, 0.0).astype(BF16)
    off1 = jnp.arange(FOX_QK_W) % FOX_HEAD_W
    ones_q = ((off1 >= HEAD_DIM + 3) & (off1 < HEAD_DIM + 6)).astype(F32).reshape(1, FOX_QK_W)
    ones_k = ((off1 >= HEAD_DIM) & (off1 < HEAD_DIM + 3)).astype(F32).reshape(1, FOX_QK_W)
    return eq, ek, ones_q, ones_k


def _shared_proj(x, batch, w_shared_kvf, b_forget, w_in_b, tm=512):
    nt, d = x.shape
    per_b = nt // batch // tm
    w_k = _widen_heads(w_shared_kvf[:, :FOX_W])
    w_v = w_shared_kvf[:, FOX_W:2 * FOX_W]
    w_q = _widen_heads(w_in_b[:, :FOX_W] * ATTN_SCALE)
    w_qm = w_in_b[:, FOX_W:] * ATTN_SCALE
    w_big = jnp.concatenate([w_k, w_v, w_q, w_qm], axis=1).astype(BF16)
    w_f = jnp.pad(w_shared_kvf[:, 2 * FOX_W:], ((0, 0), (0, LANES - N_FOX_HEADS)))
    w_fh = w_f.astype(BF16)
    w_fl = (w_f - w_fh.astype(F32)).astype(BF16)
    b_f = jnp.pad(b_forget, (0, LANES - N_FOX_HEADS)).reshape(1, LANES)
    eq, ek, ones_q, ones_k = _spread_matrices()
    nbig = w_big.shape[1]
    row = lambda bi, i: (bi * per_b + i, 0)
    const = lambda bi, i: (0, 0)
    return pl.pallas_call(
        _shared_proj_kernel,
        grid=(batch, per_b),
        in_specs=[pl.BlockSpec((tm, d), row),
                  pl.BlockSpec((d, nbig), const),
                  pl.BlockSpec((d, LANES), const), pl.BlockSpec((d, LANES), const),
                  pl.BlockSpec((1, LANES), const),
                  pl.BlockSpec((LANES, FOX_QK_W), const), pl.BlockSpec((LANES, FOX_QK_W), const),
                  pl.BlockSpec((1, FOX_QK_W), const), pl.BlockSpec((1, FOX_QK_W), const)],
        out_specs=[pl.BlockSpec((tm, FOX_QK_W), row), pl.BlockSpec((tm, FOX_W), row),
                   pl.BlockSpec((tm, FOX_QK_W), row), pl.BlockSpec((tm, MEM_W), row)],
        out_shape=[jax.ShapeDtypeStruct((nt, FOX_QK_W), BF16), jax.ShapeDtypeStruct((nt, FOX_W), BF16),
                   jax.ShapeDtypeStruct((nt, FOX_QK_W), BF16), jax.ShapeDtypeStruct((nt, MEM_W), BF16)],
        scratch_shapes=[pltpu.VMEM((1, LANES), F32)],
        compiler_params=_cparams(("arbitrary", "arbitrary"), VMEM_LIMIT),
        name="shared_proj",
    )(x, w_big, w_fh, w_fl, b_f, eq, ek, ones_q, ones_k)


def _fox_kernel(q_ref, k_ref, v_ref, o_ref):
    tq = q_ref.shape[0]
    i = pl.program_id(2)
    row = lax.broadcasted_iota(jnp.int32, (tq, tq), 0)
    col = lax.broadcasted_iota(jnp.int32, (tq, tq), 1)

    def attend(kv):
        out = jnp.zeros((tq, LANES), F32)
        for hh in range(2):
            qh = q_ref[:, hh * FOX_HEAD_W:(hh + 1) * FOX_HEAD_W]
            kh = k_ref[:kv, hh * FOX_HEAD_W:(hh + 1) * FOX_HEAD_W]
            sc = lax.dot_general(qh, kh, (((1,), (1,)), ((), ())), preferred_element_type=F32)
            diag = jnp.where(col <= row, sc[:, kv - tq:], -jnp.inf)
            sc = diag if kv == tq else jnp.concatenate([sc[:, :kv - tq], diag], axis=1)
            m = jnp.max(sc, axis=-1, keepdims=True)
            p = jnp.exp(sc - m)
            den = jnp.sum(p, axis=-1, keepdims=True)
            acc = jnp.dot(p.astype(BF16), v_ref[:kv, :], preferred_element_type=F32)
            out = jnp.where(_head_mask(LANES, hh * HEAD_DIM, (hh + 1) * HEAD_DIM), acc / den, out)
        o_ref[...] = out.astype(BF16)

    for c in range(k_ref.shape[0] // tq):
        pl.when(i == c)(functools.partial(attend, (c + 1) * tq))


def _fox_attn(qq, kq, v, batch, tq=512):
    nt = qq.shape[0]
    s = nt // batch
    per_b = s // tq
    pairs = N_FOX_HEADS // 2
    return pl.pallas_call(
        _fox_kernel,
        grid=(batch, pairs, per_b),
        in_specs=[pl.BlockSpec((tq, 2 * FOX_HEAD_W), lambda bi, p, i: (bi * per_b + i, p)),
                  pl.BlockSpec((s, 2 * FOX_HEAD_W), lambda bi, p, i: (bi, p)),
                  pl.BlockSpec((s, LANES), lambda bi, p, i: (bi, p))],
        out_specs=pl.BlockSpec((tq, LANES), lambda bi, p, i: (bi * per_b + i, p)),
        out_shape=jax.ShapeDtypeStruct((nt, FOX_W), BF16),
        compiler_params=_cparams(("parallel", "parallel", "parallel")),
        name="fox_attn",
    )(qq, kq, v)


def _rope_tables(seq):
    inv = 1.0 / (ROPE_THETA ** (jnp.arange(0, HEAD_DIM, 2, dtype=F32) / HEAD_DIM))
    ang = jnp.arange(seq, dtype=F32)[:, None] * inv[None, :]
    return jnp.tile(jnp.cos(ang), (1, DIL_HEADS)), jnp.tile(jnp.sin(ang), (1, DIL_HEADS))


def _rotary_layout(w):
    d = w.shape[0]
    w4 = w.reshape(d, DIL_HEADS, 2, HEAD_DIM // 2)
    return w4.transpose(0, 2, 1, 3).reshape(d, GROUP_W)


def _group_weights(w_in, g, with_mem):
    base = g * 3 * GROUP_W
    cols = [_rotary_layout(w_in[:, base:base + GROUP_W] * ATTN_SCALE),
            _rotary_layout(w_in[:, base + GROUP_W:base + 2 * GROUP_W]),
            w_in[:, base + 2 * GROUP_W:base + 3 * GROUP_W]]
    if with_mem:
        cols.append(w_in[:, DIL_QKV_W:] * ATTN_SCALE)
    return jnp.concatenate(cols, axis=1).astype(BF16)


def kernel(x, mem, w_in_a, w_out_a, w_in_b, w_out_b, w_shared_kvf, b_forget, w_mem_kv, ln_mix_g, ln_mix_b,
           ln_ffn_g, ln_ffn_b, w_router, b_router, w_gate_up, b_gate_up, w_down, b_down):
    b, s, d = x.shape
    nt = b * s
    mem2 = mem.reshape(b * mem.shape[1], d)
    cos_t, sin_t = _rope_tables(s)
    moe = functools.partial(_moe_layer, w_router=w_router, b_router=b_router, w_gate_up=w_gate_up,
                            b_gate_up=b_gate_up, w_down=w_down, b_down=b_down)

    outs, lses, q_mem = [], [], None
    for g, (_, dil) in enumerate(DILATED_PATTERNS):
        res = _group_proj(x, _group_weights(w_in_a[0], g, g == 0), cos_t, sin_t, dil)
        if g == 0:
            q_mem = res[3].reshape(nt, MEM_W)
        o, lse = _dilated_attn(res[0], res[1], res[2], dil)
        outs.append(o)
        lses.append(lse)
    mkv0 = _matmul(mem2, w_mem_kv[0].astype(BF16), BF16, 512)
    memo = _mem_attn(q_mem, mkv0, b)
    x2 = x.reshape(nt, d)
    x2 = _out_proj(_out_proj_a_kernel, outs + lses + [memo], x2, w_out_a[0].astype(BF16),
                   ln_mix_g[0], ln_mix_b[0], "out_proj_a")
    x2 = moe(x2, 0, ln_g=ln_ffn_g[0], ln_b=ln_ffn_b[0])

    kq, v_sh, qq, q_mem = _shared_proj(x2, b, w_shared_kvf, b_forget, w_in_b[0])

    fox = _fox_attn(qq, kq, v_sh, b)
    mkv1 = _matmul(mem2, w_mem_kv[1].astype(BF16), BF16, 512)
    memo = _mem_attn(q_mem, mkv1, b)
    x2 = _out_proj(_out_proj_b_kernel, [fox, memo], x2, w_out_b[0].astype(BF16),
                   ln_mix_g[1], ln_mix_b[1], "out_proj_b")
    x2 = moe(x2, 1, ln_g=ln_ffn_g[1], ln_b=ln_ffn_b[1])
    return x2.reshape(b, s, d)
```
